```python
import jax
import jax.numpy as jnp
from jax import lax
import numpy as np

D_MODEL = 1024
BATCH = 8
SEQ = 2048
DEPTH = 4

N_MIXERS = 3
N_A = (DEPTH + 2) // 3
N_B = (DEPTH + 1) // 3
N_C = DEPTH // 3
NORM_EPS = 1e-6

A_INNER = 2 * D_MODEL
A_HEADS = 4
A_DH = A_INNER // A_HEADS
A_CONV = 4
A_CHUNK = 64
A_IN_COLS = 2 * A_INNER + 2 * A_HEADS

B_HEADS = 16
B_DH = D_MODEL // B_HEADS
B_GROUPS = 4
B_QPG = B_HEADS // B_GROUPS
B_KV = B_GROUPS * B_DH
B_CMP_LEN = 32
B_CMP_STRIDE = 16
B_PHI_HIDDEN = 256
B_SEL_LEN = 64
B_TOPN = 16
B_N_LOCAL = 2
B_WINDOW = 512
B_SEL_QBLK = 32
B_IN_COLS = B_HEADS * B_DH + 6 * B_KV + 3 * B_HEADS
FORCE_SCORE = 1e9

C_PATTERNS = ((128, 1), (512, 4), (2048, 16))
C_HEADS = 16
C_DH = D_MODEL // C_HEADS
C_GROUP_W = C_HEADS * C_DH
C_IN_COLS = 3 * len(C_PATTERNS) * C_GROUP_W

BAND_QBLK = 128

F_HIDDEN = 2816
F_CONV = 3

kernel_name = 'hybrid_mlstm_nsa_dilated_convffn'


def rmsnorm(x, g):
    xf = x.astype(jnp.float32)
    y = xf * lax.rsqrt(jnp.mean(xf * xf, axis=-1, keepdims=True) + NORM_EPS)
    return (y * g.astype(jnp.float32)).astype(x.dtype)


def causal_dwconv(x, w, b):
    k_w, ch = w.shape
    y = lax.conv_general_dilated(x, w[:, None, :].astype(x.dtype), (1,), [(k_w - 1, 0)],
                                 dimension_numbers=('NWC', 'WIO', 'NWC'), feature_group_count=ch)
    return y + b.astype(x.dtype)


def mlstm_chunkwise(q, k, v, log_i, log_f):
    bsz, nh, t_len, dh = q.shape
    L = min(A_CHUNK, t_len)
    nc = t_len // L

    def to_chunks(a):
        return jnp.moveaxis(a.reshape(bsz, nh, nc, L, *a.shape[3:]), 2, 0)

    causal = jnp.tril(jnp.ones((L, L), dtype=bool))

    def step(carry, inp):
        c_st, n_st, m_st = carry
        qc, kc, vc, lic, lfc = inp
        b = jnp.cumsum(lfc, axis=-1)
        log_d = jnp.where(causal, b[..., :, None] - b[..., None, :] + lic[..., None, :], -jnp.inf)
        log_inter = b + m_st[..., None]
        m = jnp.maximum(log_inter, jnp.max(log_d, axis=-1))
        w_intra = jnp.exp(log_d - m[..., None])
        w_inter = jnp.exp(log_inter - m)
        s = jnp.einsum('bhid,bhjd->bhij', qc, kc) * w_intra
        num = w_inter[..., None] * jnp.einsum('bhid,bhde->bhie', qc, c_st) + jnp.einsum('bhij,bhje->bhie', s, vc)
        den = w_inter * jnp.einsum('bhid,bhd->bhi', qc, n_st) + jnp.sum(s, axis=-1)
        h = num / jnp.maximum(jnp.abs(den), jnp.exp(-m))[..., None]
        b_last = b[..., -1]
        log_w = b_last[..., None] - b + lic
        m_new = jnp.maximum(b_last + m_st, jnp.max(log_w, axis=-1))
        decay = jnp.exp(b_last + m_st - m_new)
        w_k = jnp.exp(log_w - m_new[..., None])[..., None] * kc
        c_new = decay[..., None, None] * c_st + jnp.einsum('bhjd,bhje->bhde', w_k, vc)
        n_new = decay[..., None] * n_st + jnp.sum(w_k, axis=2)
        return (c_new, n_new, m_new), h

    init = (jnp.zeros((bsz, nh, dh, dh), jnp.float32), jnp.zeros((bsz, nh, dh), jnp.float32),
            jnp.zeros((bsz, nh), jnp.float32))
    _, hs = lax.scan(step, init, (to_chunks(q), to_chunks(k), to_chunks(v), to_chunks(log_i), to_chunks(log_f)))
    return jnp.moveaxis(hs, 0, 2).reshape(bsz, nh, t_len, dh)


def mlstm_mixer(h, w_in, b_gate, conv_w, conv_b, wq, wk, wv, gn, skip, w_out):
    bsz, t_len, _ = h.shape
    f32 = jnp.float32
    proj = h @ w_in
    x_m = proj[..., :A_INNER]
    o_pre = proj[..., A_INNER:2 * A_INNER]
    gates = (proj[..., 2 * A_INNER:] + b_gate).astype(f32)
    log_i = jnp.moveaxis(gates[..., :A_HEADS], -1, 1)
    log_f = jax.nn.log_sigmoid(jnp.moveaxis(gates[..., A_HEADS:], -1, 1))
    x_c = jax.nn.silu(causal_dwconv(x_m, conv_w, conv_b))
    xc_h = x_c.reshape(bsz, t_len, A_HEADS, A_DH).astype(f32)
    xm_h = x_m.reshape(bsz, t_len, A_HEADS, A_DH).astype(f32)
    q = jnp.einsum('bthd,hde->bhte', xc_h, wq.astype(f32))
    k = jnp.einsum('bthd,hde->bhte', xc_h, wk.astype(f32)) * A_DH ** -0.5
    v = jnp.einsum('bthd,hde->bhte', xm_h, wv.astype(f32))
    h_t = jnp.moveaxis(mlstm_chunkwise(q, k, v, log_i, log_f), 1, 2)
    mu = jnp.mean(h_t, axis=-1, keepdims=True)
    var = jnp.mean(jnp.square(h_t - mu), axis=-1, keepdims=True)
    h_n = ((h_t - mu) * lax.rsqrt(var + NORM_EPS)).reshape(bsz, t_len, A_INNER) * gn.astype(f32)
    y = jax.nn.sigmoid(o_pre.astype(f32)) * h_n + skip.astype(f32) * x_c.astype(f32)
    return y.astype(h.dtype) @ w_out


def banded_attention(q, k, v, max_dist):
    bsz, t_len, g, r, dh = q.shape
    qblk = min(BAND_QBLK, t_len)
    nb = t_len // qblk
    n_prev = min(-(-max_dist // qblk), nb - 1)
    span = (n_prev + 1) * qblk
    pad = ((0, 0), (n_prev * qblk, 0), (0, 0), (0, 0))
    kp = jnp.pad(k, pad)
    vp = jnp.pad(v, pad)

    def one_block(bidx):
        start = bidx * qblk
        qb = lax.dynamic_slice_in_dim(q, start, qblk, axis=1)
        kb = lax.dynamic_slice_in_dim(kp, start, span, axis=1)
        vb = lax.dynamic_slice_in_dim(vp, start, span, axis=1)
        tq = start + jnp.arange(qblk)
        sk = start - n_prev * qblk + jnp.arange(span)
        diff = tq[:, None] - sk[None, :]
        mask = (diff >= 0) & (diff <= max_dist) & (sk >= 0)[None, :]
        s = jnp.einsum('bqgrd,bkgd->bgrqk', qb, kb).astype(jnp.float32)
        s = jnp.where(mask, s, -jnp.inf)
        m = jnp.max(s, axis=-1, keepdims=True)
        e = jnp.exp(s - m)
        den = jnp.sum(e, axis=-1, keepdims=True)
        o = jnp.einsum('bgrqk,bkgd->bqgrd', (e / den).astype(v.dtype), vb)
        return o, (m + jnp.log(den))[..., 0]

    o, lse = lax.map(one_block, jnp.arange(nb))
    o = jnp.moveaxis(o, 0, 1).reshape(bsz, t_len, g, r, dh)
    lse = jnp.transpose(lse, (1, 0, 4, 2, 3)).reshape(bsz, t_len, g, r)
    return o, lse


def nsa_compress(kv, pe, w1, b1, w2):
    bsz, t_len, g, dh = kv.shape
    ncmp = (t_len - B_CMP_LEN) // B_CMP_STRIDE + 1
    idx = jnp.arange(ncmp)[:, None] * B_CMP_STRIDE + jnp.arange(B_CMP_LEN)[None, :]
    blocks = kv[:, idx] + pe[None, None, :, None, :]
    flat = jnp.moveaxis(blocks, 3, 2).reshape(bsz, ncmp, g, B_CMP_LEN * dh)
    return jax.nn.silu(flat @ w1 + b1) @ w2


def nsa_selected(q, k, v, top_idx, top_ok):
    bsz, t_len, g, r, dh = q.shape
    n_top = top_idx.shape[-1]
    qblk = min(B_SEL_QBLK, t_len)
    nqb = t_len // qblk
    kb = jnp.moveaxis(k.reshape(bsz, t_len // B_SEL_LEN, B_SEL_LEN, g, dh), 3, 1)
    vb = jnp.moveaxis(v.reshape(bsz, t_len // B_SEL_LEN, B_SEL_LEN, g, dh), 3, 1)
    bi = jnp.arange(bsz)[:, None, None, None]
    gi = jnp.arange(g)[None, :, None, None]
    off = jnp.arange(B_SEL_LEN)

    def one_block(args):
        qb, ib, okb, tq = args
        kg = kb[bi, gi, ib]
        vg = vb[bi, gi, ib]
        kpos = ib[..., None] * B_SEL_LEN + off
        mask = (kpos <= tq[None, None, :, None, None]) & okb[..., None]
        s = jnp.einsum('bqgrd,bgqnkd->bgrqnk', qb, kg).astype(jnp.float32)
        s = jnp.where(mask[:, :, None], s, -jnp.inf).reshape(bsz, g, r, qblk, n_top * B_SEL_LEN)
        p = jax.nn.softmax(s, axis=-1).reshape(bsz, g, r, qblk, n_top, B_SEL_LEN)
        return jnp.einsum('bgrqnk,bgqnkd->bqgrd', p.astype(v.dtype), vg)

    qs = jnp.moveaxis(q.reshape(bsz, nqb, qblk, g, r, dh), 1, 0)
    iss = jnp.moveaxis(top_idx.reshape(bsz, g, nqb, qblk, n_top), 2, 0)
    oks = jnp.moveaxis(top_ok.reshape(bsz, g, nqb, qblk, n_top), 2, 0)
    tqs = jnp.arange(t_len).reshape(nqb, qblk)
    o = lax.map(one_block, (qs, iss, oks, tqs))
    return jnp.moveaxis(o, 0, 1).reshape(bsz, t_len, g, r, dh)


def nsa_mixer(h, w_in, b_gate, pe_k, pe_v, phik_w1, phik_b1, phik_w2, phiv_w1, phiv_b1, phiv_w2, w_out):
    bsz, t_len, _ = h.shape
    f32 = jnp.float32
    nq = B_HEADS * B_DH
    proj = h @ w_in
    q = proj[..., :nq].reshape(bsz, t_len, B_GROUPS, B_QPG, B_DH) * B_DH ** -0.5
    kv = proj[..., nq:nq + 6 * B_KV].reshape(bsz, t_len, 6, B_GROUPS, B_DH)
    k_cmp, v_cmp, k_sel, v_sel, k_win, v_win = [kv[:, :, i] for i in range(6)]
    gate = jax.nn.sigmoid(proj[..., nq + 6 * B_KV:] + b_gate).reshape(bsz, t_len, B_GROUPS, B_QPG, 3)

    kc = nsa_compress(k_cmp, pe_k, phik_w1, phik_b1, phik_w2)
    vc = nsa_compress(v_cmp, pe_v, phiv_w1, phiv_b1, phiv_w2)
    ncmp = kc.shape[1]
    tpos = jnp.arange(t_len)
    cmp_end = jnp.arange(ncmp) * B_CMP_STRIDE + B_CMP_LEN - 1
    cmask = cmp_end[None, :] <= tpos[:, None]
    s = jnp.einsum('btgrd,bngd->bgrtn', q, kc).astype(f32)
    s = jnp.where(cmask, s, -jnp.inf)
    m = jnp.max(s, axis=-1, keepdims=True)
    m = jnp.where(jnp.isfinite(m), m, 0.0)
    e = jnp.exp(s - m)
    den = jnp.sum(e, axis=-1, keepdims=True)
    p_cmp = e / jnp.where(den > 0, den, 1.0)
    o_cmp = jnp.einsum('bgrtn,bngd->btgrd', p_cmp.astype(h.dtype), vc)

    nsel = t_len // B_SEL_LEN
    n_top = min(B_TOPN, nsel)
    c_start = jnp.arange(ncmp)[:, None] * B_CMP_STRIDE
    s_start = jnp.arange(nsel)[None, :] * B_SEL_LEN
    overlap = ((c_start < s_start + B_SEL_LEN) & (c_start + B_CMP_LEN > s_start)).astype(f32)
    imp = jnp.einsum('bgrtn,ns->bgts', p_cmp, overlap)
    blk = jnp.arange(nsel)[None, :]
    cur = (tpos // B_SEL_LEN)[:, None]
    valid = blk <= cur
    forced = ((blk == 0) | (cur - blk < B_N_LOCAL)) & valid
    score = jnp.where(forced, FORCE_SCORE, jnp.where(valid, imp, -FORCE_SCORE))
    top_val, top_idx = lax.top_k(score, n_top)
    top_ok = top_val > -0.5 * FORCE_SCORE
    o_sel = nsa_selected(q, k_sel, v_sel, top_idx, top_ok)

    o_win, _ = banded_attention(q, k_win, v_win, B_WINDOW - 1)

    o = gate[..., 0:1] * o_cmp + gate[..., 1:2] * o_sel + gate[..., 2:3] * o_win
    return o.reshape(bsz, t_len, nq) @ w_out


def dilated_mixer(h, w_in, w_out):
    bsz, t_len, _ = h.shape
    proj = (h @ w_in).reshape(bsz, t_len, len(C_PATTERNS), 3, C_HEADS, C_DH)
    outs = []
    lses = []
    for gidx, (window, dil) in enumerate(C_PATTERNS):
        sub = t_len // dil

        def by_residue(a):
            return a.reshape(bsz, sub, dil, C_HEADS, C_DH).transpose(0, 2, 1, 3, 4).reshape(bsz * dil, sub, C_HEADS, C_DH)

        q = by_residue(proj[:, :, gidx, 0]) * C_DH ** -0.5
        k = by_residue(proj[:, :, gidx, 1])
        v = by_residue(proj[:, :, gidx, 2])
        o, lse = banded_attention(q[:, :, :, None], k, v, window // dil)
        outs.append(o.reshape(bsz, dil, sub, C_HEADS, C_DH).transpose(0, 2, 1, 3, 4).reshape(bsz, t_len, C_HEADS, C_DH))
        lses.append(lse.reshape(bsz, dil, sub, C_HEADS).transpose(0, 2, 1, 3).reshape(bsz, t_len, C_HEADS))
    alpha = jax.nn.softmax(jnp.stack(lses), axis=0)
    o = jnp.sum(alpha[..., None].astype(h.dtype) * jnp.stack(outs), axis=0)
    return o.reshape(bsz, t_len, C_GROUP_W) @ w_out


def conv_ffn(h, w_up, conv_w, conv_b, w_down):
    u = causal_dwconv(h @ w_up, conv_w, conv_b)
    gate, val = u[..., :F_HIDDEN], u[..., F_HIDDEN:]
    return (jax.nn.silu(gate) * val) @ w_down


def setup_inputs(seed: int = 0):
    key = jax.random.key(seed)
    keys = iter(jax.random.split(key, 48))

    def nrm(shape, scale):
        return jax.random.normal(next(keys), shape, jnp.float32) * scale

    def gain(shape):
        return 1.0 + nrm(shape, 0.02)

    res = (2.0 * DEPTH) ** -0.5
    d = D_MODEL
    inp = {}
    inp['x'] = nrm((BATCH, SEQ, d), 1.0)
    inp['norm_mix'] = gain((DEPTH, d))
    inp['norm_ffn'] = gain((DEPTH, d))
    inp['norm_final'] = gain((d,))
    inp['a_w_in'] = nrm((N_A, d, A_IN_COLS), d ** -0.5)
    f_bias = jnp.linspace(3.0, 6.0, A_HEADS, dtype=jnp.float32)
    inp['a_b_gate'] = jnp.concatenate([nrm((N_A, A_HEADS), 0.1), f_bias[None, :] + nrm((N_A, A_HEADS), 0.01)], axis=-1)
    inp['a_conv_w'] = nrm((N_A, A_CONV, A_INNER), A_CONV ** -0.5)
    inp['a_conv_b'] = nrm((N_A, A_INNER), 0.01)
    inp['a_wq'] = nrm((N_A, A_HEADS, A_DH, A_DH), A_DH ** -0.5)
    inp['a_wk'] = nrm((N_A, A_HEADS, A_DH, A_DH), A_DH ** -0.5)
    inp['a_wv'] = nrm((N_A, A_HEADS, A_DH, A_DH), A_DH ** -0.5)
    inp['a_gn'] = gain((N_A, A_INNER))
    inp['a_skip'] = gain((N_A, A_INNER))
    inp['a_w_out'] = nrm((N_A, A_INNER, d), A_INNER ** -0.5 * res)
    inp['b_w_in'] = nrm((N_B, d, B_IN_COLS), d ** -0.5)
    inp['b_b_gate'] = nrm((N_B, 3 * B_HEADS), 0.01)
    inp['b_pe_k'] = nrm((N_B, B_CMP_LEN, B_DH), 0.1)
    inp['b_pe_v'] = nrm((N_B, B_CMP_LEN, B_DH), 0.1)
    inp['b_phik_w1'] = nrm((N_B, B_CMP_LEN * B_DH, B_PHI_HIDDEN), (B_CMP_LEN * B_DH) ** -0.5)
    inp['b_phik_b1'] = nrm((N_B, B_PHI_HIDDEN), 0.01)
    inp['b_phik_w2'] = nrm((N_B, B_PHI_HIDDEN, B_DH), B_PHI_HIDDEN ** -0.5)
    inp['b_phiv_w1'] = nrm((N_B, B_CMP_LEN * B_DH, B_PHI_HIDDEN), (B_CMP_LEN * B_DH) ** -0.5)
    inp['b_phiv_b1'] = nrm((N_B, B_PHI_HIDDEN), 0.01)
    inp['b_phiv_w2'] = nrm((N_B, B_PHI_HIDDEN, B_DH), B_PHI_HIDDEN ** -0.5)
    inp['b_w_out'] = nrm((N_B, B_HEADS * B_DH, d), (B_HEADS * B_DH) ** -0.5 * res)
    inp['c_w_in'] = nrm((N_C, d, C_IN_COLS), d ** -0.5)
    inp['c_w_out'] = nrm((N_C, C_GROUP_W, d), C_GROUP_W ** -0.5 * res)
    inp['f_w_up'] = nrm((DEPTH, d, 2 * F_HIDDEN), d ** -0.5)
    inp['f_conv_w'] = nrm((DEPTH, F_CONV, 2 * F_HIDDEN), F_CONV ** -0.5)
    inp['f_conv_b'] = nrm((DEPTH, 2 * F_HIDDEN), 0.01)
    inp['f_w_down'] = nrm((DEPTH, F_HIDDEN, d), F_HIDDEN ** -0.5 * res)
    return inp


def reference(x, norm_mix, norm_ffn, norm_final,
              a_w_in, a_b_gate, a_conv_w, a_conv_b, a_wq, a_wk, a_wv, a_gn, a_skip, a_w_out,
              b_w_in, b_b_gate, b_pe_k, b_pe_v, b_phik_w1, b_phik_b1, b_phik_w2,
              b_phiv_w1, b_phiv_b1, b_phiv_w2, b_w_out,
              c_w_in, c_w_out,
              f_w_up, f_conv_w, f_conv_b, f_w_down):
    h = x
    for i in range(DEPTH):
        kind = i % N_MIXERS
        j = i // N_MIXERS
        hn = rmsnorm(h, norm_mix[i])
        if kind == 0:
            y = mlstm_mixer(hn, a_w_in[j], a_b_gate[j], a_conv_w[j], a_conv_b[j], a_wq[j], a_wk[j], a_wv[j],
                            a_gn[j], a_skip[j], a_w_out[j])
        elif kind == 1:
            y = nsa_mixer(hn, b_w_in[j], b_b_gate[j], b_pe_k[j], b_pe_v[j], b_phik_w1[j], b_phik_b1[j], b_phik_w2[j],
                          b_phiv_w1[j], b_phiv_b1[j], b_phiv_w2[j], b_w_out[j])
        else:
            y = dilated_mixer(hn, c_w_in[j], c_w_out[j])
        h = h + y
        h = h + conv_ffn(rmsnorm(h, norm_ffn[i]), f_w_up[i], f_conv_w[i], f_conv_b[i], f_w_down[i])
    return rmsnorm(h, norm_final)
```

```python
import functools

import jax
import jax.numpy as jnp
from jax import lax
from jax.experimental import pallas as pl
from jax.experimental.pallas import tpu as pltpu

F32 = jnp.float32
BF16 = jnp.bfloat16

NORM_EPS = 1e-6
NEG = -1e30
LANE = 128
SUBLANE = 8
VMEM_LIMIT = 56 * 1024 * 1024

A_HEADS = 4
A_CHUNK = 256
A_CONV = 4
B_HEADS = 16
B_DH = 64
B_GROUPS = 4
B_QPG = 4
B_CMP_LEN = 32
B_CMP_STRIDE = 16
B_SEL_LEN = 64
B_TOPN = 16
B_N_LOCAL = 2
B_WINDOW = 512
FORCE_SCORE = 1e9
B_TQ = 128
B_TK = 256
C_PATTERNS = ((128, 1), (512, 4), (2048, 16))
C_HEADS = 16
C_DH = 64
C_QBLK = 128
F_CHUNK = 256


def _cparams(sem):
    return pltpu.CompilerParams(dimension_semantics=sem, vmem_limit_bytes=VMEM_LIMIT)


def _const_spec(shape):
    nd = len(shape)
    return pl.BlockSpec(shape, lambda *_: (0,) * nd, pipeline_mode=pl.Buffered(1))


def _rmsnorm_rows(x, g):
    ms = jnp.mean(x * x, axis=-1, keepdims=True)
    return x * lax.rsqrt(ms + NORM_EPS) * g


def _sigmoid(x):
    return 1.0 / (1.0 + jnp.exp(-x))


def _nmm_body(x_ref, g_ref, w_ref, o_ref, xn_ref, *rest, slab, dil):
    @pl.when(pl.program_id(2) == 0)
    def _():
        xn_ref[...] = _rmsnorm_rows(x_ref[0], g_ref[...]).astype(BF16)

    r = jnp.dot(xn_ref[...], w_ref[...], preferred_element_type=F32)
    if not slab:
        o_ref[0] = r.astype(o_ref.dtype)
    elif dil == 1:
        for c in range(r.shape[1] // LANE):
            o_ref[0, c, 0] = r[:, c * LANE:(c + 1) * LANE].astype(o_ref.dtype)
    else:
        r_ref, = rest
        sub = r.shape[0] // dil
        for c in range(r.shape[1] // LANE):
            r_ref[c] = r[:, c * LANE:(c + 1) * LANE]
            for rc in range(dil):
                o_ref[0, c, rc] = r_ref[c, pl.ds(rc, sub, stride=dil), :].astype(o_ref.dtype)


def _norm_matmul(h, g, w, *, out_dtype, tm, tn, slab=False, dil=1, name):
    bsz, t_len, d = h.shape
    n = w.shape[1]
    assert t_len % tm == 0 and n % tn == 0 and tn % LANE == 0 and tm % (2 * SUBLANE * dil) == 0
    scratch = [pltpu.VMEM((tm, d), BF16)]
    if slab:
        out_shape = jax.ShapeDtypeStruct((bsz, n // LANE, dil, t_len // dil, LANE), out_dtype)
        out_spec = pl.BlockSpec((1, tn // LANE, dil, tm // dil, LANE),
                                lambda b, i, j: (b, j, 0, i, 0))
        if dil > 1:
            scratch.append(pltpu.VMEM((tn // LANE, tm, LANE), F32))
    else:
        out_shape = jax.ShapeDtypeStruct((bsz, t_len, n), out_dtype)
        out_spec = pl.BlockSpec((1, tm, tn), lambda b, i, j: (b, i, j))
    return pl.pallas_call(
        functools.partial(_nmm_body, slab=slab, dil=dil),
        out_shape=out_shape,
        grid=(bsz, t_len // tm, n // tn),
        in_specs=[
            pl.BlockSpec((1, tm, d), lambda b, i, j: (b, i, 0)),
            pl.BlockSpec((1, d), lambda b, i, j: (0, 0)),
            pl.BlockSpec((d, tn), lambda b, i, j: (0, j)),
        ],
        out_specs=out_spec,
        scratch_shapes=scratch,
        compiler_params=_cparams(("parallel", "parallel", "arbitrary")),
        name=name,
    )(h, g.reshape(1, d), w)


def _mmr_body(x_ref, w_ref, r_ref, o_ref, *, slab):
    if slab:
        x = jnp.concatenate([x_ref[0, c] for c in range(x_ref.shape[1])], axis=-1)
    else:
        x = x_ref[0]
    o_ref[0] = r_ref[0] + jnp.dot(x, w_ref[...], preferred_element_type=F32)


def _matmul_residual(x, w, res, *, tm, slab=False, name):
    bsz, t_len, n = res.shape
    k = w.shape[0]
    if slab:
        x_spec = pl.BlockSpec((1, k // LANE, tm, LANE), lambda b, i: (b, 0, i, 0))
    else:
        x_spec = pl.BlockSpec((1, tm, k), lambda b, i: (b, i, 0))
    return pl.pallas_call(
        functools.partial(_mmr_body, slab=slab),
        out_shape=jax.ShapeDtypeStruct((bsz, t_len, n), F32),
        grid=(bsz, t_len // tm),
        in_specs=[x_spec, _const_spec((k, n)),
                  pl.BlockSpec((1, tm, n), lambda b, i: (b, i, 0))],
        out_specs=pl.BlockSpec((1, tm, n), lambda b, i: (b, i, 0)),
        compiler_params=_cparams(("parallel", "parallel")),
        name=name,
    )(x, w, res)


def _ffn_body(h_ref, hp_ref, g_ref, wup_ref, cw_ref, cb_ref, wdn_ref, gf_ref, o_ref,
              xn_ref, acc_ref, *, tm, f_hidden, final_norm):
    halo = SUBLANE
    g = g_ref[...]
    keep = (pl.program_id(1) > 0).astype(F32)
    xn_ref[0:halo, :] = (_rmsnorm_rows(hp_ref[0], g) * keep).astype(BF16)
    xn_ref[halo:, :] = _rmsnorm_rows(h_ref[0], g).astype(BF16)
    xn = xn_ref[...]

    def conv(u, col0):
        w = cw_ref[:, col0:col0 + F_CHUNK]
        b = cb_ref[:, col0:col0 + F_CHUNK]
        return (w[2:3] * u[halo:halo + tm] + w[1:2] * u[halo - 1:halo - 1 + tm]
                + w[0:1] * u[halo - 2:halo - 2 + tm] + b)

    for c in range(f_hidden // F_CHUNK):
        c0 = c * F_CHUNK
        ug = jnp.dot(xn, wup_ref[:, c0:c0 + F_CHUNK], preferred_element_type=F32)
        uv = jnp.dot(xn, wup_ref[:, f_hidden + c0:f_hidden + c0 + F_CHUNK],
                     preferred_element_type=F32)
        gate = conv(ug, c0)
        val = conv(uv, f_hidden + c0)
        act = (gate * _sigmoid(gate) * val).astype(BF16)
        part = jnp.dot(act, wdn_ref[c0:c0 + F_CHUNK, :], preferred_element_type=F32)
        if c == 0:
            acc_ref[...] = part
        else:
            acc_ref[...] += part

    out = h_ref[0] + acc_ref[...]
    if final_norm:
        out = _rmsnorm_rows(out, gf_ref[...])
    o_ref[0] = out


def _conv_ffn(h, g, w_up, conv_w, conv_b, w_down, g_final, *, tm, final_norm, name):
    bsz, t_len, d = h.shape
    f2 = w_up.shape[1]
    f_hidden = f2 // 2
    assert f_hidden % F_CHUNK == 0 and t_len % tm == 0 and tm % SUBLANE == 0
    rows_per_tile = tm // SUBLANE
    return pl.pallas_call(
        functools.partial(_ffn_body, tm=tm, f_hidden=f_hidden, final_norm=final_norm),
        out_shape=jax.ShapeDtypeStruct((bsz, t_len, d), F32),
        grid=(bsz, t_len // tm),
        in_specs=[
            pl.BlockSpec((1, tm, d), lambda b, i: (b, i, 0)),
            pl.BlockSpec((1, SUBLANE, d),
                         lambda b, i: (b, jnp.maximum(i * rows_per_tile - 1, 0), 0)),
            _const_spec((1, d)),
            _const_spec((d, f2)),
            _const_spec((3, f2)),
            _const_spec((1, f2)),
            _const_spec((f_hidden, d)),
            _const_spec((1, d)),
        ],
        out_specs=pl.BlockSpec((1, tm, d), lambda b, i: (b, i, 0)),
        scratch_shapes=[pltpu.VMEM((tm + SUBLANE, d), BF16), pltpu.VMEM((tm, d), F32)],
        compiler_params=_cparams(("parallel", "parallel")),
        name=name,
    )(h, h, g.reshape(1, d), w_up, conv_w, conv_b.reshape(1, f2), w_down, g_final.reshape(1, d))


def _mlstm_body(xm_ref, op_ref, gt_ref, bg_ref, cw_ref, cb_ref, wq_ref, wk_ref, wv_ref,
                gn_ref, sk_ref, y_ref, c_ref, n_ref, m_ref, xp_ref, *, chunk, dh):
    L = chunk
    halo = SUBLANE

    @pl.when(pl.program_id(1) == 0)
    def _():
        c_ref[...] = jnp.zeros_like(c_ref)
        n_ref[...] = jnp.zeros_like(n_ref)
        m_ref[...] = jnp.zeros_like(m_ref)
        xp_ref[...] = jnp.zeros_like(xp_ref)

    xm = xm_ref[0]
    xe = jnp.concatenate([xp_ref[...], xm], axis=0)
    cw = cw_ref[...]
    xc = (cw[3:4] * xe[halo:halo + L] + cw[2:3] * xe[halo - 1:halo - 1 + L]
          + cw[1:2] * xe[halo - 2:halo - 2 + L] + cw[0:1] * xe[halo - 3:halo - 3 + L]
          + cb_ref[...])
    xc = xc * _sigmoid(xc)
    xp_ref[...] = xm[L - halo:L]

    gates = gt_ref[0] + bg_ref[...]
    ii = lax.broadcasted_iota(jnp.int32, (L, L), 0)
    jj = lax.broadcasted_iota(jnp.int32, (L, L), 1)
    eye = ii == jj
    tril = jj <= ii

    def to_row(col):
        return jnp.sum(jnp.where(eye, col, 0.0), axis=0, keepdims=True)

    for hd in range(A_HEADS):
        lo, hi = hd * dh, (hd + 1) * dh
        xc_h = xc[:, lo:hi]
        xcb = xc_h.astype(BF16)
        xmb = xm[:, lo:hi].astype(BF16)
        q = jnp.dot(xcb, wq_ref[hd], preferred_element_type=F32)
        k = jnp.dot(xcb, wk_ref[hd], preferred_element_type=F32) * (dh ** -0.5)
        v = jnp.dot(xmb, wv_ref[hd], preferred_element_type=F32)
        qb = q.astype(BF16)
        kb = k.astype(BF16)
        vb = v.astype(BF16)

        li = gates[:, hd:hd + 1]
        fpre = gates[:, A_HEADS + hd:A_HEADS + hd + 1]
        lf = jnp.minimum(fpre, 0.0) - jnp.log1p(jnp.exp(-jnp.abs(fpre)))
        li_row = to_row(li)
        lf_row = to_row(lf)
        b_col = jnp.sum(jnp.where(tril, lf_row, 0.0), axis=1, keepdims=True)
        b_row = to_row(b_col)
        m_st = m_ref[hd][:, 0:1]

        log_d = jnp.where(tril, b_col - b_row + li_row, NEG)
        log_inter = b_col + m_st
        m = jnp.maximum(log_inter, jnp.max(log_d, axis=1, keepdims=True))
        w_intra = jnp.exp(log_d - m)
        w_inter = jnp.exp(log_inter - m)
        s = lax.dot_general(qb, kb, (((1,), (1,)), ((), ())),
                            preferred_element_type=F32) * w_intra
        num = (w_inter * jnp.dot(qb, c_ref[hd].astype(BF16), preferred_element_type=F32)
               + jnp.dot(s.astype(BF16), vb, preferred_element_type=F32))
        den = (w_inter * jnp.sum(q * n_ref[hd], axis=1, keepdims=True)
               + jnp.sum(s, axis=1, keepdims=True))
        hh = num / jnp.maximum(jnp.abs(den), jnp.exp(-m))

        b_last = b_col[L - 1:L, :]
        log_w = b_last - b_col + li
        m_new = jnp.maximum(b_last + m_st, jnp.max(log_w, axis=0, keepdims=True))
        decay = jnp.exp(b_last + m_st - m_new)
        wk = jnp.exp(log_w - m_new) * k
        c_ref[hd] = decay * c_ref[hd] + lax.dot_general(
            wk.astype(BF16), vb, (((0,), (0,)), ((), ())), preferred_element_type=F32)
        n_ref[hd] = decay * n_ref[hd] + jnp.sum(wk, axis=0, keepdims=True)
        m_ref[hd] = jnp.broadcast_to(m_new, m_ref.shape[1:])

        mu = jnp.mean(hh, axis=-1, keepdims=True)
        ctr = hh - mu
        var = jnp.mean(ctr * ctr, axis=-1, keepdims=True)
        hn = ctr * lax.rsqrt(var + NORM_EPS) * gn_ref[:, lo:hi]
        y = _sigmoid(op_ref[0, :, lo:hi]) * hn + sk_ref[:, lo:hi] * xc_h
        y_ref[0, :, lo:hi] = y.astype(y_ref.dtype)


def _mlstm_core(proj, b_gate, conv_w, conv_b, wq, wk, wv, gn, skip, *, inner):
    bsz, t_len, _ = proj.shape
    dh = inner // A_HEADS
    L = A_CHUNK
    gate_blk = 2 * inner // LANE
    bg = jnp.zeros((1, LANE), F32).at[0, :2 * A_HEADS].set(b_gate)
    return pl.pallas_call(
        functools.partial(_mlstm_body, chunk=L, dh=dh),
        out_shape=jax.ShapeDtypeStruct((bsz, t_len, inner), BF16),
        grid=(bsz, t_len // L),
        in_specs=[
            pl.BlockSpec((1, L, inner), lambda b, c: (b, c, 0)),
            pl.BlockSpec((1, L, inner), lambda b, c: (b, c, 1)),
            pl.BlockSpec((1, L, LANE), lambda b, c: (b, c, gate_blk)),
            _const_spec((1, LANE)),
            _const_spec((A_CONV, inner)),
            _const_spec((1, inner)),
            _const_spec((A_HEADS, dh, dh)),
            _const_spec((A_HEADS, dh, dh)),
            _const_spec((A_HEADS, dh, dh)),
            _const_spec((1, inner)),
            _const_spec((1, inner)),
        ],
        out_specs=pl.BlockSpec((1, L, inner), lambda b, c: (b, c, 0)),
        scratch_shapes=[
            pltpu.VMEM((A_HEADS, dh, dh), F32),
            pltpu.VMEM((A_HEADS, 1, dh), F32),
            pltpu.VMEM((A_HEADS, 1, LANE), F32),
            pltpu.VMEM((SUBLANE, inner), F32),
        ],
        compiler_params=_cparams(("parallel", "arbitrary")),
        name="mlstm_core",
    )(proj, proj, proj, bg, conv_w, conv_b.reshape(1, inner), wq, wk, wv,
      gn.reshape(1, inner), skip.reshape(1, inner))


def _mlstm_layer(h, g, w_in, b_gate, conv_w, conv_b, wq, wk, wv, gn, skip, w_out):
    d = h.shape[-1]
    inner = w_out.shape[0]
    w_a = jnp.concatenate(
        [w_in[:, :2 * inner], w_in[:, 2 * inner:], jnp.zeros((d, LANE - 2 * A_HEADS), F32)],
        axis=1).astype(BF16)
    proj = _norm_matmul(h, g, w_a, out_dtype=F32, tm=512, tn=w_a.shape[1] // 3, name="mlstm_in")
    y = _mlstm_core(proj, b_gate, conv_w, conv_b, wq.astype(BF16), wk.astype(BF16),
                    wv.astype(BF16), gn, skip, inner=inner)
    return _matmul_residual(y, w_out.astype(BF16), h, tm=512, name="mlstm_out")


def _cmp_body(x_ref, pe_ref, w1_ref, b1_ref, w2_ref, o_ref):
    half = x_ref.shape[-1]
    x = x_ref[0, 0, 0]
    pe = pe_ref[0]
    a = (x + pe[:, :half]).astype(BF16)
    b = (x + pe[:, half:]).astype(BF16)
    w1 = w1_ref[0]
    pa = jnp.dot(a, w1[:half], preferred_element_type=F32)
    pb = jnp.dot(b, w1[half:], preferred_element_type=F32)
    pb_next = jnp.concatenate([pb[1:], jnp.zeros((1, pb.shape[1]), F32)], axis=0)
    pre = pa + pb_next + b1_ref[0]
    hid = (pre * _sigmoid(pre)).astype(BF16)
    out = jnp.dot(hid, w2_ref[0], preferred_element_type=F32)
    row = lax.broadcasted_iota(jnp.int32, out.shape, 0)
    o_ref[0, 0, 0] = jnp.where(row < out.shape[0] - 1, out, 0.0).astype(o_ref.dtype)


def _nsa_compress(kv_cmp, pe, w1, b1, w2):
    two, bsz, g, t_len, dh = kv_cmp.shape
    n_half = t_len // B_CMP_STRIDE
    half = B_CMP_STRIDE * dh
    x = kv_cmp.reshape(two, bsz, g, n_half, half)
    hidden = w1.shape[-1]
    return pl.pallas_call(
        _cmp_body,
        out_shape=jax.ShapeDtypeStruct((two, bsz, g, n_half, dh), BF16),
        grid=(two, bsz, g),
        in_specs=[
            pl.BlockSpec((1, 1, 1, n_half, half), lambda s, b, gi: (s, b, gi, 0, 0)),
            pl.BlockSpec((1, 1, 2 * half), lambda s, b, gi: (s, 0, 0)),
            pl.BlockSpec((1, 2 * half, hidden), lambda s, b, gi: (s, 0, 0)),
            pl.BlockSpec((1, 1, hidden), lambda s, b, gi: (s, 0, 0)),
            pl.BlockSpec((1, hidden, dh), lambda s, b, gi: (s, 0, 0)),
        ],
        out_specs=pl.BlockSpec((1, 1, 1, n_half, dh), lambda s, b, gi: (s, b, gi, 0, 0)),
        compiler_params=_cparams(("parallel", "parallel", "parallel")),
        name="nsa_compress",
    )(x, pe.reshape(two, 1, 2 * half), w1, b1.reshape(two, 1, hidden), w2)


def _flash(qs, k_ref, v_ref, kt_lo, kt_hi, bias_fn, tk):
    rows, dh = qs.shape

    def body(kt, carry):
        m, l, acc = carry
        ks = pl.multiple_of(kt * tk, tk)
        k = k_ref[0, 0, pl.ds(ks, tk), :]
        v = v_ref[0, 0, pl.ds(ks, tk), :]
        s = lax.dot_general(qs, k, (((1,), (1,)), ((), ())), preferred_element_type=F32)
        s = s + bias_fn(ks)
        m_new = jnp.maximum(m, jnp.max(s, axis=-1, keepdims=True))
        alpha = jnp.exp(m - m_new)
        p = jnp.exp(s - m_new)
        l = alpha * l + jnp.sum(p, axis=-1, keepdims=True)
        acc = alpha * acc + jnp.dot(p.astype(BF16), v, preferred_element_type=F32)
        return m_new, l, acc

    init = (jnp.full((rows, 1), NEG, F32), jnp.zeros((rows, 1), F32), jnp.zeros((rows, dh), F32))
    _, l, acc = lax.fori_loop(kt_lo, kt_hi, body, init)
    return acc / l


def _nsa_body(q_ref, kc_ref, vc_ref, ks_ref, vs_ref, kw_ref, vw_ref, gp_ref, bg_ref,
              ovl_ref, exp_ref, o_ref, bias_ref, *, tq, tk, t_len):
    qi = pl.program_id(2)
    t0 = qi * tq
    qt = q_ref[0]
    qs = jnp.concatenate([qt[:, r * B_DH:(r + 1) * B_DH] for r in range(B_QPG)], axis=0)
    rows = B_QPG * tq

    ncp = kc_ref.shape[2]
    s = lax.dot_general(qs, kc_ref[0, 0], (((1,), (1,)), ((), ())), preferred_element_type=F32)
    n_idx = lax.broadcasted_iota(jnp.int32, (rows, ncp), 1)
    t_row = t0 + lax.broadcasted_iota(jnp.int32, (rows, ncp), 0) % tq
    cmask = (n_idx * B_CMP_STRIDE + (B_CMP_LEN - 1) <= t_row) & (n_idx < ncp - 1)
    s = jnp.where(cmask, s, NEG)
    m = jnp.max(s, axis=-1, keepdims=True)
    e = jnp.where(cmask, jnp.exp(s - m), 0.0)
    den = jnp.sum(e, axis=-1, keepdims=True)
    p = e / jnp.where(den > 0, den, 1.0)
    o_cmp = jnp.dot(p.astype(BF16), vc_ref[0, 0], preferred_element_type=F32)

    psum = p[0:tq]
    for r in range(1, B_QPG):
        psum = psum + p[r * tq:(r + 1) * tq]
    nsel = t_len // B_SEL_LEN
    imp = lax.dot_general(ovl_ref[...], psum, (((1,), (1,)), ((), ())),
                          precision=lax.Precision.HIGHEST, preferred_element_type=F32)
    imp = imp[0:nsel]
    blk = lax.broadcasted_iota(jnp.int32, (nsel, tq), 0)
    cur = (t0 + lax.broadcasted_iota(jnp.int32, (nsel, tq), 1)) // B_SEL_LEN
    valid = blk <= cur
    forced = ((blk == 0) | (cur - blk < B_N_LOCAL)) & valid
    score = jnp.where(forced, FORCE_SCORE, jnp.where(valid, imp, -FORCE_SCORE))
    n_top = min(B_TOPN, nsel)
    sel = jnp.zeros((nsel, tq), F32)
    for j in range(nsel):
        sj = score[j:j + 1, :]
        ahead = (score > sj) | ((score == sj) & (blk < j))
        cnt = jnp.sum(ahead.astype(F32), axis=0, keepdims=True)
        sel = jnp.where((blk == j) & (cnt < n_top), 1.0, sel)
    sel = jnp.where(valid, sel, 0.0)
    sel_pad = jnp.concatenate([sel, jnp.zeros((LANE - nsel, tq), F32)], axis=0)
    sel_q = sel_pad.T.astype(BF16)
    allow = jnp.dot(sel_q, exp_ref[...], preferred_element_type=F32)
    kpos = lax.broadcasted_iota(jnp.int32, (tq, t_len), 1)
    tpos = t0 + lax.broadcasted_iota(jnp.int32, (tq, t_len), 0)
    bias_ref[...] = jnp.where((allow > 0.5) & (kpos <= tpos), 0.0, NEG)

    def sel_bias(ks):
        b = bias_ref[:, pl.ds(ks, tk)]
        return jnp.concatenate([b] * B_QPG, axis=0)

    o_sel = _flash(qs, ks_ref, vs_ref, 0, (t0 + tq + tk - 1) // tk, sel_bias, tk)

    def win_bias(ks):
        kp = ks + lax.broadcasted_iota(jnp.int32, (rows, tk), 1)
        tp = t0 + lax.broadcasted_iota(jnp.int32, (rows, tk), 0) % tq
        diff = tp - kp
        return jnp.where((diff >= 0) & (diff <= B_WINDOW - 1), 0.0, NEG)

    w_lo = jnp.maximum(t0 - (B_WINDOW - 1), 0) // tk
    o_win = _flash(qs, kw_ref, vw_ref, w_lo, (t0 + tq + tk - 1) // tk, win_bias, tk)

    gate = _sigmoid(gp_ref[0, 0] + bg_ref[0])
    outs = []
    for r in range(B_QPG):
        sl = slice(r * tq, (r + 1) * tq)
        outs.append(gate[:, r:r + 1] * o_cmp[sl]
                    + gate[:, B_QPG + r:B_QPG + r + 1] * o_sel[sl]
                    + gate[:, 2 * B_QPG + r:2 * B_QPG + r + 1] * o_win[sl])
    o_ref[0] = jnp.concatenate(outs, axis=-1).astype(o_ref.dtype)


def _nsa_attention(q, kc, vc, k_sel, v_sel, k_win, v_win, gpre, bg):
    bsz, t_len, _ = q.shape
    tq, tk = B_TQ, B_TK
    ncp = kc.shape[2]
    nsel = t_len // B_SEL_LEN
    gw = B_QPG * B_DH
    c_start = jnp.arange(ncp)[None, :] * B_CMP_STRIDE
    s_start = jnp.arange(LANE)[:, None] * B_SEL_LEN
    ovl = ((c_start < s_start + B_SEL_LEN) & (c_start + B_CMP_LEN > s_start)
           & (jnp.arange(LANE)[:, None] < nsel) & (jnp.arange(ncp)[None, :] < ncp - 1)).astype(F32)
    expand = (jnp.arange(LANE)[:, None] == jnp.arange(t_len)[None, :] // B_SEL_LEN).astype(BF16)

    def kv_spec(n_rows):
        return pl.BlockSpec((1, 1, n_rows, B_DH), lambda b, g, i: (b, g, 0, 0))

    return pl.pallas_call(
        functools.partial(_nsa_body, tq=tq, tk=tk, t_len=t_len),
        out_shape=jax.ShapeDtypeStruct((bsz, t_len, B_HEADS * B_DH), BF16),
        grid=(bsz, B_GROUPS, t_len // tq),
        in_specs=[
            pl.BlockSpec((1, tq, gw), lambda b, g, i: (b, i, g)),
            kv_spec(ncp), kv_spec(ncp), kv_spec(t_len), kv_spec(t_len),
            kv_spec(t_len), kv_spec(t_len),
            pl.BlockSpec((1, 1, tq, 16), lambda b, g, i: (b, g, i, 0)),
            pl.BlockSpec((1, 1, 16), lambda b, g, i: (g, 0, 0)),
            _const_spec((LANE, ncp)),
            _const_spec((LANE, t_len)),
        ],
        out_specs=pl.BlockSpec((1, tq, gw), lambda b, g, i: (b, i, g)),
        scratch_shapes=[pltpu.VMEM((tq, t_len), F32)],
        compiler_params=_cparams(("parallel", "parallel", "parallel")),
        name="nsa_attention",
    )(q, kc, vc, k_sel, v_sel, k_win, v_win, gpre, bg, ovl, expand)


def _nsa_layer(h, g, w_in, b_gate, pe_k, pe_v, phik_w1, phik_b1, phik_w2,
               phiv_w1, phiv_b1, phiv_w2, w_out):
    bsz, t_len, d = h.shape
    nq = B_HEADS * B_DH
    nkv = 6 * B_GROUPS * B_DH
    wg = w_in[:, nq + nkv:].reshape(d, B_GROUPS, B_QPG, 3).transpose(0, 1, 3, 2)
    wg = jnp.pad(wg.reshape(d, B_GROUPS, 3 * B_QPG), ((0, 0), (0, 0), (0, 16 - 3 * B_QPG)))
    wg = jnp.pad(wg.reshape(d, B_GROUPS * 16), ((0, 0), (0, LANE - B_GROUPS * 16)))
    bg = b_gate.reshape(B_GROUPS, B_QPG, 3).transpose(0, 2, 1).reshape(B_GROUPS, 3 * B_QPG)
    bg = jnp.pad(bg, ((0, 0), (0, 16 - 3 * B_QPG))).reshape(B_GROUPS, 1, 16)
    w_b = jnp.concatenate([w_in[:, :nq + nkv], wg], axis=1).astype(BF16)
    proj = _norm_matmul(h, g, w_b, out_dtype=F32, tm=512, tn=w_b.shape[1] // 3, name="nsa_in")

    q = (proj[..., :nq] * (B_DH ** -0.5)).astype(BF16)
    kv = proj[..., nq:nq + nkv].reshape(bsz, t_len, 6, B_GROUPS, B_DH).transpose(2, 0, 3, 1, 4)
    kv_cmp = kv[0:2]
    kv_rest = kv[2:6].astype(BF16)
    gpre = proj[..., nq + nkv:nq + nkv + B_GROUPS * 16].reshape(bsz, t_len, B_GROUPS, 16)
    gpre = gpre.transpose(0, 2, 1, 3)

    cmp = _nsa_compress(
        kv_cmp, jnp.stack([pe_k, pe_v]), jnp.stack([phik_w1, phiv_w1]).astype(BF16),
        jnp.stack([phik_b1, phiv_b1]), jnp.stack([phik_w2, phiv_w2]).astype(BF16))
    o = _nsa_attention(q, cmp[0], cmp[1], kv_rest[0], kv_rest[1], kv_rest[2], kv_rest[3], gpre, bg)
    return _matmul_residual(o, w_out.astype(BF16), h, tm=512, name="nsa_out")


def _dil_body(*refs, t_len):
    qkv = refs[:9]
    o_ref = refs[9]
    og_ref, lg_ref = refs[10], refs[11]
    qb = C_QBLK
    npair = LANE // C_DH

    for gi, (window, dil) in enumerate(C_PATTERNS):
        sub = t_len // dil
        nb = sub // qb
        nk = qb if nb == 1 else 2 * qb

        def body(i, carry, gi=gi, dil=dil, sub=sub, nb=nb, nk=nk, window=window):
            rc = i // nb
            bq = i % nb
            bk = jnp.maximum(bq - 1, 0)
            q0 = pl.multiple_of(rc * sub + bq * qb, qb)
            k0 = pl.multiple_of(rc * sub + bk * qb, qb)
            q2 = qkv[3 * gi][0, 0, pl.ds(q0, qb), :]
            k2 = qkv[3 * gi + 1][0, 0, pl.ds(k0, nk), :]
            v2 = qkv[3 * gi + 2][0, 0, pl.ds(k0, nk), :]
            diff = ((bq - bk) * qb + lax.broadcasted_iota(jnp.int32, (qb, nk), 0)
                    - lax.broadcasted_iota(jnp.int32, (qb, nk), 1))
            band = (diff >= 0) & (diff <= window // dil)
            outs, lses = [], []
            for hp in range(npair):
                lanes = slice(hp * C_DH, (hp + 1) * C_DH)
                s = lax.dot_general(q2[:, lanes], k2[:, lanes], (((1,), (1,)), ((), ())),
                                    preferred_element_type=F32)
                s = jnp.where(band, s, NEG)
                m = jnp.max(s, axis=-1, keepdims=True)
                e = jnp.exp(s - m)
                den = jnp.sum(e, axis=-1, keepdims=True)
                outs.append(jnp.dot((e / den).astype(BF16), v2[:, lanes],
                                    preferred_element_type=F32))
                lses.append(jnp.broadcast_to(m + jnp.log(den), (qb, C_DH)))
            if dil == 1:
                rows = pl.ds(q0, qb)
            else:
                rows = pl.ds(rc + bq * (qb * dil), qb, stride=dil)
            og_ref[gi, rows, :] = jnp.concatenate(outs, axis=-1)
            lg_ref[gi, rows, :] = jnp.concatenate(lses, axis=-1)
            return carry

        lax.fori_loop(0, dil * nb, body, 0)

    lse = lg_ref[...]
    mx = jnp.max(lse, axis=0, keepdims=True)
    w = jnp.exp(lse - mx)
    alpha = w / jnp.sum(w, axis=0, keepdims=True)
    o_ref[0, 0] = jnp.sum(alpha * og_ref[...], axis=0).astype(o_ref.dtype)


def _dilated_attention(projs):
    bsz, _, t_len, _ = projs[0].shape
    npairs = C_HEADS * C_DH // LANE
    ng = len(C_PATTERNS)

    def spec(col):
        return pl.BlockSpec((1, 1, t_len, LANE), lambda b, hp, col=col: (b, col * npairs + hp, 0, 0))

    return pl.pallas_call(
        functools.partial(_dil_body, t_len=t_len),
        out_shape=jax.ShapeDtypeStruct((bsz, npairs, t_len, LANE), BF16),
        grid=(bsz, npairs),
        in_specs=[spec(c) for _ in range(ng) for c in range(3)],
        out_specs=pl.BlockSpec((1, 1, t_len, LANE), lambda b, hp: (b, hp, 0, 0)),
        scratch_shapes=[pltpu.VMEM((ng, t_len, LANE), F32), pltpu.VMEM((ng, t_len, LANE), F32)],
        compiler_params=_cparams(("parallel", "parallel")),
        name="dilated_attention",
    )(*[projs[gi] for gi in range(ng) for _ in range(3)])


def _dilated_layer(h, g, w_in, w_out):
    bsz, t_len, d = h.shape
    gw = C_HEADS * C_DH
    scale = jnp.ones((3, 1), F32).at[0, :].set(C_DH ** -0.5)
    w_c = w_in.reshape(d, len(C_PATTERNS), 3, gw)
    projs = []
    for gi, (_, dil) in enumerate(C_PATTERNS):
        w_g = (w_c[:, gi] * scale).reshape(d, 3 * gw).astype(BF16)
        p = _norm_matmul(h, g, w_g, out_dtype=BF16, tm=512, tn=3 * gw // 2, slab=True, dil=dil,
                         name=f"dilated_in{gi}")
        projs.append(p.reshape(bsz, 3 * gw // LANE, t_len, LANE))
    o = _dilated_attention(projs)
    return _matmul_residual(o, w_out.astype(BF16), h, tm=512, slab=True, name="dilated_out")


def kernel(x, norm_mix, norm_ffn, norm_final, a_w_in, a_b_gate, a_conv_w, a_conv_b, a_wq, a_wk, a_wv, a_gn, a_skip, a_w_out, b_w_in, b_b_gate, b_pe_k, b_pe_v, b_phik_w1, b_phik_b1, b_phik_w2, b_phiv_w1, b_phiv_b1, b_phiv_w2, b_w_out, c_w_in, c_w_out, f_w_up, f_conv_w, f_conv_b, f_w_down):
    depth = norm_mix.shape[0]
    h = x
    for i in range(depth):
        kind, j = i % 3, i // 3
        if kind == 0:
            h = _mlstm_layer(h, norm_mix[i], a_w_in[j], a_b_gate[j], a_conv_w[j], a_conv_b[j],
                             a_wq[j], a_wk[j], a_wv[j], a_gn[j], a_skip[j], a_w_out[j])
        elif kind == 1:
            h = _nsa_layer(h, norm_mix[i], b_w_in[j], b_b_gate[j], b_pe_k[j], b_pe_v[j],
                           b_phik_w1[j], b_phik_b1[j], b_phik_w2[j],
                           b_phiv_w1[j], b_phiv_b1[j], b_phiv_w2[j], b_w_out[j])
        else:
            h = _dilated_layer(h, norm_mix[i], c_w_in[j], c_w_out[j])
        h = _conv_ffn(h, norm_ffn[i], f_w_up[i].astype(BF16), f_conv_w[i], f_conv_b[i],
                      f_w_down[i].astype(BF16), norm_final, tm=512,
                      final_norm=(i == depth - 1), name="conv_ffn")
    return h
```

```python
import functools

import jax
import jax.numpy as jnp
from jax import lax
from jax.experimental import pallas as pl
from jax.experimental.pallas import tpu as pltpu

F32 = jnp.float32
BF16 = jnp.bfloat16

NORM_EPS = 1e-6
NEG = -1e30
LANE = 128
SUBLANE = 8
VMEM_LIMIT = 56 * 1024 * 1024

A_HEADS = 4
A_CHUNK = 256
A_CONV = 4
B_HEADS = 16
B_DH = 64
B_GROUPS = 4
B_QPG = 4
B_CMP_LEN = 32
B_CMP_STRIDE = 16
B_SEL_LEN = 64
B_TOPN = 16
B_N_LOCAL = 2
B_WINDOW = 512
FORCE_SCORE = 1e9
B_TQ = 128
B_TK = 256
C_PATTERNS = ((128, 1), (512, 4), (2048, 16))
C_HEADS = 16
C_DH = 64
C_QBLK = 128
F_CHUNK = 256


def _cparams(sem):
    return pltpu.CompilerParams(dimension_semantics=sem, vmem_limit_bytes=VMEM_LIMIT)


def _const_spec(shape):
    nd = len(shape)
    return pl.BlockSpec(shape, lambda *_: (0,) * nd, pipeline_mode=pl.Buffered(1))


def _rmsnorm_rows(x, g):
    ms = jnp.mean(x * x, axis=-1, keepdims=True)
    return x * lax.rsqrt(ms + NORM_EPS) * g


def _sigmoid(x):
    return 1.0 / (1.0 + jnp.exp(-x))


def _nmm_body(x_ref, g_ref, w_ref, o_ref, xn_ref, *rest, slab, dil):
    @pl.when(pl.program_id(2) == 0)
    def _():
        xn_ref[...] = _rmsnorm_rows(x_ref[0], g_ref[...]).astype(BF16)

    r = jnp.dot(xn_ref[...], w_ref[...], preferred_element_type=F32)
    if not slab:
        o_ref[0] = r.astype(o_ref.dtype)
    elif dil == 1:
        for c in range(r.shape[1] // LANE):
            o_ref[0, c, 0] = r[:, c * LANE:(c + 1) * LANE].astype(o_ref.dtype)
    else:
        r_ref, = rest
        sub = r.shape[0] // dil
        for c in range(r.shape[1] // LANE):
            r_ref[c] = r[:, c * LANE:(c + 1) * LANE]
            for rc in range(dil):
                o_ref[0, c, rc] = r_ref[c, pl.ds(rc, sub, stride=dil), :].astype(o_ref.dtype)


def _norm_matmul(h, g, w, *, out_dtype, tm, tn, slab=False, dil=1, name):
    bsz, t_len, d = h.shape
    n = w.shape[1]
    assert t_len % tm == 0 and n % tn == 0 and tn % LANE == 0 and tm % (2 * SUBLANE * dil) == 0
    scratch = [pltpu.VMEM((tm, d), BF16)]
    if slab:
        out_shape = jax.ShapeDtypeStruct((bsz, n // LANE, dil, t_len // dil, LANE), out_dtype)
        out_spec = pl.BlockSpec((1, tn // LANE, dil, tm // dil, LANE),
                                lambda b, i, j: (b, j, 0, i, 0))
        if dil > 1:
            scratch.append(pltpu.VMEM((tn // LANE, tm, LANE), F32))
    else:
        out_shape = jax.ShapeDtypeStruct((bsz, t_len, n), out_dtype)
        out_spec = pl.BlockSpec((1, tm, tn), lambda b, i, j: (b, i, j))
    return pl.pallas_call(
        functools.partial(_nmm_body, slab=slab, dil=dil),
        out_shape=out_shape,
        grid=(bsz, t_len // tm, n // tn),
        in_specs=[
            pl.BlockSpec((1, tm, d), lambda b, i, j: (b, i, 0)),
            pl.BlockSpec((1, d), lambda b, i, j: (0, 0)),
            pl.BlockSpec((d, tn), lambda b, i, j: (0, j)),
        ],
        out_specs=out_spec,
        scratch_shapes=scratch,
        compiler_params=_cparams(("parallel", "parallel", "arbitrary")),
        name=name,
    )(h, g.reshape(1, d), w)


def _mmr_body(x_ref, w_ref, r_ref, o_ref, *, slab):
    if slab:
        x = jnp.concatenate([x_ref[0, c] for c in range(x_ref.shape[1])], axis=-1)
    else:
        x = x_ref[0]
    o_ref[0] = r_ref[0] + jnp.dot(x, w_ref[...], preferred_element_type=F32)


def _matmul_residual(x, w, res, *, tm, slab=False, name):
    bsz, t_len, n = res.shape
    k = w.shape[0]
    if slab:
        x_spec = pl.BlockSpec((1, k // LANE, tm, LANE), lambda b, i: (b, 0, i, 0))
    else:
        x_spec = pl.BlockSpec((1, tm, k), lambda b, i: (b, i, 0))
    return pl.pallas_call(
        functools.partial(_mmr_body, slab=slab),
        out_shape=jax.ShapeDtypeStruct((bsz, t_len, n), F32),
        grid=(bsz, t_len // tm),
        in_specs=[x_spec, _const_spec((k, n)),
                  pl.BlockSpec((1, tm, n), lambda b, i: (b, i, 0))],
        out_specs=pl.BlockSpec((1, tm, n), lambda b, i: (b, i, 0)),
        compiler_params=_cparams(("parallel", "parallel")),
        name=name,
    )(x, w, res)


def _ffn_body(h_ref, hp_ref, g_ref, wup_ref, cw_ref, cb_ref, wdn_ref, gf_ref, o_ref,
              xn_ref, acc_ref, *, tm, f_hidden, final_norm):
    halo = SUBLANE
    g = g_ref[...]
    keep = (pl.program_id(1) > 0).astype(F32)
    xn_ref[0:halo, :] = (_rmsnorm_rows(hp_ref[0], g) * keep).astype(BF16)
    xn_ref[halo:, :] = _rmsnorm_rows(h_ref[0], g).astype(BF16)
    xn = xn_ref[...]

    def conv(u, col0):
        w = cw_ref[:, col0:col0 + F_CHUNK]
        b = cb_ref[:, col0:col0 + F_CHUNK]
        return (w[2:3] * u[halo:halo + tm] + w[1:2] * u[halo - 1:halo - 1 + tm]
                + w[0:1] * u[halo - 2:halo - 2 + tm] + b)

    for c in range(f_hidden // F_CHUNK):
        c0 = c * F_CHUNK
        ug = jnp.dot(xn, wup_ref[:, c0:c0 + F_CHUNK], preferred_element_type=F32)
        uv = jnp.dot(xn, wup_ref[:, f_hidden + c0:f_hidden + c0 + F_CHUNK],
                     preferred_element_type=F32)
        gate = conv(ug, c0)
        val = conv(uv, f_hidden + c0)
        act = (gate * _sigmoid(gate) * val).astype(BF16)
        part = jnp.dot(act, wdn_ref[c0:c0 + F_CHUNK, :], preferred_element_type=F32)
        if c == 0:
            acc_ref[...] = part
        else:
            acc_ref[...] += part

    out = h_ref[0] + acc_ref[...]
    if final_norm:
        out = _rmsnorm_rows(out, gf_ref[...])
    o_ref[0] = out


def _conv_ffn(h, g, w_up, conv_w, conv_b, w_down, g_final, *, tm, final_norm, name):
    bsz, t_len, d = h.shape
    f2 = w_up.shape[1]
    f_hidden = f2 // 2
    assert f_hidden % F_CHUNK == 0 and t_len % tm == 0 and tm % SUBLANE == 0
    rows_per_tile = tm // SUBLANE
    return pl.pallas_call(
        functools.partial(_ffn_body, tm=tm, f_hidden=f_hidden, final_norm=final_norm),
        out_shape=jax.ShapeDtypeStruct((bsz, t_len, d), F32),
        grid=(bsz, t_len // tm),
        in_specs=[
            pl.BlockSpec((1, tm, d), lambda b, i: (b, i, 0)),
            pl.BlockSpec((1, SUBLANE, d),
                         lambda b, i: (b, jnp.maximum(i * rows_per_tile - 1, 0), 0)),
            _const_spec((1, d)),
            _const_spec((d, f2)),
            _const_spec((3, f2)),
            _const_spec((1, f2)),
            _const_spec((f_hidden, d)),
            _const_spec((1, d)),
        ],
        out_specs=pl.BlockSpec((1, tm, d), lambda b, i: (b, i, 0)),
        scratch_shapes=[pltpu.VMEM((tm + SUBLANE, d), BF16), pltpu.VMEM((tm, d), F32)],
        compiler_params=_cparams(("parallel", "parallel")),
        name=name,
    )(h, h, g.reshape(1, d), w_up, conv_w, conv_b.reshape(1, f2), w_down, g_final.reshape(1, d))


def _mlstm_body(xm_ref, op_ref, gt_ref, bg_ref, cw_ref, cb_ref, wq_ref, wk_ref, wv_ref,
                gn_ref, sk_ref, y_ref, c_ref, n_ref, m_ref, xp_ref, *, chunk, dh):
    L = chunk
    halo = SUBLANE

    @pl.when(pl.program_id(1) == 0)
    def _():
        c_ref[...] = jnp.zeros_like(c_ref)
        n_ref[...] = jnp.zeros_like(n_ref)
        m_ref[...] = jnp.zeros_like(m_ref)
        xp_ref[...] = jnp.zeros_like(xp_ref)

    xm = xm_ref[0]
    xe = jnp.concatenate([xp_ref[...], xm], axis=0)
    cw = cw_ref[...]
    xc = (cw[3:4] * xe[halo:halo + L] + cw[2:3] * xe[halo - 1:halo - 1 + L]
          + cw[1:2] * xe[halo - 2:halo - 2 + L] + cw[0:1] * xe[halo - 3:halo - 3 + L]
          + cb_ref[...])
    xc = xc * _sigmoid(xc)
    xp_ref[...] = xm[L - halo:L]

    gates = gt_ref[0] + bg_ref[...]
    ii = lax.broadcasted_iota(jnp.int32, (L, L), 0)
    jj = lax.broadcasted_iota(jnp.int32, (L, L), 1)
    eye = ii == jj
    tril = jj <= ii

    def to_row(col):
        return jnp.sum(jnp.where(eye, col, 0.0), axis=0, keepdims=True)

    for hd in range(A_HEADS):
        lo, hi = hd * dh, (hd + 1) * dh
        xc_h = xc[:, lo:hi]
        xcb = xc_h.astype(BF16)
        xmb = xm[:, lo:hi].astype(BF16)
        q = jnp.dot(xcb, wq_ref[hd], preferred_element_type=F32)
        k = jnp.dot(xcb, wk_ref[hd], preferred_element_type=F32) * (dh ** -0.5)
        v = jnp.dot(xmb, wv_ref[hd], preferred_element_type=F32)
        qb = q.astype(BF16)
        kb = k.astype(BF16)
        vb = v.astype(BF16)

        li = gates[:, hd:hd + 1]
        fpre = gates[:, A_HEADS + hd:A_HEADS + hd + 1]
        lf = jnp.minimum(fpre, 0.0) - jnp.log1p(jnp.exp(-jnp.abs(fpre)))
        li_row = to_row(li)
        lf_row = to_row(lf)
        b_col = jnp.sum(jnp.where(tril, lf_row, 0.0), axis=1, keepdims=True)
        b_row = to_row(b_col)
        m_st = m_ref[hd][:, 0:1]

        log_d = jnp.where(tril, b_col - b_row + li_row, NEG)
        log_inter = b_col + m_st
        m = jnp.maximum(log_inter, jnp.max(log_d, axis=1, keepdims=True))
        w_intra = jnp.exp(log_d - m)
        w_inter = jnp.exp(log_inter - m)
        s = lax.dot_general(qb, kb, (((1,), (1,)), ((), ())),
                            preferred_element_type=F32) * w_intra
        num = (w_inter * jnp.dot(qb, c_ref[hd].astype(BF16), preferred_element_type=F32)
               + jnp.dot(s.astype(BF16), vb, preferred_element_type=F32))
        den = (w_inter * jnp.sum(q * n_ref[hd], axis=1, keepdims=True)
               + jnp.sum(s, axis=1, keepdims=True))
        hh = num / jnp.maximum(jnp.abs(den), jnp.exp(-m))

        b_last = b_col[L - 1:L, :]
        log_w = b_last - b_col + li
        m_new = jnp.maximum(b_last + m_st, jnp.max(log_w, axis=0, keepdims=True))
        decay = jnp.exp(b_last + m_st - m_new)
        wk = jnp.exp(log_w - m_new) * k
        c_ref[hd] = decay * c_ref[hd] + lax.dot_general(
            wk.astype(BF16), vb, (((0,), (0,)), ((), ())), preferred_element_type=F32)
        n_ref[hd] = decay * n_ref[hd] + jnp.sum(wk, axis=0, keepdims=True)
        m_ref[hd] = jnp.broadcast_to(m_new, m_ref.shape[1:])

        mu = jnp.mean(hh, axis=-1, keepdims=True)
        ctr = hh - mu
        var = jnp.mean(ctr * ctr, axis=-1, keepdims=True)
        hn = ctr * lax.rsqrt(var + NORM_EPS) * gn_ref[:, lo:hi]
        y = _sigmoid(op_ref[0, :, lo:hi]) * hn + sk_ref[:, lo:hi] * xc_h
        y_ref[0, :, lo:hi] = y.astype(y_ref.dtype)


def _mlstm_core(proj, b_gate, conv_w, conv_b, wq, wk, wv, gn, skip, *, inner):
    bsz, t_len, _ = proj.shape
    dh = inner // A_HEADS
    L = A_CHUNK
    gate_blk = 2 * inner // LANE
    bg = jnp.zeros((1, LANE), F32).at[0, :2 * A_HEADS].set(b_gate)
    return pl.pallas_call(
        functools.partial(_mlstm_body, chunk=L, dh=dh),
        out_shape=jax.ShapeDtypeStruct((bsz, t_len, inner), BF16),
        grid=(bsz, t_len // L),
        in_specs=[
            pl.BlockSpec((1, L, inner), lambda b, c: (b, c, 0)),
            pl.BlockSpec((1, L, inner), lambda b, c: (b, c, 1)),
            pl.BlockSpec((1, L, LANE), lambda b, c: (b, c, gate_blk)),
            _const_spec((1, LANE)),
            _const_spec((A_CONV, inner)),
            _const_spec((1, inner)),
            _const_spec((A_HEADS, dh, dh)),
            _const_spec((A_HEADS, dh, dh)),
            _const_spec((A_HEADS, dh, dh)),
            _const_spec((1, inner)),
            _const_spec((1, inner)),
        ],
        out_specs=pl.BlockSpec((1, L, inner), lambda b, c: (b, c, 0)),
        scratch_shapes=[
            pltpu.VMEM((A_HEADS, dh, dh), F32),
            pltpu.VMEM((A_HEADS, 1, dh), F32),
            pltpu.VMEM((A_HEADS, 1, LANE), F32),
            pltpu.VMEM((SUBLANE, inner), F32),
        ],
        compiler_params=_cparams(("parallel", "arbitrary")),
        name="mlstm_core",
    )(proj, proj, proj, bg, conv_w, conv_b.reshape(1, inner), wq, wk, wv,
      gn.reshape(1, inner), skip.reshape(1, inner))


def _mlstm_layer(h, g, w_in, b_gate, conv_w, conv_b, wq, wk, wv, gn, skip, w_out):
    d = h.shape[-1]
    inner = w_out.shape[0]
    w_a = jnp.concatenate(
        [w_in[:, :2 * inner], w_in[:, 2 * inner:], jnp.zeros((d, LANE - 2 * A_HEADS), F32)],
        axis=1).astype(BF16)
    proj = _norm_matmul(h, g, w_a, out_dtype=F32, tm=512, tn=w_a.shape[1] // 3, name="mlstm_in")
    y = _mlstm_core(proj, b_gate, conv_w, conv_b, wq.astype(BF16), wk.astype(BF16),
                    wv.astype(BF16), gn, skip, inner=inner)
    return _matmul_residual(y, w_out.astype(BF16), h, tm=512, name="mlstm_out")


def _cmp_body(x_ref, pe_ref, w1_ref, b1_ref, w2_ref, o_ref):
    half = x_ref.shape[-1]
    x = x_ref[0, 0, 0]
    pe = pe_ref[0]
    a = (x + pe[:, :half]).astype(BF16)
    b = (x + pe[:, half:]).astype(BF16)
    w1 = w1_ref[0]
    pa = jnp.dot(a, w1[:half], preferred_element_type=F32)
    pb = jnp.dot(b, w1[half:], preferred_element_type=F32)
    pb_next = jnp.concatenate([pb[1:], jnp.zeros((1, pb.shape[1]), F32)], axis=0)
    pre = pa + pb_next + b1_ref[0]
    hid = (pre * _sigmoid(pre)).astype(BF16)
    out = jnp.dot(hid, w2_ref[0], preferred_element_type=F32)
    row = lax.broadcasted_iota(jnp.int32, out.shape, 0)
    o_ref[0, 0, 0] = jnp.where(row < out.shape[0] - 1, out, 0.0).astype(o_ref.dtype)


def _nsa_compress(kv_cmp, pe, w1, b1, w2):
    two, bsz, g, t_len, dh = kv_cmp.shape
    n_half = t_len // B_CMP_STRIDE
    half = B_CMP_STRIDE * dh
    x = kv_cmp.reshape(two, bsz, g, n_half, half)
    hidden = w1.shape[-1]
    return pl.pallas_call(
        _cmp_body,
        out_shape=jax.ShapeDtypeStruct((two, bsz, g, n_half, dh), BF16),
        grid=(two, bsz, g),
        in_specs=[
            pl.BlockSpec((1, 1, 1, n_half, half), lambda s, b, gi: (s, b, gi, 0, 0)),
            pl.BlockSpec((1, 1, 2 * half), lambda s, b, gi: (s, 0, 0)),
            pl.BlockSpec((1, 2 * half, hidden), lambda s, b, gi: (s, 0, 0)),
            pl.BlockSpec((1, 1, hidden), lambda s, b, gi: (s, 0, 0)),
            pl.BlockSpec((1, hidden, dh), lambda s, b, gi: (s, 0, 0)),
        ],
        out_specs=pl.BlockSpec((1, 1, 1, n_half, dh), lambda s, b, gi: (s, b, gi, 0, 0)),
        compiler_params=_cparams(("parallel", "parallel", "parallel")),
        name="nsa_compress",
    )(x, pe.reshape(two, 1, 2 * half), w1, b1.reshape(two, 1, hidden), w2)


def _flash_t(q_ts, k_ref, vt_ref, acc_ref, m_ref, l_ref, kt_lo, kt_hi, bias_fn, tk):
    acc_ref[...] = jnp.zeros_like(acc_ref)
    m_ref[...] = jnp.full_like(m_ref, NEG)
    l_ref[...] = jnp.zeros_like(l_ref)

    def body(kt, carry):
        ks = pl.multiple_of(kt * tk, tk)
        for g, q_t in enumerate(q_ts):
            k = k_ref[0, g, pl.ds(ks, tk), :]
            v_t = vt_ref[0, g, :, pl.ds(ks, tk)]
            s = jnp.dot(k, q_t, preferred_element_type=F32) + bias_fn(g, kt, ks)
            m = m_ref[g]
            m_new = jnp.maximum(m, jnp.max(s, axis=0, keepdims=True))
            alpha = jnp.exp(m - m_new)
            p = jnp.exp(s - m_new)
            m_ref[g] = m_new
            l_ref[g] = alpha * l_ref[g] + jnp.sum(p, axis=0, keepdims=True)
            acc_ref[g] = alpha * acc_ref[g] + jnp.dot(v_t, p.astype(BF16),
                                                      preferred_element_type=F32)
        return carry

    lax.fori_loop(kt_lo, kt_hi, body, 0)
    return [acc_ref[g] / l_ref[g] for g in range(len(q_ts))]


def _nsa_body(q_ref, kc_ref, vc_ref, ks_ref, vs_ref, kw_ref, vw_ref, gp_ref, bg_ref,
              ovl_ref, o_ref, sel_ref, acc_ref, m_ref, l_ref, *, tq, tk, t_len):
    qi = pl.program_id(1)
    t0 = qi * tq
    gw = B_QPG * B_DH
    ncp = kc_ref.shape[2]
    nsel = t_len // B_SEL_LEN
    n_top = min(B_TOPN, nsel)

    def lanes4(x):
        return jnp.concatenate([x] * B_QPG, axis=1)

    n_idx = lax.broadcasted_iota(jnp.int32, (ncp, tq), 0)
    t_col = t0 + lax.broadcasted_iota(jnp.int32, (ncp, tq), 1)
    cmask = (n_idx * B_CMP_STRIDE + (B_CMP_LEN - 1) <= t_col) & (n_idx < ncp - 1)
    cbias = lanes4(jnp.where(cmask, 0.0, NEG))
    ckeep = lanes4(cmask.astype(F32))
    blk = lax.broadcasted_iota(jnp.int32, (nsel, tq), 0)
    cur = (t0 + lax.broadcasted_iota(jnp.int32, (nsel, tq), 1)) // B_SEL_LEN
    valid = blk <= cur
    forced = ((blk == 0) | (cur - blk < B_N_LOCAL)) & valid
    valid_row = valid.astype(F32)

    q_ts, o_cmps = [], []
    for g in range(B_GROUPS):
        q_full = q_ref[0, :, g * gw:(g + 1) * gw].astype(F32).T.astype(BF16)
        q_t = jnp.concatenate([q_full[r * B_DH:(r + 1) * B_DH, :] for r in range(B_QPG)], axis=1)
        q_ts.append(q_t)

        s = jnp.dot(kc_ref[0, g], q_t, preferred_element_type=F32) + cbias
        m = jnp.max(s, axis=0, keepdims=True)
        e = jnp.exp(s - m) * ckeep
        den = jnp.sum(e, axis=0, keepdims=True)
        p = e / jnp.where(den > 0, den, 1.0)
        o_cmps.append(lax.dot_general(vc_ref[0, g], p.astype(BF16), (((0,), (0,)), ((), ())),
                                      preferred_element_type=F32))

        psum = p[:, 0:tq]
        for r in range(1, B_QPG):
            psum = psum + p[:, r * tq:(r + 1) * tq]
        imp = jnp.dot(ovl_ref[...], psum, precision=lax.Precision.HIGHEST,
                      preferred_element_type=F32)
        imp = imp[0:nsel]
        score = jnp.where(forced, FORCE_SCORE, jnp.where(valid, imp, -FORCE_SCORE))
        for j in range(nsel):
            sj = score[j:j + 1, :]
            ahead = (score > sj) | ((score == sj) & (blk < j))
            cnt = jnp.sum(ahead.astype(F32), axis=0, keepdims=True)
            sel_j = jnp.where(cnt < n_top, valid_row[j:j + 1, :], 0.0)
            sel_ref[g, j] = jnp.broadcast_to(sel_j, (SUBLANE, tq))

    blocks_per_tile = tk // B_SEL_LEN
    rep = B_SEL_LEN // SUBLANE
    k_off = lax.broadcasted_iota(jnp.int32, (tk, tq), 0)
    t_pos = t0 + lax.broadcasted_iota(jnp.int32, (tk, tq), 1)

    def sel_bias(g, kt, ks):
        picked = jnp.concatenate(
            [sel_ref[g, kt * blocks_per_tile + i] for i in range(blocks_per_tile)
             for _ in range(rep)], axis=0)
        ok = (picked > 0.5) & (ks + k_off <= t_pos)
        return lanes4(jnp.where(ok, 0.0, NEG))

    kt_hi = (t0 + tq + tk - 1) // tk
    o_sels = _flash_t(q_ts, ks_ref, vs_ref, acc_ref, m_ref, l_ref, 0, kt_hi, sel_bias, tk)

    def win_bias(g, kt, ks):
        diff = t_pos - (ks + k_off)
        return lanes4(jnp.where((diff >= 0) & (diff <= B_WINDOW - 1), 0.0, NEG))

    w_lo = jnp.maximum(t0 - (B_WINDOW - 1), 0) // tk
    o_wins = _flash_t(q_ts, kw_ref, vw_ref, acc_ref, m_ref, l_ref, w_lo, kt_hi, win_bias, tk)

    for g in range(B_GROUPS):
        gate = _sigmoid(gp_ref[0, g] + bg_ref[g])
        outs = []
        for r in range(B_QPG):
            sl = slice(r * tq, (r + 1) * tq)
            outs.append(gate[r:r + 1, :] * o_cmps[g][:, sl]
                        + gate[B_QPG + r:B_QPG + r + 1, :] * o_sels[g][:, sl]
                        + gate[2 * B_QPG + r:2 * B_QPG + r + 1, :] * o_wins[g][:, sl])
        o_ref[0, :, g * gw:(g + 1) * gw] = jnp.concatenate(outs, axis=0).T.astype(o_ref.dtype)


def _nsa_attention(q, kc, vc, k_sel, vt_sel, k_win, vt_win, gpre_t, bg):
    bsz, t_len, _ = q.shape
    tq, tk = B_TQ, B_TK
    ncp = kc.shape[2]
    nsel = t_len // B_SEL_LEN
    gw = B_QPG * B_DH
    c_start = jnp.arange(ncp)[None, :] * B_CMP_STRIDE
    s_start = jnp.arange(LANE)[:, None] * B_SEL_LEN
    ovl = ((c_start < s_start + B_SEL_LEN) & (c_start + B_CMP_LEN > s_start)
           & (jnp.arange(LANE)[:, None] < nsel) & (jnp.arange(ncp)[None, :] < ncp - 1)).astype(F32)

    def k_spec(n_rows):
        return pl.BlockSpec((1, B_GROUPS, n_rows, B_DH), lambda b, i: (b, 0, 0, 0))

    vt_spec = pl.BlockSpec((1, B_GROUPS, B_DH, t_len), lambda b, i: (b, 0, 0, 0))
    n_q = B_QPG * tq

    return pl.pallas_call(
        functools.partial(_nsa_body, tq=tq, tk=tk, t_len=t_len),
        out_shape=jax.ShapeDtypeStruct((bsz, t_len, B_GROUPS * gw), BF16),
        grid=(bsz, t_len // tq),
        in_specs=[
            pl.BlockSpec((1, tq, B_GROUPS * gw), lambda b, i: (b, i, 0)),
            k_spec(ncp), k_spec(ncp), k_spec(t_len), vt_spec, k_spec(t_len), vt_spec,
            pl.BlockSpec((1, B_GROUPS, 16, tq), lambda b, i: (b, 0, 0, i)),
            _const_spec((B_GROUPS, 16, tq)),
            _const_spec((LANE, ncp)),
        ],
        out_specs=pl.BlockSpec((1, tq, B_GROUPS * gw), lambda b, i: (b, i, 0)),
        scratch_shapes=[pltpu.VMEM((B_GROUPS, nsel, SUBLANE, tq), F32),
                        pltpu.VMEM((B_GROUPS, B_DH, n_q), F32),
                        pltpu.VMEM((B_GROUPS, 1, n_q), F32),
                        pltpu.VMEM((B_GROUPS, 1, n_q), F32)],
        compiler_params=_cparams(("parallel", "parallel")),
        name="nsa_attention",
    )(q, kc, vc, k_sel, vt_sel, k_win, vt_win, gpre_t, bg, ovl)


def _nsa_layer(h, g, w_in, b_gate, pe_k, pe_v, phik_w1, phik_b1, phik_w2,
               phiv_w1, phiv_b1, phiv_w2, w_out):
    bsz, t_len, d = h.shape
    nq = B_HEADS * B_DH
    nkv = 6 * B_GROUPS * B_DH
    wg = w_in[:, nq + nkv:].reshape(d, B_GROUPS, B_QPG, 3).transpose(0, 1, 3, 2)
    wg = jnp.pad(wg.reshape(d, B_GROUPS, 3 * B_QPG), ((0, 0), (0, 0), (0, 16 - 3 * B_QPG)))
    wg = jnp.pad(wg.reshape(d, B_GROUPS * 16), ((0, 0), (0, LANE - B_GROUPS * 16)))
    bg = b_gate.reshape(B_GROUPS, B_QPG, 3).transpose(0, 2, 1).reshape(B_GROUPS, 3 * B_QPG)
    bg = jnp.pad(bg, ((0, 0), (0, 16 - 3 * B_QPG)))
    bg = jnp.broadcast_to(bg[:, :, None], (B_GROUPS, 16, B_TQ))
    w_b = jnp.concatenate([w_in[:, :nq + nkv], wg], axis=1).astype(BF16)
    proj = _norm_matmul(h, g, w_b, out_dtype=F32, tm=512, tn=w_b.shape[1] // 3, name="nsa_in")

    q = (proj[..., :nq] * (B_DH ** -0.5)).astype(BF16)
    kv = proj[..., nq:nq + nkv].reshape(bsz, t_len, 6, B_GROUPS, B_DH)
    kv_cmp = kv[:, :, 0:2].transpose(2, 0, 3, 1, 4)
    k_sel, k_win = (kv[:, :, i].transpose(0, 2, 1, 3).astype(BF16) for i in (2, 4))
    vt_sel, vt_win = (kv[:, :, i].transpose(0, 2, 3, 1).astype(BF16) for i in (3, 5))
    gpre_t = proj[..., nq + nkv:nq + nkv + B_GROUPS * 16].reshape(bsz, t_len, B_GROUPS, 16)
    gpre_t = gpre_t.transpose(0, 2, 3, 1)

    cmp = _nsa_compress(
        kv_cmp, jnp.stack([pe_k, pe_v]), jnp.stack([phik_w1, phiv_w1]).astype(BF16),
        jnp.stack([phik_b1, phiv_b1]), jnp.stack([phik_w2, phiv_w2]).astype(BF16))
    o = _nsa_attention(q, cmp[0], cmp[1], k_sel, vt_sel, k_win, vt_win, gpre_t, bg)
    return _matmul_residual(o, w_out.astype(BF16), h, tm=512, name="nsa_out")


def _dil_body(*refs, t_len):
    qkv = refs[:9]
    o_ref = refs[9]
    og_ref, lg_ref = refs[10], refs[11]
    qb = C_QBLK
    assert qb == LANE and 2 * C_DH == LANE
    top = lax.broadcasted_iota(jnp.int32, (LANE, qb), 0) < C_DH

    for gi, (window, dil) in enumerate(C_PATTERNS):
        sub = t_len // dil
        nb = sub // qb
        nk = qb if nb == 1 else 2 * qb

        def body(i, carry, gi=gi, dil=dil, sub=sub, nb=nb, nk=nk, window=window):
            rc = i // nb
            bq = i % nb
            bk = jnp.maximum(bq - 1, 0)
            q0 = pl.multiple_of(rc * sub + bq * qb, qb)
            k0 = pl.multiple_of(rc * sub + bk * qb, qb)
            q2 = qkv[3 * gi][0, 0, pl.ds(q0, qb), :]
            k2 = qkv[3 * gi + 1][0, 0, pl.ds(k0, nk), :]
            v2 = qkv[3 * gi + 2][0, 0, pl.ds(k0, nk), :]
            q_t = q2.astype(F32).T
            q_bd = jnp.concatenate([jnp.where(top, q_t, 0.0), jnp.where(top, 0.0, q_t)],
                                   axis=1).astype(BF16)
            s = jnp.dot(k2, q_bd, preferred_element_type=F32)
            diff = ((bq - bk) * qb + lax.broadcasted_iota(jnp.int32, (nk, qb), 1)
                    - lax.broadcasted_iota(jnp.int32, (nk, qb), 0))
            bias = jnp.where((diff >= 0) & (diff <= window // dil), 0.0, NEG)
            s = s + jnp.concatenate([bias, bias], axis=1)
            m = jnp.max(s, axis=0, keepdims=True)
            e = jnp.exp(s - m)
            den = jnp.sum(e, axis=0, keepdims=True)
            pv = lax.dot_general(v2, e.astype(BF16), (((0,), (0,)), ((), ())),
                                 preferred_element_type=F32)
            inv = 1.0 / den
            lse = m + jnp.log(den)
            o_t = jnp.concatenate([pv[0:C_DH, 0:qb] * inv[:, 0:qb],
                                   pv[C_DH:, qb:] * inv[:, qb:]], axis=0)
            l_t = jnp.concatenate([jnp.broadcast_to(lse[:, 0:qb], (C_DH, qb)),
                                   jnp.broadcast_to(lse[:, qb:], (C_DH, qb))], axis=0)
            if dil == 1:
                rows = pl.ds(q0, qb)
            else:
                rows = pl.ds(rc + bq * (qb * dil), qb, stride=dil)
            og_ref[gi, rows, :] = o_t.T
            lg_ref[gi, rows, :] = l_t.T
            return carry

        lax.fori_loop(0, dil * nb, body, 0, unroll=2)

    lse = lg_ref[...]
    mx = jnp.max(lse, axis=0, keepdims=True)
    w = jnp.exp(lse - mx)
    alpha = w / jnp.sum(w, axis=0, keepdims=True)
    o_ref[0, 0] = jnp.sum(alpha * og_ref[...], axis=0).astype(o_ref.dtype)


def _dilated_attention(projs):
    bsz, _, t_len, _ = projs[0].shape
    npairs = C_HEADS * C_DH // LANE
    ng = len(C_PATTERNS)

    def spec(col):
        return pl.BlockSpec((1, 1, t_len, LANE), lambda b, hp, col=col: (b, col * npairs + hp, 0, 0))

    return pl.pallas_call(
        functools.partial(_dil_body, t_len=t_len),
        out_shape=jax.ShapeDtypeStruct((bsz, npairs, t_len, LANE), BF16),
        grid=(bsz, npairs),
        in_specs=[spec(c) for _ in range(ng) for c in range(3)],
        out_specs=pl.BlockSpec((1, 1, t_len, LANE), lambda b, hp: (b, hp, 0, 0)),
        scratch_shapes=[pltpu.VMEM((ng, t_len, LANE), F32), pltpu.VMEM((ng, t_len, LANE), F32)],
        compiler_params=_cparams(("parallel", "parallel")),
        name="dilated_attention",
    )(*[projs[gi] for gi in range(ng) for _ in range(3)])


def _dilated_layer(h, g, w_in, w_out):
    bsz, t_len, d = h.shape
    gw = C_HEADS * C_DH
    scale = jnp.ones((3, 1), F32).at[0, :].set(C_DH ** -0.5)
    w_c = w_in.reshape(d, len(C_PATTERNS), 3, gw)
    projs = []
    for gi, (_, dil) in enumerate(C_PATTERNS):
        w_g = (w_c[:, gi] * scale).reshape(d, 3 * gw).astype(BF16)
        p = _norm_matmul(h, g, w_g, out_dtype=BF16, tm=512, tn=3 * gw // 2, slab=True, dil=dil,
                         name=f"dilated_in{gi}")
        projs.append(p.reshape(bsz, 3 * gw // LANE, t_len, LANE))
    o = _dilated_attention(projs)
    return _matmul_residual(o, w_out.astype(BF16), h, tm=512, slab=True, name="dilated_out")


def kernel(x, norm_mix, norm_ffn, norm_final, a_w_in, a_b_gate, a_conv_w, a_conv_b, a_wq, a_wk, a_wv, a_gn, a_skip, a_w_out, b_w_in, b_b_gate, b_pe_k, b_pe_v, b_phik_w1, b_phik_b1, b_phik_w2, b_phiv_w1, b_phiv_b1, b_phiv_w2, b_w_out, c_w_in, c_w_out, f_w_up, f_conv_w, f_conv_b, f_w_down):
    depth = norm_mix.shape[0]
    h = x
    for i in range(depth):
        kind, j = i % 3, i // 3
        if kind == 0:
            h = _mlstm_layer(h, norm_mix[i], a_w_in[j], a_b_gate[j], a_conv_w[j], a_conv_b[j],
                             a_wq[j], a_wk[j], a_wv[j], a_gn[j], a_skip[j], a_w_out[j])
        elif kind == 1:
            h = _nsa_layer(h, norm_mix[i], b_w_in[j], b_b_gate[j], b_pe_k[j], b_pe_v[j],
                           b_phik_w1[j], b_phik_b1[j], b_phik_w2[j],
                           b_phiv_w1[j], b_phiv_b1[j], b_phiv_w2[j], b_w_out[j])
        else:
            h = _dilated_layer(h, norm_mix[i], c_w_in[j], c_w_out[j])
        h = _conv_ffn(h, norm_ffn[i], f_w_up[i].astype(BF16), f_conv_w[i], f_conv_b[i],
                      f_w_down[i].astype(BF16), norm_final, tm=512,
                      final_norm=(i == depth - 1), name="conv_ffn")
    return h
```

```python
import functools

import jax
import jax.numpy as jnp
from jax import lax
from jax.experimental import pallas as pl
from jax.experimental.pallas import tpu as pltpu

F32 = jnp.float32
BF16 = jnp.bfloat16

NORM_EPS = 1e-6
NEG = -1e30
LANE = 128
SUBLANE = 8
VMEM_LIMIT = 56 * 1024 * 1024

A_HEADS = 4
A_CHUNK = 256
A_CONV = 4
B_HEADS = 16
B_DH = 64
B_GROUPS = 4
B_QPG = 4
B_CMP_LEN = 32
B_CMP_STRIDE = 16
B_SEL_LEN = 64
B_TOPN = 16
B_N_LOCAL = 2
B_WINDOW = 512
FORCE_SCORE = 1e9
B_TQ = 128
B_TK = 256
C_PATTERNS = ((128, 1), (512, 4), (2048, 16))
C_HEADS = 16
C_DH = 64
C_QBLK = 128
C_BLOCKS_PER_ITER = 4
F_CHUNK = 256


def _cparams(sem):
    return pltpu.CompilerParams(dimension_semantics=sem, vmem_limit_bytes=VMEM_LIMIT)


def _const_spec(shape):
    nd = len(shape)
    return pl.BlockSpec(shape, lambda *_: (0,) * nd, pipeline_mode=pl.Buffered(1))


def _rmsnorm_rows(x, g):
    ms = jnp.mean(x * x, axis=-1, keepdims=True)
    return x * lax.rsqrt(ms + NORM_EPS) * g


def _sigmoid(x):
    return 1.0 / (1.0 + jnp.exp(-x))


def _nmm_body(x_ref, g_ref, w_ref, o_ref, xn_ref, *rest, slab, dil):
    @pl.when(pl.program_id(2) == 0)
    def _():
        xn_ref[...] = _rmsnorm_rows(x_ref[0], g_ref[...]).astype(BF16)

    r = jnp.dot(xn_ref[...], w_ref[...], preferred_element_type=F32)
    if not slab:
        o_ref[0] = r.astype(o_ref.dtype)
    elif dil == 1:
        for c in range(r.shape[1] // LANE):
            o_ref[0, c, 0] = r[:, c * LANE:(c + 1) * LANE].astype(o_ref.dtype)
    else:
        r_ref, = rest
        sub = r.shape[0] // dil
        for c in range(r.shape[1] // LANE):
            r_ref[c] = r[:, c * LANE:(c + 1) * LANE]
            for rc in range(dil):
                o_ref[0, c, rc] = r_ref[c, pl.ds(rc, sub, stride=dil), :].astype(o_ref.dtype)


def _norm_matmul(h, g, w, *, out_dtype, tm, tn, slab=False, dil=1, name):
    bsz, t_len, d = h.shape
    n = w.shape[1]
    assert t_len % tm == 0 and n % tn == 0 and tn % LANE == 0 and tm % (2 * SUBLANE * dil) == 0
    scratch = [pltpu.VMEM((tm, d), BF16)]
    if slab:
        out_shape = jax.ShapeDtypeStruct((bsz, n // LANE, dil, t_len // dil, LANE), out_dtype)
        out_spec = pl.BlockSpec((1, tn // LANE, dil, tm // dil, LANE),
                                lambda b, i, j: (b, j, 0, i, 0))
        if dil > 1:
            scratch.append(pltpu.VMEM((tn // LANE, tm, LANE), F32))
    else:
        out_shape = jax.ShapeDtypeStruct((bsz, t_len, n), out_dtype)
        out_spec = pl.BlockSpec((1, tm, tn), lambda b, i, j: (b, i, j))
    return pl.pallas_call(
        functools.partial(_nmm_body, slab=slab, dil=dil),
        out_shape=out_shape,
        grid=(bsz, t_len // tm, n // tn),
        in_specs=[
            pl.BlockSpec((1, tm, d), lambda b, i, j: (b, i, 0)),
            pl.BlockSpec((1, d), lambda b, i, j: (0, 0)),
            pl.BlockSpec((d, tn), lambda b, i, j: (0, j)),
        ],
        out_specs=out_spec,
        scratch_shapes=scratch,
        compiler_params=_cparams(("parallel", "parallel", "arbitrary")),
        name=name,
    )(h, g.reshape(1, d), w)


def _mmr_body(x_ref, w_ref, r_ref, o_ref, *, slab):
    if slab:
        x = jnp.concatenate([x_ref[0, c] for c in range(x_ref.shape[1])], axis=-1)
    else:
        x = x_ref[0]
    o_ref[0] = r_ref[0] + jnp.dot(x, w_ref[...], preferred_element_type=F32)


def _matmul_residual(x, w, res, *, tm, slab=False, name):
    bsz, t_len, n = res.shape
    k = w.shape[0]
    if slab:
        x_spec = pl.BlockSpec((1, k // LANE, tm, LANE), lambda b, i: (b, 0, i, 0))
    else:
        x_spec = pl.BlockSpec((1, tm, k), lambda b, i: (b, i, 0))
    return pl.pallas_call(
        functools.partial(_mmr_body, slab=slab),
        out_shape=jax.ShapeDtypeStruct((bsz, t_len, n), F32),
        grid=(bsz, t_len // tm),
        in_specs=[x_spec, _const_spec((k, n)),
                  pl.BlockSpec((1, tm, n), lambda b, i: (b, i, 0))],
        out_specs=pl.BlockSpec((1, tm, n), lambda b, i: (b, i, 0)),
        compiler_params=_cparams(("parallel", "parallel")),
        name=name,
    )(x, w, res)


def _ffn_body(h_ref, hp_ref, g_ref, wup_ref, cw_ref, cb_ref, wdn_ref, gf_ref, o_ref,
              xn_ref, acc_ref, *, tm, f_hidden, final_norm):
    halo = SUBLANE
    g = g_ref[...]
    keep = (pl.program_id(1) > 0).astype(F32)
    xn_ref[0:halo, :] = (_rmsnorm_rows(hp_ref[0], g) * keep).astype(BF16)
    xn_ref[halo:, :] = _rmsnorm_rows(h_ref[0], g).astype(BF16)
    xn = xn_ref[...]

    def conv(u, col0):
        w = cw_ref[:, col0:col0 + F_CHUNK]
        b = cb_ref[:, col0:col0 + F_CHUNK]
        return (w[2:3] * u[halo:halo + tm] + w[1:2] * u[halo - 1:halo - 1 + tm]
                + w[0:1] * u[halo - 2:halo - 2 + tm] + b)

    def up(c):
        c0 = c * F_CHUNK
        return (jnp.dot(xn, wup_ref[:, c0:c0 + F_CHUNK], preferred_element_type=F32),
                jnp.dot(xn, wup_ref[:, f_hidden + c0:f_hidden + c0 + F_CHUNK],
                        preferred_element_type=F32))

    n_chunks = f_hidden // F_CHUNK
    nxt = up(0)
    for c in range(n_chunks):
        c0 = c * F_CHUNK
        ug, uv = nxt
        if c + 1 < n_chunks:
            nxt = up(c + 1)
        gate = conv(ug, c0)
        val = conv(uv, f_hidden + c0)
        act = (gate * _sigmoid(gate) * val).astype(BF16)
        part = jnp.dot(act, wdn_ref[c0:c0 + F_CHUNK, :], preferred_element_type=F32)
        if c == 0:
            acc_ref[...] = part
        else:
            acc_ref[...] += part

    out = h_ref[0] + acc_ref[...]
    if final_norm:
        out = _rmsnorm_rows(out, gf_ref[...])
    o_ref[0] = out


def _conv_ffn(h, g, w_up, conv_w, conv_b, w_down, g_final, *, tm, final_norm, name):
    bsz, t_len, d = h.shape
    f2 = w_up.shape[1]
    f_hidden = f2 // 2
    assert f_hidden % F_CHUNK == 0 and t_len % tm == 0 and tm % SUBLANE == 0
    rows_per_tile = tm // SUBLANE
    return pl.pallas_call(
        functools.partial(_ffn_body, tm=tm, f_hidden=f_hidden, final_norm=final_norm),
        out_shape=jax.ShapeDtypeStruct((bsz, t_len, d), F32),
        grid=(bsz, t_len // tm),
        in_specs=[
            pl.BlockSpec((1, tm, d), lambda b, i: (b, i, 0)),
            pl.BlockSpec((1, SUBLANE, d),
                         lambda b, i: (b, jnp.maximum(i * rows_per_tile - 1, 0), 0)),
            _const_spec((1, d)),
            _const_spec((d, f2)),
            _const_spec((3, f2)),
            _const_spec((1, f2)),
            _const_spec((f_hidden, d)),
            _const_spec((1, d)),
        ],
        out_specs=pl.BlockSpec((1, tm, d), lambda b, i: (b, i, 0)),
        scratch_shapes=[pltpu.VMEM((tm + SUBLANE, d), BF16), pltpu.VMEM((tm, d), F32)],
        compiler_params=_cparams(("parallel", "parallel")),
        name=name,
    )(h, h, g.reshape(1, d), w_up, conv_w, conv_b.reshape(1, f2), w_down, g_final.reshape(1, d))


def _mlstm_body(h_ref, g_ref, win_ref, bg_ref, cw_ref, cb_ref, wq_ref, wk_ref, wv_ref,
                gn_ref, sk_ref, wout_ref, o_ref, c_ref, n_ref, m_ref, xp_ref, y_ref,
                *, chunk, dh):
    L = chunk
    halo = SUBLANE
    inner = A_HEADS * dh

    @pl.when(pl.program_id(1) == 0)
    def _():
        c_ref[...] = jnp.zeros_like(c_ref)
        n_ref[...] = jnp.zeros_like(n_ref)
        m_ref[...] = jnp.zeros_like(m_ref)
        xp_ref[...] = jnp.zeros_like(xp_ref)

    xn = _rmsnorm_rows(h_ref[0], g_ref[...]).astype(BF16)
    xm = jnp.dot(xn, win_ref[:, 0:inner], preferred_element_type=F32)
    gates = jnp.dot(xn, win_ref[:, 2 * inner:], preferred_element_type=F32) + bg_ref[...]
    heads = [dict(lo=hd * dh, hi=(hd + 1) * dh) for hd in range(A_HEADS)]
    for hd, st in enumerate(heads):
        st["v"] = jnp.dot(xm[:, st["lo"]:st["hi"]].astype(BF16), wv_ref[hd],
                          preferred_element_type=F32)
    o_pre = jnp.dot(xn, win_ref[:, inner:2 * inner], preferred_element_type=F32)

    xe = jnp.concatenate([xp_ref[...], xm], axis=0)
    cw = cw_ref[...]
    xc = (cw[3:4] * xe[halo:halo + L] + cw[2:3] * xe[halo - 1:halo - 1 + L]
          + cw[1:2] * xe[halo - 2:halo - 2 + L] + cw[0:1] * xe[halo - 3:halo - 3 + L]
          + cb_ref[...])
    xc = xc * _sigmoid(xc)
    xp_ref[...] = xm[L - halo:L]

    ii = lax.broadcasted_iota(jnp.int32, (L, L), 0)
    jj = lax.broadcasted_iota(jnp.int32, (L, L), 1)
    eye = ii == jj
    tril = jj <= ii

    def to_row(col):
        return jnp.sum(jnp.where(eye, col, 0.0), axis=0, keepdims=True)

    for hd, st in enumerate(heads):
        st["xc"] = xc[:, st["lo"]:st["hi"]]
        xcb = st["xc"].astype(BF16)
        st["q"] = jnp.dot(xcb, wq_ref[hd], preferred_element_type=F32)
        st["k"] = jnp.dot(xcb, wk_ref[hd], preferred_element_type=F32)

    for hd, st in enumerate(heads):
        li = gates[:, hd:hd + 1]
        fpre = gates[:, A_HEADS + hd:A_HEADS + hd + 1]
        lf = jnp.minimum(fpre, 0.0) - jnp.log1p(jnp.exp(-jnp.abs(fpre)))
        li_row = to_row(li)
        lf_row = to_row(lf)
        b_col = jnp.sum(jnp.where(tril, lf_row, 0.0), axis=1, keepdims=True)
        b_row = to_row(b_col)
        m_st = m_ref[hd][:, 0:1]
        log_d = jnp.where(tril, b_col - b_row + li_row, NEG)
        log_inter = b_col + m_st
        m = jnp.maximum(log_inter, jnp.max(log_d, axis=1, keepdims=True))
        st["m"] = m
        st["w_intra"] = jnp.exp(log_d - m)
        st["w_inter"] = jnp.exp(log_inter - m)
        b_last = b_col[L - 1:L, :]
        log_w = b_last - b_col + li
        m_new = jnp.maximum(b_last + m_st, jnp.max(log_w, axis=0, keepdims=True))
        st["decay"] = jnp.exp(b_last + m_st - m_new)
        st["w_k"] = jnp.exp(log_w - m_new)
        m_ref[hd] = jnp.broadcast_to(m_new, m_ref.shape[1:])

    for hd, st in enumerate(heads):
        st["k"] = st["k"] * (dh ** -0.5)
        st["qb"] = st["q"].astype(BF16)
        st["vb"] = st["v"].astype(BF16)
        st["s"] = lax.dot_general(st["qb"], st["k"].astype(BF16), (((1,), (1,)), ((), ())),
                                  preferred_element_type=F32)
        st["qc"] = jnp.dot(st["qb"], c_ref[hd].astype(BF16), preferred_element_type=F32)

    for hd, st in enumerate(heads):
        s = st["s"] * st["w_intra"]
        st["den"] = (st["w_inter"] * jnp.sum(st["q"] * n_ref[hd], axis=1, keepdims=True)
                     + jnp.sum(s, axis=1, keepdims=True))
        wk = st["w_k"] * st["k"]
        n_ref[hd] = st["decay"] * n_ref[hd] + jnp.sum(wk, axis=0, keepdims=True)
        st["sv"] = jnp.dot(s.astype(BF16), st["vb"], preferred_element_type=F32)
        st["kv"] = lax.dot_general(wk.astype(BF16), st["vb"], (((0,), (0,)), ((), ())),
                                   preferred_element_type=F32)

    for hd, st in enumerate(heads):
        lo, hi = st["lo"], st["hi"]
        c_ref[hd] = st["decay"] * c_ref[hd] + st["kv"]
        num = st["w_inter"] * st["qc"] + st["sv"]
        hh = num / jnp.maximum(jnp.abs(st["den"]), jnp.exp(-st["m"]))
        mu = jnp.mean(hh, axis=-1, keepdims=True)
        ctr = hh - mu
        var = jnp.mean(ctr * ctr, axis=-1, keepdims=True)
        hn = ctr * lax.rsqrt(var + NORM_EPS) * gn_ref[:, lo:hi]
        y = _sigmoid(o_pre[:, lo:hi]) * hn + sk_ref[:, lo:hi] * st["xc"]
        y_ref[:, lo:hi] = y.astype(y_ref.dtype)

    o_ref[0] = h_ref[0] + jnp.dot(y_ref[...], wout_ref[...], preferred_element_type=F32)


def _mlstm_layer(h, g, w_in, b_gate, conv_w, conv_b, wq, wk, wv, gn, skip, w_out):
    bsz, t_len, d = h.shape
    inner = w_out.shape[0]
    dh = inner // A_HEADS
    L = A_CHUNK
    w_a = jnp.concatenate([w_in, jnp.zeros((d, LANE - 2 * A_HEADS), F32)], axis=1).astype(BF16)
    bg = jnp.zeros((1, LANE), F32).at[0, :2 * A_HEADS].set(b_gate)
    return pl.pallas_call(
        functools.partial(_mlstm_body, chunk=L, dh=dh),
        out_shape=jax.ShapeDtypeStruct((bsz, t_len, d), F32),
        grid=(bsz, t_len // L),
        in_specs=[
            pl.BlockSpec((1, L, d), lambda b, c: (b, c, 0)),
            _const_spec((1, d)),
            _const_spec((d, 2 * inner + LANE)),
            _const_spec((1, LANE)),
            _const_spec((A_CONV, inner)),
            _const_spec((1, inner)),
            _const_spec((A_HEADS, dh, dh)),
            _const_spec((A_HEADS, dh, dh)),
            _const_spec((A_HEADS, dh, dh)),
            _const_spec((1, inner)),
            _const_spec((1, inner)),
            _const_spec((inner, d)),
        ],
        out_specs=pl.BlockSpec((1, L, d), lambda b, c: (b, c, 0)),
        scratch_shapes=[
            pltpu.VMEM((A_HEADS, dh, dh), F32),
            pltpu.VMEM((A_HEADS, 1, dh), F32),
            pltpu.VMEM((A_HEADS, 1, LANE), F32),
            pltpu.VMEM((SUBLANE, inner), F32),
            pltpu.VMEM((L, inner), BF16),
        ],
        compiler_params=_cparams(("parallel", "arbitrary")),
        name="mlstm_layer",
    )(h, g.reshape(1, d), w_a, bg, conv_w, conv_b.reshape(1, inner), wq.astype(BF16),
      wk.astype(BF16), wv.astype(BF16), gn.reshape(1, inner), skip.reshape(1, inner),
      w_out.astype(BF16))


def _cmp_body(x_ref, pe_ref, w1_ref, b1_ref, w2_ref, o_ref):
    half = x_ref.shape[-1]
    x = x_ref[0, 0, 0]
    pe = pe_ref[0]
    a = (x + pe[:, :half]).astype(BF16)
    b = (x + pe[:, half:]).astype(BF16)
    w1 = w1_ref[0]
    pa = jnp.dot(a, w1[:half], preferred_element_type=F32)
    pb = jnp.dot(b, w1[half:], preferred_element_type=F32)
    pb_next = jnp.concatenate([pb[1:], jnp.zeros((1, pb.shape[1]), F32)], axis=0)
    pre = pa + pb_next + b1_ref[0]
    hid = (pre * _sigmoid(pre)).astype(BF16)
    out = jnp.dot(hid, w2_ref[0], preferred_element_type=F32)
    row = lax.broadcasted_iota(jnp.int32, out.shape, 0)
    o_ref[0, 0, 0] = jnp.where(row < out.shape[0] - 1, out, 0.0).astype(o_ref.dtype)


def _nsa_compress(kv_cmp, pe, w1, b1, w2):
    two, bsz, g, t_len, dh = kv_cmp.shape
    n_half = t_len // B_CMP_STRIDE
    half = B_CMP_STRIDE * dh
    x = kv_cmp.reshape(two, bsz, g, n_half, half)
    hidden = w1.shape[-1]
    return pl.pallas_call(
        _cmp_body,
        out_shape=jax.ShapeDtypeStruct((two, bsz, g, n_half, dh), BF16),
        grid=(two, bsz, g),
        in_specs=[
            pl.BlockSpec((1, 1, 1, n_half, half), lambda s, b, gi: (s, b, gi, 0, 0)),
            pl.BlockSpec((1, 1, 2 * half), lambda s, b, gi: (s, 0, 0)),
            pl.BlockSpec((1, 2 * half, hidden), lambda s, b, gi: (s, 0, 0)),
            pl.BlockSpec((1, 1, hidden), lambda s, b, gi: (s, 0, 0)),
            pl.BlockSpec((1, hidden, dh), lambda s, b, gi: (s, 0, 0)),
        ],
        out_specs=pl.BlockSpec((1, 1, 1, n_half, dh), lambda s, b, gi: (s, b, gi, 0, 0)),
        compiler_params=_cparams(("parallel", "parallel", "parallel")),
        name="nsa_compress",
    )(x, pe.reshape(two, 1, 2 * half), w1, b1.reshape(two, 1, hidden), w2)


def _flash_t(q_ts, k_ref, vt_ref, acc_ref, m_ref, l_ref, kt_lo, kt_hi, bias_fn, tk):
    acc_ref[...] = jnp.zeros_like(acc_ref)
    m_ref[...] = jnp.full_like(m_ref, NEG)
    l_ref[...] = jnp.zeros_like(l_ref)

    def body(kt, carry):
        ks = pl.multiple_of(kt * tk, tk)
        ss = [jnp.dot(k_ref[0, g, pl.ds(ks, tk), :], q_t, preferred_element_type=F32)
              for g, q_t in enumerate(q_ts)]
        ps, alphas = [], []
        for g in range(len(q_ts)):
            s = ss[g] + bias_fn(g, kt, ks)
            m = m_ref[g]
            m_new = jnp.maximum(m, jnp.max(s, axis=0, keepdims=True))
            alpha = jnp.exp(m - m_new)
            p = jnp.exp(s - m_new)
            m_ref[g] = m_new
            l_ref[g] = alpha * l_ref[g] + jnp.sum(p, axis=0, keepdims=True)
            ps.append(p.astype(BF16))
            alphas.append(alpha)
        for g in range(len(q_ts)):
            v_t = vt_ref[0, g, :, pl.ds(ks, tk)]
            acc_ref[g] = alphas[g] * acc_ref[g] + jnp.dot(v_t, ps[g],
                                                          preferred_element_type=F32)
        return carry

    lax.fori_loop(kt_lo, kt_hi, body, 0)
    return [acc_ref[g] / l_ref[g] for g in range(len(q_ts))]


def _nsa_body(q_ref, kc_ref, vc_ref, ks_ref, vs_ref, kw_ref, vw_ref, gp_ref, bg_ref,
              ovl_ref, wout_ref, h_ref, o_ref, sel_ref, acc_ref, m_ref, l_ref, og_ref,
              *, tq, tk, t_len):
    qi = pl.program_id(1)
    t0 = qi * tq
    gw = B_QPG * B_DH
    ncp = kc_ref.shape[2]
    nsel = t_len // B_SEL_LEN
    n_top = min(B_TOPN, nsel)

    def lanes4(x):
        return jnp.concatenate([x] * B_QPG, axis=1)

    n_idx = lax.broadcasted_iota(jnp.int32, (ncp, tq), 0)
    t_col = t0 + lax.broadcasted_iota(jnp.int32, (ncp, tq), 1)
    cmask = (n_idx * B_CMP_STRIDE + (B_CMP_LEN - 1) <= t_col) & (n_idx < ncp - 1)
    cbias = lanes4(jnp.where(cmask, 0.0, NEG))
    ckeep = lanes4(cmask.astype(F32))
    blk = lax.broadcasted_iota(jnp.int32, (nsel, tq), 0)
    cur = (t0 + lax.broadcasted_iota(jnp.int32, (nsel, tq), 1)) // B_SEL_LEN
    valid = blk <= cur
    forced = ((blk == 0) | (cur - blk < B_N_LOCAL)) & valid
    valid_row = valid.astype(F32)

    q_ts, o_cmps = [], []
    for g in range(B_GROUPS):
        q_full = q_ref[0, :, g * gw:(g + 1) * gw].astype(F32).T.astype(BF16)
        q_t = jnp.concatenate([q_full[r * B_DH:(r + 1) * B_DH, :] for r in range(B_QPG)], axis=1)
        q_ts.append(q_t)

        s = jnp.dot(kc_ref[0, g], q_t, preferred_element_type=F32) + cbias
        m = jnp.max(s, axis=0, keepdims=True)
        e = jnp.exp(s - m) * ckeep
        den = jnp.sum(e, axis=0, keepdims=True)
        p = e / jnp.where(den > 0, den, 1.0)
        o_cmps.append(lax.dot_general(vc_ref[0, g], p.astype(BF16), (((0,), (0,)), ((), ())),
                                      preferred_element_type=F32))

        psum = p[:, 0:tq]
        for r in range(1, B_QPG):
            psum = psum + p[:, r * tq:(r + 1) * tq]
        imp = jnp.dot(ovl_ref[...], psum, precision=lax.Precision.HIGHEST,
                      preferred_element_type=F32)
        imp = imp[0:nsel]
        score = jnp.where(forced, FORCE_SCORE, jnp.where(valid, imp, -FORCE_SCORE))
        for j in range(nsel):
            sj = score[j:j + 1, :]
            ahead = (score > sj) | ((score == sj) & (blk < j))
            cnt = jnp.sum(ahead.astype(F32), axis=0, keepdims=True)
            sel_j = jnp.where(cnt < n_top, valid_row[j:j + 1, :], 0.0)
            sel_ref[g, j] = jnp.broadcast_to(sel_j, (SUBLANE, tq))

    blocks_per_tile = tk // B_SEL_LEN
    rep = B_SEL_LEN // SUBLANE
    k_off = lax.broadcasted_iota(jnp.int32, (tk, tq), 0)
    t_pos = t0 + lax.broadcasted_iota(jnp.int32, (tk, tq), 1)

    def sel_bias(g, kt, ks):
        picked = jnp.concatenate(
            [sel_ref[g, kt * blocks_per_tile + i] for i in range(blocks_per_tile)
             for _ in range(rep)], axis=0)
        ok = (picked > 0.5) & (ks + k_off <= t_pos)
        return lanes4(jnp.where(ok, 0.0, NEG))

    kt_hi = (t0 + tq + tk - 1) // tk
    o_sels = _flash_t(q_ts, ks_ref, vs_ref, acc_ref, m_ref, l_ref, 0, kt_hi, sel_bias, tk)

    def win_bias(g, kt, ks):
        diff = t_pos - (ks + k_off)
        return lanes4(jnp.where((diff >= 0) & (diff <= B_WINDOW - 1), 0.0, NEG))

    w_lo = jnp.maximum(t0 - (B_WINDOW - 1), 0) // tk
    o_wins = _flash_t(q_ts, kw_ref, vw_ref, acc_ref, m_ref, l_ref, w_lo, kt_hi, win_bias, tk)

    for g in range(B_GROUPS):
        gate = _sigmoid(gp_ref[0, g] + bg_ref[g])
        outs = []
        for r in range(B_QPG):
            sl = slice(r * tq, (r + 1) * tq)
            outs.append(gate[r:r + 1, :] * o_cmps[g][:, sl]
                        + gate[B_QPG + r:B_QPG + r + 1, :] * o_sels[g][:, sl]
                        + gate[2 * B_QPG + r:2 * B_QPG + r + 1, :] * o_wins[g][:, sl])
        og_ref[:, g * gw:(g + 1) * gw] = jnp.concatenate(outs, axis=0).T.astype(og_ref.dtype)

    o_ref[0] = h_ref[0] + jnp.dot(og_ref[...], wout_ref[...], preferred_element_type=F32)


def _nsa_attention(q, kc, vc, k_sel, vt_sel, k_win, vt_win, gpre_t, bg, w_out, h):
    bsz, t_len, _ = q.shape
    d = h.shape[-1]
    tq, tk = B_TQ, B_TK
    ncp = kc.shape[2]
    nsel = t_len // B_SEL_LEN
    gw = B_QPG * B_DH
    c_start = jnp.arange(ncp)[None, :] * B_CMP_STRIDE
    s_start = jnp.arange(LANE)[:, None] * B_SEL_LEN
    ovl = ((c_start < s_start + B_SEL_LEN) & (c_start + B_CMP_LEN > s_start)
           & (jnp.arange(LANE)[:, None] < nsel) & (jnp.arange(ncp)[None, :] < ncp - 1)).astype(F32)

    def k_spec(n_rows):
        return pl.BlockSpec((1, B_GROUPS, n_rows, B_DH), lambda b, i: (b, 0, 0, 0))

    vt_spec = pl.BlockSpec((1, B_GROUPS, B_DH, t_len), lambda b, i: (b, 0, 0, 0))
    n_q = B_QPG * tq

    return pl.pallas_call(
        functools.partial(_nsa_body, tq=tq, tk=tk, t_len=t_len),
        out_shape=jax.ShapeDtypeStruct((bsz, t_len, d), F32),
        grid=(bsz, t_len // tq),
        in_specs=[
            pl.BlockSpec((1, tq, B_GROUPS * gw), lambda b, i: (b, i, 0)),
            k_spec(ncp), k_spec(ncp), k_spec(t_len), vt_spec, k_spec(t_len), vt_spec,
            pl.BlockSpec((1, B_GROUPS, 16, tq), lambda b, i: (b, 0, 0, i)),
            _const_spec((B_GROUPS, 16, tq)),
            _const_spec((LANE, ncp)),
            _const_spec((B_GROUPS * gw, d)),
            pl.BlockSpec((1, tq, d), lambda b, i: (b, i, 0)),
        ],
        out_specs=pl.BlockSpec((1, tq, d), lambda b, i: (b, i, 0)),
        scratch_shapes=[pltpu.VMEM((B_GROUPS, nsel, SUBLANE, tq), F32),
                        pltpu.VMEM((B_GROUPS, B_DH, n_q), F32),
                        pltpu.VMEM((B_GROUPS, 1, n_q), F32),
                        pltpu.VMEM((B_GROUPS, 1, n_q), F32),
                        pltpu.VMEM((tq, B_GROUPS * gw), BF16)],
        compiler_params=_cparams(("parallel", "parallel")),
        name="nsa_attention",
    )(q, kc, vc, k_sel, vt_sel, k_win, vt_win, gpre_t, bg, ovl, w_out, h)


def _nsa_in_body(x_ref, g_ref, w_ref, q_ref, kvc_ref, ks_ref, vs_ref, kw_ref, vw_ref, gp_ref):
    nq = B_HEADS * B_DH
    gw = B_GROUPS * B_DH
    xn = _rmsnorm_rows(x_ref[0], g_ref[...]).astype(BF16)
    q_ref[0] = jnp.dot(xn, w_ref[:, 0:nq], preferred_element_type=F32).astype(q_ref.dtype)

    def seg(i):
        return jnp.dot(xn, w_ref[:, nq + i * gw:nq + (i + 1) * gw], preferred_element_type=F32)

    def put_rows(ref, lead, r):
        for gi in range(B_GROUPS):
            ref[lead + (gi,)] = r[:, gi * B_DH:(gi + 1) * B_DH].astype(ref.dtype)

    def put_cols(ref, r, width):
        r_t = r.T
        for gi in range(B_GROUPS):
            ref[0, gi] = r_t[gi * width:(gi + 1) * width, :].astype(ref.dtype)

    put_rows(kvc_ref, (0, 0), seg(0))
    put_rows(kvc_ref, (1, 0), seg(1))
    put_rows(ks_ref, (0,), seg(2))
    put_cols(vs_ref, seg(3), B_DH)
    put_rows(kw_ref, (0,), seg(4))
    put_cols(vw_ref, seg(5), B_DH)
    gates = jnp.dot(xn, w_ref[:, nq + 6 * gw:], preferred_element_type=F32)
    put_cols(gp_ref, gates, 16)


def _nsa_in(h, g, w):
    bsz, t_len, d = h.shape
    tm = 512
    nq = B_HEADS * B_DH
    G = B_GROUPS
    sds = jax.ShapeDtypeStruct
    k_shape = sds((bsz, G, t_len, B_DH), BF16)
    vt_shape = sds((bsz, G, B_DH, t_len), BF16)
    k_spec = pl.BlockSpec((1, G, tm, B_DH), lambda b, i: (b, 0, i, 0))
    vt_spec = pl.BlockSpec((1, G, B_DH, tm), lambda b, i: (b, 0, 0, i))
    return pl.pallas_call(
        _nsa_in_body,
        out_shape=(sds((bsz, t_len, nq), BF16), sds((2, bsz, G, t_len, B_DH), F32),
                   k_shape, vt_shape, k_shape, vt_shape, sds((bsz, G, 16, t_len), F32)),
        grid=(bsz, t_len // tm),
        in_specs=[pl.BlockSpec((1, tm, d), lambda b, i: (b, i, 0)),
                  _const_spec((1, d)), _const_spec(w.shape)],
        out_specs=(pl.BlockSpec((1, tm, nq), lambda b, i: (b, i, 0)),
                   pl.BlockSpec((2, 1, G, tm, B_DH), lambda b, i: (0, b, 0, i, 0)),
                   k_spec, vt_spec, k_spec, vt_spec,
                   pl.BlockSpec((1, G, 16, tm), lambda b, i: (b, 0, 0, i))),
        compiler_params=_cparams(("parallel", "parallel")),
        name="nsa_in",
    )(h, g.reshape(1, d), w)


def _nsa_layer(h, g, w_in, b_gate, pe_k, pe_v, phik_w1, phik_b1, phik_w2,
               phiv_w1, phiv_b1, phiv_w2, w_out):
    bsz, t_len, d = h.shape
    nq = B_HEADS * B_DH
    nkv = 6 * B_GROUPS * B_DH
    wg = w_in[:, nq + nkv:].reshape(d, B_GROUPS, B_QPG, 3).transpose(0, 1, 3, 2)
    wg = jnp.pad(wg.reshape(d, B_GROUPS, 3 * B_QPG), ((0, 0), (0, 0), (0, 16 - 3 * B_QPG)))
    wg = jnp.pad(wg.reshape(d, B_GROUPS * 16), ((0, 0), (0, LANE - B_GROUPS * 16)))
    bg = b_gate.reshape(B_GROUPS, B_QPG, 3).transpose(0, 2, 1).reshape(B_GROUPS, 3 * B_QPG)
    bg = jnp.pad(bg, ((0, 0), (0, 16 - 3 * B_QPG)))
    bg = jnp.broadcast_to(bg[:, :, None], (B_GROUPS, 16, B_TQ))
    w_b = jnp.concatenate([w_in[:, :nq] * (B_DH ** -0.5), w_in[:, nq:nq + nkv], wg],
                          axis=1).astype(BF16)
    q, kv_cmp, k_sel, vt_sel, k_win, vt_win, gpre_t = _nsa_in(h, g, w_b)

    cmp = _nsa_compress(
        kv_cmp, jnp.stack([pe_k, pe_v]), jnp.stack([phik_w1, phiv_w1]).astype(BF16),
        jnp.stack([phik_b1, phiv_b1]), jnp.stack([phik_w2, phiv_w2]).astype(BF16))
    return _nsa_attention(q, cmp[0], cmp[1], k_sel, vt_sel, k_win, vt_win, gpre_t, bg,
                          w_out.astype(BF16), h)


def _dil_body(*refs, t_len):
    qkv = refs[:9]
    o_ref = refs[9]
    og_ref, lg_ref = refs[10], refs[11]
    qb = C_QBLK
    assert qb == LANE and 2 * C_DH == LANE
    top = lax.broadcasted_iota(jnp.int32, (LANE, qb), 0) < C_DH

    n_blocks = t_len // qb
    per_iter = C_BLOCKS_PER_ITER
    assert n_blocks % per_iter == 0

    def body(it, carry):
        blocks = []
        for gi, (window, dil) in enumerate(C_PATTERNS):
            sub = t_len // dil
            nb = sub // qb
            nk = qb if nb == 1 else 2 * qb
            for j in range(per_iter):
                i = it * per_iter + j
                rc = i // nb
                bq = i % nb
                bk = jnp.maximum(bq - 1, 0)
                q0 = pl.multiple_of(rc * sub + bq * qb, qb)
                k0 = pl.multiple_of(rc * sub + bk * qb, qb)
                if dil == 1:
                    rows = pl.ds(q0, qb)
                else:
                    rows = pl.ds(rc + bq * (qb * dil), qb, stride=dil)
                blocks.append(dict(gi=gi, nk=nk, k0=k0, rows=rows, shift=(bq - bk) * qb,
                                   max_dist=window // dil,
                                   q2=qkv[3 * gi][0, 0, pl.ds(q0, qb), :]))
        for b in blocks:
            q_t = b["q2"].astype(F32).T
            b["q_bd"] = jnp.concatenate([jnp.where(top, q_t, 0.0), jnp.where(top, 0.0, q_t)],
                                        axis=1).astype(BF16)
        for b in blocks:
            k2 = qkv[3 * b["gi"] + 1][0, 0, pl.ds(b["k0"], b["nk"]), :]
            b["s"] = jnp.dot(k2, b["q_bd"], preferred_element_type=F32)
        for b in blocks:
            nk = b["nk"]
            diff = (b["shift"] + lax.broadcasted_iota(jnp.int32, (nk, qb), 1)
                    - lax.broadcasted_iota(jnp.int32, (nk, qb), 0))
            bias = jnp.where((diff >= 0) & (diff <= b["max_dist"]), 0.0, NEG)
            s = b["s"] + jnp.concatenate([bias, bias], axis=1)
            m = jnp.max(s, axis=0, keepdims=True)
            e = jnp.exp(s - m)
            den = jnp.sum(e, axis=0, keepdims=True)
            b["e"] = e.astype(BF16)
            b["inv"] = 1.0 / den
            b["lse"] = m + jnp.log(den)
        for b in blocks:
            v2 = qkv[3 * b["gi"] + 2][0, 0, pl.ds(b["k0"], b["nk"]), :]
            b["pv"] = lax.dot_general(v2, b["e"], (((0,), (0,)), ((), ())),
                                      preferred_element_type=F32)
        for b in blocks:
            pv, inv, lse = b["pv"], b["inv"], b["lse"]
            o_t = jnp.concatenate([pv[0:C_DH, 0:qb] * inv[:, 0:qb],
                                   pv[C_DH:, qb:] * inv[:, qb:]], axis=0)
            l_t = jnp.concatenate([jnp.broadcast_to(lse[:, 0:qb], (C_DH, qb)),
                                   jnp.broadcast_to(lse[:, qb:], (C_DH, qb))], axis=0)
            og_ref[b["gi"], b["rows"], :] = o_t.T
            lg_ref[b["gi"], b["rows"], :] = l_t.T
        return carry

    lax.fori_loop(0, n_blocks // per_iter, body, 0)

    lse = lg_ref[...]
    mx = jnp.max(lse, axis=0, keepdims=True)
    w = jnp.exp(lse - mx)
    alpha = w / jnp.sum(w, axis=0, keepdims=True)
    o_ref[0, 0] = jnp.sum(alpha * og_ref[...], axis=0).astype(o_ref.dtype)


def _dilated_attention(projs):
    bsz, _, t_len, _ = projs[0].shape
    npairs = C_HEADS * C_DH // LANE
    ng = len(C_PATTERNS)

    def spec(col):
        return pl.BlockSpec((1, 1, t_len, LANE), lambda b, hp, col=col: (b, col * npairs + hp, 0, 0))

    return pl.pallas_call(
        functools.partial(_dil_body, t_len=t_len),
        out_shape=jax.ShapeDtypeStruct((bsz, npairs, t_len, LANE), BF16),
        grid=(bsz, npairs),
        in_specs=[spec(c) for _ in range(ng) for c in range(3)],
        out_specs=pl.BlockSpec((1, 1, t_len, LANE), lambda b, hp: (b, hp, 0, 0)),
        scratch_shapes=[pltpu.VMEM((ng, t_len, LANE), F32), pltpu.VMEM((ng, t_len, LANE), F32)],
        compiler_params=_cparams(("parallel", "parallel")),
        name="dilated_attention",
    )(*[projs[gi] for gi in range(ng) for _ in range(3)])


def _dilated_layer(h, g, w_in, w_out):
    bsz, t_len, d = h.shape
    gw = C_HEADS * C_DH
    scale = jnp.ones((3, 1), F32).at[0, :].set(C_DH ** -0.5)
    w_c = w_in.reshape(d, len(C_PATTERNS), 3, gw)
    projs = []
    for gi, (_, dil) in enumerate(C_PATTERNS):
        w_g = (w_c[:, gi] * scale).reshape(d, 3 * gw).astype(BF16)
        p = _norm_matmul(h, g, w_g, out_dtype=BF16, tm=512, tn=3 * gw // 2, slab=True, dil=dil,
                         name=f"dilated_in{gi}")
        projs.append(p.reshape(bsz, 3 * gw // LANE, t_len, LANE))
    o = _dilated_attention(projs)
    return _matmul_residual(o, w_out.astype(BF16), h, tm=512, slab=True, name="dilated_out")


def kernel(x, norm_mix, norm_ffn, norm_final, a_w_in, a_b_gate, a_conv_w, a_conv_b, a_wq, a_wk, a_wv, a_gn, a_skip, a_w_out, b_w_in, b_b_gate, b_pe_k, b_pe_v, b_phik_w1, b_phik_b1, b_phik_w2, b_phiv_w1, b_phiv_b1, b_phiv_w2, b_w_out, c_w_in, c_w_out, f_w_up, f_conv_w, f_conv_b, f_w_down):
    depth = norm_mix.shape[0]
    h = x
    for i in range(depth):
        kind, j = i % 3, i // 3
        if kind == 0:
            h = _mlstm_layer(h, norm_mix[i], a_w_in[j], a_b_gate[j], a_conv_w[j], a_conv_b[j],
                             a_wq[j], a_wk[j], a_wv[j], a_gn[j], a_skip[j], a_w_out[j])
        elif kind == 1:
            h = _nsa_layer(h, norm_mix[i], b_w_in[j], b_b_gate[j], b_pe_k[j], b_pe_v[j],
                           b_phik_w1[j], b_phik_b1[j], b_phik_w2[j],
                           b_phiv_w1[j], b_phiv_b1[j], b_phiv_w2[j], b_w_out[j])
        else:
            h = _dilated_layer(h, norm_mix[i], c_w_in[j], c_w_out[j])
        h = _conv_ffn(h, norm_ffn[i], f_w_up[i].astype(BF16), f_conv_w[i], f_conv_b[i],
                      f_w_down[i].astype(BF16), norm_final, tm=512,
                      final_norm=(i == depth - 1), name="conv_ffn")
    return h
```

```python
import functools

import jax
import jax.numpy as jnp
from jax import lax
from jax.experimental import pallas as pl
from jax.experimental.pallas import tpu as pltpu

F32 = jnp.float32
BF16 = jnp.bfloat16

NORM_EPS = 1e-6
NEG = -1e30
LOG2E = 1.4426950408889634
LANE = 128
SUBLANE = 8
VMEM_LIMIT = 56 * 1024 * 1024
NMM_COLS = 512

A_HEADS = 4
A_CHUNK = 256
A_CONV = 4
B_HEADS = 16
B_DH = 64
B_GROUPS = 4
B_QPG = 4
B_CMP_LEN = 32
B_CMP_STRIDE = 16
B_SEL_LEN = 64
B_TOPN = 16
B_N_LOCAL = 2
B_WINDOW = 512
FORCE_SCORE = 1e9
B_TQ = 128
B_TK = 256
C_PATTERNS = ((128, 1), (512, 4), (2048, 16))
C_HEADS = 16
C_DH = 64
C_QBLK = 128
C_BLOCKS_PER_ITER = 4
F_CHUNK = 256


def _cparams(sem):
    return pltpu.CompilerParams(dimension_semantics=sem, vmem_limit_bytes=VMEM_LIMIT)


def _const_spec(shape):
    nd = len(shape)
    return pl.BlockSpec(shape, lambda *_: (0,) * nd, pipeline_mode=pl.Buffered(1))


def _rmsnorm_rows(x, g):
    ms = jnp.mean(x * x, axis=-1, keepdims=True)
    return x * lax.rsqrt(ms + NORM_EPS) * g


def _sigmoid(x):
    return 1.0 / (1.0 + jnp.exp(-x))


def _nmm_body(x_ref, g_ref, w_ref, o_ref, xn_ref, *rest, slab, dil):
    @pl.when(pl.program_id(2) == 0)
    def _():
        xn_ref[...] = _rmsnorm_rows(x_ref[0], g_ref[...]).astype(BF16)

    if not slab:
        o_ref[0] = jnp.dot(xn_ref[...], w_ref[...],
                           preferred_element_type=F32).astype(o_ref.dtype)
        return
    tm, tn = xn_ref.shape[0], w_ref.shape[1]
    step = min(tn, NMM_COLS)
    for c0 in range(0, tn, step):
        r = jnp.dot(xn_ref[...], w_ref[:, c0:c0 + step], preferred_element_type=F32)
        for cl in range(step // LANE):
            c = c0 // LANE + cl
            piece = r[:, cl * LANE:(cl + 1) * LANE]
            if dil == 1:
                o_ref[0, c, 0] = piece.astype(o_ref.dtype)
            else:
                r_ref, = rest
                r_ref[c] = piece
                for rc in range(dil):
                    o_ref[0, c, rc] = r_ref[c, pl.ds(rc, tm // dil, stride=dil),
                                            :].astype(o_ref.dtype)


def _norm_matmul(h, g, w, *, out_dtype, tm, tn, slab=False, dil=1, name):
    bsz, t_len, d = h.shape
    n = w.shape[1]
    assert t_len % tm == 0 and n % tn == 0 and tn % LANE == 0 and tm % (2 * SUBLANE * dil) == 0
    scratch = [pltpu.VMEM((tm, d), BF16)]
    if slab:
        out_shape = jax.ShapeDtypeStruct((bsz, n // LANE, dil, t_len // dil, LANE), out_dtype)
        out_spec = pl.BlockSpec((1, tn // LANE, dil, tm // dil, LANE),
                                lambda b, i, j: (b, j, 0, i, 0))
        if dil > 1:
            scratch.append(pltpu.VMEM((tn // LANE, tm, LANE), F32))
    else:
        out_shape = jax.ShapeDtypeStruct((bsz, t_len, n), out_dtype)
        out_spec = pl.BlockSpec((1, tm, tn), lambda b, i, j: (b, i, j))
    return pl.pallas_call(
        functools.partial(_nmm_body, slab=slab, dil=dil),
        out_shape=out_shape,
        grid=(bsz, t_len // tm, n // tn),
        in_specs=[
            pl.BlockSpec((1, tm, d), lambda b, i, j: (b, i, 0)),
            pl.BlockSpec((1, d), lambda b, i, j: (0, 0)),
            _const_spec((d, n)) if tn == n else pl.BlockSpec((d, tn), lambda b, i, j: (0, j)),
        ],
        out_specs=out_spec,
        scratch_shapes=scratch,
        compiler_params=_cparams(("parallel", "parallel", "arbitrary")),
        name=name,
    )(h, g.reshape(1, d), w)


def _mmr_body(x_ref, w_ref, r_ref, o_ref, *, slab):
    if slab:
        x = jnp.concatenate([x_ref[0, c] for c in range(x_ref.shape[1])], axis=-1)
    else:
        x = x_ref[0]
    o_ref[0] = r_ref[0] + jnp.dot(x, w_ref[...], preferred_element_type=F32)


def _matmul_residual(x, w, res, *, tm, slab=False, name):
    bsz, t_len, n = res.shape
    k = w.shape[0]
    if slab:
        x_spec = pl.BlockSpec((1, k // LANE, tm, LANE), lambda b, i: (b, 0, i, 0))
    else:
        x_spec = pl.BlockSpec((1, tm, k), lambda b, i: (b, i, 0))
    return pl.pallas_call(
        functools.partial(_mmr_body, slab=slab),
        out_shape=jax.ShapeDtypeStruct((bsz, t_len, n), F32),
        grid=(bsz, t_len // tm),
        in_specs=[x_spec, _const_spec((k, n)),
                  pl.BlockSpec((1, tm, n), lambda b, i: (b, i, 0))],
        out_specs=pl.BlockSpec((1, tm, n), lambda b, i: (b, i, 0)),
        compiler_params=_cparams(("parallel", "parallel")),
        name=name,
    )(x, w, res)


def _ffn_body(h_ref, hp_ref, g_ref, wup_ref, cw_ref, cb_ref, wdn_ref, gf_ref, o_ref,
              xn_ref, acc_ref, *, tm, f_hidden, final_norm):
    halo = SUBLANE
    g = g_ref[...]
    keep = (pl.program_id(1) > 0).astype(F32)
    xn_ref[0:halo, :] = (_rmsnorm_rows(hp_ref[0], g) * keep).astype(BF16)
    xn_ref[halo:, :] = _rmsnorm_rows(h_ref[0], g).astype(BF16)
    xn = xn_ref[...]

    def conv(u, col0):
        w = cw_ref[:, col0:col0 + F_CHUNK]
        b = cb_ref[:, col0:col0 + F_CHUNK]
        return (w[2:3] * u[halo:halo + tm] + w[1:2] * u[halo - 1:halo - 1 + tm]
                + w[0:1] * u[halo - 2:halo - 2 + tm] + b)

    def up(c):
        c0 = c * F_CHUNK
        return (jnp.dot(xn, wup_ref[:, c0:c0 + F_CHUNK], preferred_element_type=F32),
                jnp.dot(xn, wup_ref[:, f_hidden + c0:f_hidden + c0 + F_CHUNK],
                        preferred_element_type=F32))

    n_chunks = f_hidden // F_CHUNK
    nxt = up(0)
    for c in range(n_chunks):
        c0 = c * F_CHUNK
        ug, uv = nxt
        if c + 1 < n_chunks:
            nxt = up(c + 1)
        gate = conv(ug, c0)
        val = conv(uv, f_hidden + c0)
        act = (gate * _sigmoid(gate) * val).astype(BF16)
        part = jnp.dot(act, wdn_ref[c0:c0 + F_CHUNK, :], preferred_element_type=F32)
        if c == 0:
            acc_ref[...] = part
        else:
            acc_ref[...] += part

    out = h_ref[0] + acc_ref[...]
    if final_norm:
        out = _rmsnorm_rows(out, gf_ref[...])
    o_ref[0] = out


def _conv_ffn(h, g, w_up, conv_w, conv_b, w_down, g_final, *, tm, final_norm, name):
    bsz, t_len, d = h.shape
    f2 = w_up.shape[1]
    f_hidden = f2 // 2
    assert f_hidden % F_CHUNK == 0 and t_len % tm == 0 and tm % SUBLANE == 0
    rows_per_tile = tm // SUBLANE
    return pl.pallas_call(
        functools.partial(_ffn_body, tm=tm, f_hidden=f_hidden, final_norm=final_norm),
        out_shape=jax.ShapeDtypeStruct((bsz, t_len, d), F32),
        grid=(bsz, t_len // tm),
        in_specs=[
            pl.BlockSpec((1, tm, d), lambda b, i: (b, i, 0)),
            pl.BlockSpec((1, SUBLANE, d),
                         lambda b, i: (b, jnp.maximum(i * rows_per_tile - 1, 0), 0)),
            _const_spec((1, d)),
            _const_spec((d, f2)),
            _const_spec((3, f2)),
            _const_spec((1, f2)),
            _const_spec((f_hidden, d)),
            _const_spec((1, d)),
        ],
        out_specs=pl.BlockSpec((1, tm, d), lambda b, i: (b, i, 0)),
        scratch_shapes=[pltpu.VMEM((tm + SUBLANE, d), BF16), pltpu.VMEM((tm, d), F32)],
        compiler_params=_cparams(("parallel", "parallel")),
        name=name,
    )(h, h, g.reshape(1, d), w_up, conv_w, conv_b.reshape(1, f2), w_down, g_final.reshape(1, d))


def _mlstm_body(h_ref, g_ref, win_ref, bg_ref, cw_ref, cb_ref, wq_ref, wk_ref, wv_ref,
                gn_ref, sk_ref, wout_ref, o_ref, c_ref, n_ref, m_ref, xp_ref, y_ref,
                *, chunk, dh):
    L = chunk
    halo = SUBLANE
    inner = A_HEADS * dh

    @pl.when(pl.program_id(1) == 0)
    def _():
        c_ref[...] = jnp.zeros_like(c_ref)
        n_ref[...] = jnp.zeros_like(n_ref)
        m_ref[...] = jnp.zeros_like(m_ref)
        xp_ref[...] = jnp.zeros_like(xp_ref)

    xn = _rmsnorm_rows(h_ref[0], g_ref[...]).astype(BF16)
    xm = jnp.dot(xn, win_ref[:, 0:inner], preferred_element_type=F32)
    gates = jnp.dot(xn, win_ref[:, 2 * inner:], preferred_element_type=F32) + bg_ref[...]
    heads = [dict(lo=hd * dh, hi=(hd + 1) * dh) for hd in range(A_HEADS)]
    for hd, st in enumerate(heads):
        st["v"] = jnp.dot(xm[:, st["lo"]:st["hi"]].astype(BF16), wv_ref[hd],
                          preferred_element_type=F32)
    o_pre = jnp.dot(xn, win_ref[:, inner:2 * inner], preferred_element_type=F32)

    xe = jnp.concatenate([xp_ref[...], xm], axis=0)
    cw = cw_ref[...]
    xc = (cw[3:4] * xe[halo:halo + L] + cw[2:3] * xe[halo - 1:halo - 1 + L]
          + cw[1:2] * xe[halo - 2:halo - 2 + L] + cw[0:1] * xe[halo - 3:halo - 3 + L]
          + cb_ref[...])
    xc = xc * _sigmoid(xc)
    xp_ref[...] = xm[L - halo:L]

    ii = lax.broadcasted_iota(jnp.int32, (L, L), 0)
    jj = lax.broadcasted_iota(jnp.int32, (L, L), 1)
    eye = ii == jj
    tril = jj <= ii

    def to_row(col):
        return jnp.sum(jnp.where(eye, col, 0.0), axis=0, keepdims=True)

    for hd, st in enumerate(heads):
        st["xc"] = xc[:, st["lo"]:st["hi"]]
        xcb = st["xc"].astype(BF16)
        st["q"] = jnp.dot(xcb, wq_ref[hd], preferred_element_type=F32)
        st["k"] = jnp.dot(xcb, wk_ref[hd], preferred_element_type=F32)

    for hd, st in enumerate(heads):
        li = gates[:, hd:hd + 1]
        fpre = gates[:, A_HEADS + hd:A_HEADS + hd + 1]
        lf = jnp.minimum(fpre, 0.0) - jnp.log1p(jnp.exp(-jnp.abs(fpre)))
        li_row = to_row(li)
        lf_row = to_row(lf)
        b_col = jnp.sum(jnp.where(tril, lf_row, 0.0), axis=1, keepdims=True)
        b_row = to_row(b_col)
        m_st = m_ref[hd][:, 0:1]
        log_d = jnp.where(tril, b_col - b_row + li_row, NEG)
        log_inter = b_col + m_st
        m = jnp.maximum(log_inter, jnp.max(log_d, axis=1, keepdims=True))
        st["m"] = m
        st["w_intra"] = jnp.exp(log_d - m)
        st["w_inter"] = jnp.exp(log_inter - m)
        b_last = b_col[L - 1:L, :]
        log_w = b_last - b_col + li
        m_new = jnp.maximum(b_last + m_st, jnp.max(log_w, axis=0, keepdims=True))
        st["decay"] = jnp.exp(b_last + m_st - m_new)
        st["w_k"] = jnp.exp(log_w - m_new)
        m_ref[hd] = jnp.broadcast_to(m_new, m_ref.shape[1:])

    for hd, st in enumerate(heads):
        st["k"] = st["k"] * (dh ** -0.5)
        st["qb"] = st["q"].astype(BF16)
        st["vb"] = st["v"].astype(BF16)
        st["s"] = lax.dot_general(st["qb"], st["k"].astype(BF16), (((1,), (1,)), ((), ())),
                                  preferred_element_type=F32)
        st["qc"] = jnp.dot(st["qb"], c_ref[hd].astype(BF16), preferred_element_type=F32)

    for hd, st in enumerate(heads):
        s = st["s"] * st["w_intra"]
        st["den"] = (st["w_inter"] * jnp.sum(st["q"] * n_ref[hd], axis=1, keepdims=True)
                     + jnp.sum(s, axis=1, keepdims=True))
        wk = st["w_k"] * st["k"]
        n_ref[hd] = st["decay"] * n_ref[hd] + jnp.sum(wk, axis=0, keepdims=True)
        st["sv"] = jnp.dot(s.astype(BF16), st["vb"], preferred_element_type=F32)
        st["kv"] = lax.dot_general(wk.astype(BF16), st["vb"], (((0,), (0,)), ((), ())),
                                   preferred_element_type=F32)

    for hd, st in enumerate(heads):
        lo, hi = st["lo"], st["hi"]
        c_ref[hd] = st["decay"] * c_ref[hd] + st["kv"]
        num = st["w_inter"] * st["qc"] + st["sv"]
        hh = num / jnp.maximum(jnp.abs(st["den"]), jnp.exp(-st["m"]))
        mu = jnp.mean(hh, axis=-1, keepdims=True)
        ctr = hh - mu
        var = jnp.mean(ctr * ctr, axis=-1, keepdims=True)
        hn = ctr * lax.rsqrt(var + NORM_EPS) * gn_ref[:, lo:hi]
        y = _sigmoid(o_pre[:, lo:hi]) * hn + sk_ref[:, lo:hi] * st["xc"]
        y_ref[:, lo:hi] = y.astype(y_ref.dtype)

    o_ref[0] = h_ref[0] + jnp.dot(y_ref[...], wout_ref[...], preferred_element_type=F32)


def _mlstm_layer(h, g, w_in, b_gate, conv_w, conv_b, wq, wk, wv, gn, skip, w_out):
    bsz, t_len, d = h.shape
    inner = w_out.shape[0]
    dh = inner // A_HEADS
    L = A_CHUNK
    w_a = w_in.astype(BF16)
    bg = b_gate.reshape(1, 2 * A_HEADS)
    return pl.pallas_call(
        functools.partial(_mlstm_body, chunk=L, dh=dh),
        out_shape=jax.ShapeDtypeStruct((bsz, t_len, d), F32),
        grid=(bsz, t_len // L),
        in_specs=[
            pl.BlockSpec((1, L, d), lambda b, c: (b, c, 0)),
            _const_spec((1, d)),
            _const_spec((d, 2 * inner + 2 * A_HEADS)),
            _const_spec((1, 2 * A_HEADS)),
            _const_spec((A_CONV, inner)),
            _const_spec((1, inner)),
            _const_spec((A_HEADS, dh, dh)),
            _const_spec((A_HEADS, dh, dh)),
            _const_spec((A_HEADS, dh, dh)),
            _const_spec((1, inner)),
            _const_spec((1, inner)),
            _const_spec((inner, d)),
        ],
        out_specs=pl.BlockSpec((1, L, d), lambda b, c: (b, c, 0)),
        scratch_shapes=[
            pltpu.VMEM((A_HEADS, dh, dh), F32),
            pltpu.VMEM((A_HEADS, 1, dh), F32),
            pltpu.VMEM((A_HEADS, 1, LANE), F32),
            pltpu.VMEM((SUBLANE, inner), F32),
            pltpu.VMEM((L, inner), BF16),
        ],
        compiler_params=_cparams(("parallel", "arbitrary")),
        name="mlstm_layer",
    )(h, g.reshape(1, d), w_a, bg, conv_w, conv_b.reshape(1, inner), wq.astype(BF16),
      wk.astype(BF16), wv.astype(BF16), gn.reshape(1, inner), skip.reshape(1, inner),
      w_out.astype(BF16))


def _cmp_body(x_ref, pe_ref, w1_ref, b1_ref, w2_ref, o_ref):
    half = x_ref.shape[-1]
    x = x_ref[0, 0, 0]
    pe = pe_ref[0]
    a = (x + pe[:, :half]).astype(BF16)
    b = (x + pe[:, half:]).astype(BF16)
    w1 = w1_ref[0]
    pa = jnp.dot(a, w1[:half], preferred_element_type=F32)
    pb = jnp.dot(b, w1[half:], preferred_element_type=F32)
    pb_next = jnp.concatenate([pb[1:], jnp.zeros((1, pb.shape[1]), F32)], axis=0)
    pre = pa + pb_next + b1_ref[0]
    hid = (pre * _sigmoid(pre)).astype(BF16)
    out = jnp.dot(hid, w2_ref[0], preferred_element_type=F32)
    row = lax.broadcasted_iota(jnp.int32, out.shape, 0)
    o_ref[0, 0, 0] = jnp.where(row < out.shape[0] - 1, out, 0.0).astype(o_ref.dtype)


def _nsa_compress(kv_cmp, pe, w1, b1, w2):
    two, bsz, g, t_len, dh = kv_cmp.shape
    n_half = t_len // B_CMP_STRIDE
    half = B_CMP_STRIDE * dh
    x = kv_cmp.reshape(two, bsz, g, n_half, half)
    hidden = w1.shape[-1]
    return pl.pallas_call(
        _cmp_body,
        out_shape=jax.ShapeDtypeStruct((two, bsz, g, n_half, dh), BF16),
        grid=(two, bsz, g),
        in_specs=[
            pl.BlockSpec((1, 1, 1, n_half, half), lambda s, b, gi: (s, b, gi, 0, 0)),
            pl.BlockSpec((1, 1, 2 * half), lambda s, b, gi: (s, 0, 0)),
            pl.BlockSpec((1, 2 * half, hidden), lambda s, b, gi: (s, 0, 0)),
            pl.BlockSpec((1, 1, hidden), lambda s, b, gi: (s, 0, 0)),
            pl.BlockSpec((1, hidden, dh), lambda s, b, gi: (s, 0, 0)),
        ],
        out_specs=pl.BlockSpec((1, 1, 1, n_half, dh), lambda s, b, gi: (s, b, gi, 0, 0)),
        compiler_params=_cparams(("parallel", "parallel", "parallel")),
        name="nsa_compress",
    )(x, pe.reshape(two, 1, 2 * half), w1, b1.reshape(two, 1, hidden), w2)


def _flash_t(q_ts, k_ref, vt_ref, acc_ref, m_ref, kt_lo, kt_hi, bias_fn, last_bias_fn, tk):
    acc_ref[...] = jnp.zeros_like(acc_ref)
    m_ref[...] = jnp.full_like(m_ref, NEG)
    groups = range(len(q_ts))

    def tile(kt, bias_of):
        ks = pl.multiple_of(kt * tk, tk)
        ss = [jnp.dot(k_ref[0, g, pl.ds(ks, tk), :], q_ts[g], preferred_element_type=F32)
              for g in groups]
        bias = None if bias_of is None else bias_of(kt, ks)
        ps, alphas = [], []
        for g in groups:
            s = ss[g] if bias is None else ss[g] + bias
            m = m_ref[g]
            m_new = jnp.maximum(m, jnp.max(s, axis=0, keepdims=True))
            alphas.append(jnp.exp2(m - m_new))
            ps.append(jnp.exp2(s - m_new).astype(BF16))
            m_ref[g] = m_new
        for g in groups:
            v_t = vt_ref[0, g, :, pl.ds(ks, tk)]
            acc_ref[g] = alphas[g] * acc_ref[g] + jnp.dot(v_t, ps[g],
                                                          preferred_element_type=F32)

    def body(kt, carry):
        tile(kt, bias_fn)
        return carry

    lax.fori_loop(kt_lo, kt_hi - 1, body, 0)
    tile(kt_hi - 1, last_bias_fn)
    dh = acc_ref.shape[1] - 16
    return [acc_ref[g, 0:dh, :] / acc_ref[g, dh:dh + 1, :] for g in groups]


def _nsa_body(q_ref, kc_ref, vc_ref, ks_ref, vs_ref, kw_ref, vw_ref, gp_ref, bg_ref,
              ovl_ref, wout_ref, h_ref, o_ref, acc_ref, m_ref, og_ref, *, tq, tk, t_len):
    qi = pl.program_id(1)
    t0 = qi * tq
    gw = B_QPG * B_DH
    ncp = kc_ref.shape[2]
    nsel = t_len // B_SEL_LEN
    n_top = min(B_TOPN, nsel)

    def lanes4(x):
        return jnp.concatenate([x] * B_QPG, axis=1)

    n_idx = lax.broadcasted_iota(jnp.int32, (ncp, tq), 0)
    t_col = t0 + lax.broadcasted_iota(jnp.int32, (ncp, tq), 1)
    cmask = (n_idx * B_CMP_STRIDE + (B_CMP_LEN - 1) <= t_col) & (n_idx < ncp - 1)
    cbias = lanes4(jnp.where(cmask, 0.0, NEG))
    ckeep = lanes4(cmask.astype(F32))
    blk = lax.broadcasted_iota(jnp.int32, (nsel, tq), 0)
    cur = (t0 + lax.broadcasted_iota(jnp.int32, (nsel, tq), 1)) // B_SEL_LEN
    valid = blk <= cur
    forced = ((blk == 0) | (cur - blk < B_N_LOCAL)) & valid

    q_ts, q_sels, o_cmps = [], [], []
    for g in range(B_GROUPS):
        q_full = q_ref[0, :, g * gw:(g + 1) * gw].astype(F32).T.astype(BF16)
        q_t = jnp.concatenate([q_full[r * B_DH:(r + 1) * B_DH, :] for r in range(B_QPG)], axis=1)
        q_ts.append(q_t)

        s = jnp.dot(kc_ref[0, g], q_t, preferred_element_type=F32) + cbias
        m = jnp.max(s, axis=0, keepdims=True)
        e = jnp.exp2(s - m) * ckeep
        den = jnp.sum(e, axis=0, keepdims=True)
        p = e / jnp.where(den > 0, den, 1.0)
        o_cmps.append(lax.dot_general(vc_ref[0, g], p.astype(BF16), (((0,), (0,)), ((), ())),
                                      preferred_element_type=F32))

        psum = p[:, 0:tq]
        for r in range(1, B_QPG):
            psum = psum + p[:, r * tq:(r + 1) * tq]
        imp = jnp.dot(ovl_ref[...], psum, precision=lax.Precision.HIGHEST,
                      preferred_element_type=F32)
        imp = imp[0:nsel]
        score = jnp.where(forced, FORCE_SCORE, jnp.where(valid, imp, -FORCE_SCORE))
        sel_bias = jnp.full((nsel, tq), NEG, F32)
        for j in range(nsel):
            sj = score[j:j + 1, :]
            ahead = (score > sj) | ((score == sj) & (blk < j))
            cnt = jnp.sum(ahead.astype(F32), axis=0, keepdims=True)
            sel_bias = jnp.where((blk == j) & (cnt < n_top) & valid, 0.0, sel_bias)
        q_sels.append(jnp.concatenate(
            [q_t, lanes4(sel_bias).astype(BF16),
             jnp.zeros((LANE - B_DH - nsel, B_QPG * tq), BF16)], axis=0))

    k_off = lax.broadcasted_iota(jnp.int32, (tk, tq), 0)
    t_pos = t0 + lax.broadcasted_iota(jnp.int32, (tk, tq), 1)

    def causal_bias(kt, ks):
        return lanes4(jnp.where(ks + k_off <= t_pos, 0.0, NEG))

    kt_hi = (t0 + tq + tk - 1) // tk
    o_sels = _flash_t(q_sels, ks_ref, vs_ref, acc_ref, m_ref, 0, kt_hi, None, causal_bias, tk)

    def win_bias(kt, ks):
        diff = t_pos - (ks + k_off)
        return lanes4(jnp.where((diff >= 0) & (diff <= B_WINDOW - 1), 0.0, NEG))

    w_lo = jnp.maximum(t0 - (B_WINDOW - 1), 0) // tk
    o_wins = _flash_t(q_ts, kw_ref, vw_ref, acc_ref, m_ref, w_lo, kt_hi, win_bias, win_bias, tk)

    for g in range(B_GROUPS):
        gate = _sigmoid(gp_ref[0, g] + bg_ref[g])
        outs = []
        for r in range(B_QPG):
            sl = slice(r * tq, (r + 1) * tq)
            outs.append(gate[r:r + 1, :] * o_cmps[g][:, sl]
                        + gate[B_QPG + r:B_QPG + r + 1, :] * o_sels[g][:, sl]
                        + gate[2 * B_QPG + r:2 * B_QPG + r + 1, :] * o_wins[g][:, sl])
        og_ref[:, g * gw:(g + 1) * gw] = jnp.concatenate(outs, axis=0).T.astype(og_ref.dtype)

    o_ref[0] = h_ref[0] + jnp.dot(og_ref[...], wout_ref[...], preferred_element_type=F32)


def _nsa_attention(q, kc, vc, k_sel, vt_sel, k_win, vt_win, gpre_t, bg, w_out, h):
    bsz, t_len, _ = q.shape
    d = h.shape[-1]
    tq, tk = B_TQ, B_TK
    ncp = kc.shape[2]
    nsel = t_len // B_SEL_LEN
    gw = B_QPG * B_DH
    c_start = jnp.arange(ncp)[None, :] * B_CMP_STRIDE
    s_start = jnp.arange(LANE)[:, None] * B_SEL_LEN
    ovl = ((c_start < s_start + B_SEL_LEN) & (c_start + B_CMP_LEN > s_start)
           & (jnp.arange(LANE)[:, None] < nsel) & (jnp.arange(ncp)[None, :] < ncp - 1)).astype(F32)

    def k_spec(n_rows, width=B_DH):
        return pl.BlockSpec((1, B_GROUPS, n_rows, width), lambda b, i: (b, 0, 0, 0))

    vt_spec = pl.BlockSpec((1, B_GROUPS, B_DH + 16, t_len), lambda b, i: (b, 0, 0, 0))
    n_q = B_QPG * tq

    return pl.pallas_call(
        functools.partial(_nsa_body, tq=tq, tk=tk, t_len=t_len),
        out_shape=jax.ShapeDtypeStruct((bsz, t_len, d), F32),
        grid=(bsz, t_len // tq),
        in_specs=[
            pl.BlockSpec((1, tq, B_GROUPS * gw), lambda b, i: (b, i, 0)),
            k_spec(ncp), k_spec(ncp), k_spec(t_len, LANE), vt_spec, k_spec(t_len), vt_spec,
            pl.BlockSpec((1, B_GROUPS, 16, tq), lambda b, i: (b, 0, 0, i)),
            _const_spec((B_GROUPS, 16, tq)),
            _const_spec((LANE, ncp)),
            _const_spec((B_GROUPS * gw, d)),
            pl.BlockSpec((1, tq, d), lambda b, i: (b, i, 0)),
        ],
        out_specs=pl.BlockSpec((1, tq, d), lambda b, i: (b, i, 0)),
        scratch_shapes=[pltpu.VMEM((B_GROUPS, B_DH + 16, n_q), F32),
                        pltpu.VMEM((B_GROUPS, 1, n_q), F32),
                        pltpu.VMEM((tq, B_GROUPS * gw), BF16)],
        compiler_params=_cparams(("parallel", "parallel")),
        name="nsa_attention",
    )(q, kc, vc, k_sel, vt_sel, k_win, vt_win, gpre_t, bg, ovl, w_out, h)


def _nsa_in_body(x_ref, g_ref, w_ref, q_ref, kvc_ref, ks_ref, vs_ref, kw_ref, vw_ref, gp_ref):
    nq = B_HEADS * B_DH
    gw = B_GROUPS * B_DH
    xn = _rmsnorm_rows(x_ref[0], g_ref[...]).astype(BF16)
    q_ref[0] = jnp.dot(xn, w_ref[:, 0:nq], preferred_element_type=F32).astype(q_ref.dtype)

    def seg(i):
        return jnp.dot(xn, w_ref[:, nq + i * gw:nq + (i + 1) * gw], preferred_element_type=F32)

    def put_rows(ref, lead, r):
        for gi in range(B_GROUPS):
            ref[lead + (gi,)] = r[:, gi * B_DH:(gi + 1) * B_DH].astype(ref.dtype)

    def put_cols(ref, r, width, ones_rows=0):
        r_t = r.T
        tm = r.shape[0]
        for gi in range(B_GROUPS):
            ref[0, gi, 0:width, :] = r_t[gi * width:(gi + 1) * width, :].astype(ref.dtype)
            if ones_rows:
                ref[0, gi, width:width + ones_rows, :] = jnp.ones((ones_rows, tm), ref.dtype)

    put_rows(kvc_ref, (0, 0), seg(0))
    put_rows(kvc_ref, (1, 0), seg(1))
    r = seg(2)
    tm = r.shape[0]
    blk = (pl.program_id(1) * tm + lax.broadcasted_iota(jnp.int32, (tm, LANE - B_DH), 0)) // B_SEL_LEN
    onehot = (blk == lax.broadcasted_iota(jnp.int32, (tm, LANE - B_DH), 1)).astype(F32)
    for gi in range(B_GROUPS):
        ks_ref[0, gi] = jnp.concatenate([r[:, gi * B_DH:(gi + 1) * B_DH], onehot],
                                        axis=1).astype(ks_ref.dtype)
    put_cols(vs_ref, seg(3), B_DH, ones_rows=16)
    put_rows(kw_ref, (0,), seg(4))
    put_cols(vw_ref, seg(5), B_DH, ones_rows=16)
    gates = jnp.dot(xn, w_ref[:, nq + 6 * gw:], preferred_element_type=F32)
    put_cols(gp_ref, gates, 16)


def _nsa_in(h, g, w):
    bsz, t_len, d = h.shape
    tm = 512
    nq = B_HEADS * B_DH
    G = B_GROUPS
    sds = jax.ShapeDtypeStruct
    k_shape = sds((bsz, G, t_len, B_DH), BF16)
    vt_shape = sds((bsz, G, B_DH + 16, t_len), BF16)
    k_spec = pl.BlockSpec((1, G, tm, B_DH), lambda b, i: (b, 0, i, 0))
    vt_spec = pl.BlockSpec((1, G, B_DH + 16, tm), lambda b, i: (b, 0, 0, i))
    return pl.pallas_call(
        _nsa_in_body,
        out_shape=(sds((bsz, t_len, nq), BF16), sds((2, bsz, G, t_len, B_DH), F32),
                   sds((bsz, G, t_len, LANE), BF16), vt_shape, k_shape, vt_shape,
                   sds((bsz, G, 16, t_len), F32)),
        grid=(bsz, t_len // tm),
        in_specs=[pl.BlockSpec((1, tm, d), lambda b, i: (b, i, 0)),
                  _const_spec((1, d)), _const_spec(w.shape)],
        out_specs=(pl.BlockSpec((1, tm, nq), lambda b, i: (b, i, 0)),
                   pl.BlockSpec((2, 1, G, tm, B_DH), lambda b, i: (0, b, 0, i, 0)),
                   pl.BlockSpec((1, G, tm, LANE), lambda b, i: (b, 0, i, 0)), vt_spec,
                   k_spec, vt_spec,
                   pl.BlockSpec((1, G, 16, tm), lambda b, i: (b, 0, 0, i))),
        compiler_params=_cparams(("parallel", "parallel")),
        name="nsa_in",
    )(h, g.reshape(1, d), w)


def _nsa_layer(h, g, w_in, b_gate, pe_k, pe_v, phik_w1, phik_b1, phik_w2,
               phiv_w1, phiv_b1, phiv_w2, w_out):
    bsz, t_len, d = h.shape
    nq = B_HEADS * B_DH
    nkv = 6 * B_GROUPS * B_DH
    wg = w_in[:, nq + nkv:].reshape(d, B_GROUPS, B_QPG, 3).transpose(0, 1, 3, 2)
    wg = jnp.pad(wg.reshape(d, B_GROUPS, 3 * B_QPG), ((0, 0), (0, 0), (0, 16 - 3 * B_QPG)))
    wg = jnp.pad(wg.reshape(d, B_GROUPS * 16), ((0, 0), (0, LANE - B_GROUPS * 16)))
    bg = b_gate.reshape(B_GROUPS, B_QPG, 3).transpose(0, 2, 1).reshape(B_GROUPS, 3 * B_QPG)
    bg = jnp.pad(bg, ((0, 0), (0, 16 - 3 * B_QPG)))
    bg = jnp.broadcast_to(bg[:, :, None], (B_GROUPS, 16, B_TQ))
    w_b = jnp.concatenate([w_in[:, :nq] * (B_DH ** -0.5 * LOG2E), w_in[:, nq:nq + nkv], wg],
                          axis=1).astype(BF16)
    q, kv_cmp, k_sel, vt_sel, k_win, vt_win, gpre_t = _nsa_in(h, g, w_b)

    cmp = _nsa_compress(
        kv_cmp, jnp.stack([pe_k, pe_v]), jnp.stack([phik_w1, phiv_w1]).astype(BF16),
        jnp.stack([phik_b1, phiv_b1]), jnp.stack([phik_w2, phiv_w2]).astype(BF16))
    return _nsa_attention(q, cmp[0], cmp[1], k_sel, vt_sel, k_win, vt_win, gpre_t, bg,
                          w_out.astype(BF16), h)


def _dil_body(*refs, t_len):
    qkv = refs[:9]
    o_ref = refs[9]
    og_ref, lg_ref = refs[10], refs[11]
    qb = C_QBLK
    assert qb == LANE and 2 * C_DH == LANE
    top = lax.broadcasted_iota(jnp.int32, (LANE, qb), 0) < C_DH

    n_blocks = t_len // qb
    per_iter = C_BLOCKS_PER_ITER
    assert n_blocks % per_iter == 0

    def body(it, carry):
        blocks = []
        for gi, (window, dil) in enumerate(C_PATTERNS):
            sub = t_len // dil
            nb = sub // qb
            nk = qb if nb == 1 else 2 * qb
            for j in range(per_iter):
                i = it * per_iter + j
                rc = i // nb
                bq = i % nb
                bk = jnp.maximum(bq - 1, 0)
                q0 = pl.multiple_of(rc * sub + bq * qb, qb)
                k0 = pl.multiple_of(rc * sub + bk * qb, qb)
                if dil == 1:
                    rows = pl.ds(q0, qb)
                else:
                    rows = pl.ds(rc + bq * (qb * dil), qb, stride=dil)
                blocks.append(dict(gi=gi, nk=nk, k0=k0, rows=rows, shift=(bq - bk) * qb,
                                   max_dist=window // dil,
                                   q2=qkv[3 * gi][0, 0, pl.ds(q0, qb), :]))
        for b in blocks:
            q_t = b["q2"].astype(F32).T
            b["q_bd"] = jnp.concatenate([jnp.where(top, q_t, 0.0), jnp.where(top, 0.0, q_t)],
                                        axis=1).astype(BF16)
        for b in blocks:
            k2 = qkv[3 * b["gi"] + 1][0, 0, pl.ds(b["k0"], b["nk"]), :]
            b["s"] = jnp.dot(k2, b["q_bd"], preferred_element_type=F32)
        for b in blocks:
            nk = b["nk"]
            diff = (b["shift"] + lax.broadcasted_iota(jnp.int32, (nk, qb), 1)
                    - lax.broadcasted_iota(jnp.int32, (nk, qb), 0))
            bias = jnp.where((diff >= 0) & (diff <= b["max_dist"]), 0.0, NEG)
            s = b["s"] + jnp.concatenate([bias, bias], axis=1)
            m = jnp.max(s, axis=0, keepdims=True)
            e = jnp.exp2(s - m)
            den = jnp.sum(e, axis=0, keepdims=True)
            b["e"] = e.astype(BF16)
            b["inv"] = 1.0 / den
            b["lse"] = m + jnp.log2(den)
        for b in blocks:
            v2 = qkv[3 * b["gi"] + 2][0, 0, pl.ds(b["k0"], b["nk"]), :]
            b["pv"] = lax.dot_general(v2, b["e"], (((0,), (0,)), ((), ())),
                                      preferred_element_type=F32)
        for b in blocks:
            pv, inv, lse = b["pv"], b["inv"], b["lse"]
            o_t = jnp.concatenate([pv[0:C_DH, 0:qb] * inv[:, 0:qb],
                                   pv[C_DH:, qb:] * inv[:, qb:]], axis=0)
            l_t = jnp.concatenate([jnp.broadcast_to(lse[:, 0:qb], (C_DH, qb)),
                                   jnp.broadcast_to(lse[:, qb:], (C_DH, qb))], axis=0)
            og_ref[b["gi"], b["rows"], :] = o_t.T
            lg_ref[b["gi"], b["rows"], :] = l_t.T
        return carry

    lax.fori_loop(0, n_blocks // per_iter, body, 0)

    lse = lg_ref[...]
    mx = jnp.max(lse, axis=0, keepdims=True)
    w = jnp.exp2(lse - mx)
    alpha = w / jnp.sum(w, axis=0, keepdims=True)
    o_ref[0, 0] = jnp.sum(alpha * og_ref[...], axis=0).astype(o_ref.dtype)


def _dilated_attention(projs):
    bsz, _, t_len, _ = projs[0].shape
    npairs = C_HEADS * C_DH // LANE
    ng = len(C_PATTERNS)

    def spec(col):
        return pl.BlockSpec((1, 1, t_len, LANE), lambda b, hp, col=col: (b, col * npairs + hp, 0, 0))

    return pl.pallas_call(
        functools.partial(_dil_body, t_len=t_len),
        out_shape=jax.ShapeDtypeStruct((bsz, npairs, t_len, LANE), BF16),
        grid=(bsz, npairs),
        in_specs=[spec(c) for _ in range(ng) for c in range(3)],
        out_specs=pl.BlockSpec((1, 1, t_len, LANE), lambda b, hp: (b, hp, 0, 0)),
        scratch_shapes=[pltpu.VMEM((ng, t_len, LANE), F32), pltpu.VMEM((ng, t_len, LANE), F32)],
        compiler_params=_cparams(("parallel", "parallel")),
        name="dilated_attention",
    )(*[projs[gi] for gi in range(ng) for _ in range(3)])


def _dilated_layer(h, g, w_in, w_out):
    bsz, t_len, d = h.shape
    gw = C_HEADS * C_DH
    scale = jnp.ones((3, 1), F32).at[0, :].set(C_DH ** -0.5 * LOG2E)
    w_c = w_in.reshape(d, len(C_PATTERNS), 3, gw)
    projs = []
    for gi, (_, dil) in enumerate(C_PATTERNS):
        w_g = (w_c[:, gi] * scale).reshape(d, 3 * gw).astype(BF16)
        p = _norm_matmul(h, g, w_g, out_dtype=BF16, tm=512, tn=3 * gw, slab=True, dil=dil,
                         name=f"dilated_in{gi}")
        projs.append(p.reshape(bsz, 3 * gw // LANE, t_len, LANE))
    o = _dilated_attention(projs)
    return _matmul_residual(o, w_out.astype(BF16), h, tm=512, slab=True, name="dilated_out")


def kernel(x, norm_mix, norm_ffn, norm_final, a_w_in, a_b_gate, a_conv_w, a_conv_b, a_wq, a_wk, a_wv, a_gn, a_skip, a_w_out, b_w_in, b_b_gate, b_pe_k, b_pe_v, b_phik_w1, b_phik_b1, b_phik_w2, b_phiv_w1, b_phiv_b1, b_phiv_w2, b_w_out, c_w_in, c_w_out, f_w_up, f_conv_w, f_conv_b, f_w_down):
    depth = norm_mix.shape[0]
    h = x
    for i in range(depth):
        kind, j = i % 3, i // 3
        if kind == 0:
            h = _mlstm_layer(h, norm_mix[i], a_w_in[j], a_b_gate[j], a_conv_w[j], a_conv_b[j],
                             a_wq[j], a_wk[j], a_wv[j], a_gn[j], a_skip[j], a_w_out[j])
        elif kind == 1:
            h = _nsa_layer(h, norm_mix[i], b_w_in[j], b_b_gate[j], b_pe_k[j], b_pe_v[j],
                           b_phik_w1[j], b_phik_b1[j], b_phik_w2[j],
                           b_phiv_w1[j], b_phiv_b1[j], b_phiv_w2[j], b_w_out[j])
        else:
            h = _dilated_layer(h, norm_mix[i], c_w_in[j], c_w_out[j])
        h = _conv_ffn(h, norm_ffn[i], f_w_up[i].astype(BF16), f_conv_w[i], f_conv_b[i],
                      f_w_down[i].astype(BF16), norm_final, tm=512,
                      final_norm=(i == depth - 1), name="conv_ffn")
    return h
```

```python
import functools

import jax
import jax.numpy as jnp
from jax import lax
from jax.experimental import pallas as pl
from jax.experimental.pallas import tpu as pltpu

F32 = jnp.float32
BF16 = jnp.bfloat16

NORM_EPS = 1e-6
NEG = -1e30
LOG2E = 1.4426950408889634
LANE = 128
SUBLANE = 8
VMEM_LIMIT = 56 * 1024 * 1024
NMM_COLS = 512
NMM_MAX_STRIDE = 4

A_HEADS = 4
A_CHUNK = 256
A_CONV = 4
B_HEADS = 16
B_DH = 64
B_GROUPS = 4
B_QPG = 4
B_CMP_LEN = 32
B_CMP_STRIDE = 16
B_SEL_LEN = 64
B_TOPN = 16
B_N_LOCAL = 2
B_WINDOW = 512
FORCE_SCORE = 1e9
B_TQ = 128
B_TK = 256
C_PATTERNS = ((128, 1), (512, 4), (2048, 16))
C_HEADS = 16
C_DH = 64
C_QBLK = 128
C_BLOCKS_PER_ITER = 4
F_CHUNK = 256
F_SPLIT = 2


def _cparams(sem):
    return pltpu.CompilerParams(dimension_semantics=sem, vmem_limit_bytes=VMEM_LIMIT)


def _const_spec(shape):
    nd = len(shape)
    return pl.BlockSpec(shape, lambda *_: (0,) * nd, pipeline_mode=pl.Buffered(1))


def _rmsnorm_rows(x, g):
    ms = jnp.mean(x * x, axis=-1, keepdims=True)
    return x * lax.rsqrt(ms + NORM_EPS) * g


def _sigmoid(x):
    return 1.0 / (1.0 + jnp.exp(-x))


def _nmm_body(x_ref, g_ref, w_ref, o_ref, xn_ref, *rest, slab, dil):
    @pl.when(pl.program_id(2) == 0)
    def _():
        xn_ref[...] = _rmsnorm_rows(x_ref[0], g_ref[...]).astype(BF16)

    if not slab:
        o_ref[0] = jnp.dot(xn_ref[...], w_ref[...],
                           preferred_element_type=F32).astype(o_ref.dtype)
        return
    tm, tn = xn_ref.shape[0], w_ref.shape[1]
    step = min(tn, NMM_COLS)
    for c0 in range(0, tn, step):
        r = jnp.dot(xn_ref[...], w_ref[:, c0:c0 + step], preferred_element_type=F32)
        for cl in range(step // LANE):
            c = c0 // LANE + cl
            piece = r[:, cl * LANE:(cl + 1) * LANE]
            if dil == 1:
                o_ref[0, c, 0] = piece.astype(o_ref.dtype)
            else:
                r_ref, t_ref = rest
                r_ref[c] = piece
                if dil <= NMM_MAX_STRIDE:
                    for rc in range(dil):
                        o_ref[0, c, rc] = r_ref[c, pl.ds(rc, tm // dil, stride=dil),
                                                :].astype(o_ref.dtype)
                else:
                    d1 = NMM_MAX_STRIDE
                    d2 = dil // d1
                    run = tm // d1
                    for a in range(d1):
                        t_ref[a * run:(a + 1) * run, :] = r_ref[c, pl.ds(a, run, stride=d1), :]
                    for a in range(d1):
                        for j in range(d2):
                            o_ref[0, c, d1 * j + a] = t_ref[
                                pl.ds(a * run + j, run // d2, stride=d2), :].astype(o_ref.dtype)


def _norm_matmul(h, g, w, *, out_dtype, tm, tn, slab=False, dil=1, name):
    bsz, t_len, d = h.shape
    n = w.shape[1]
    assert t_len % tm == 0 and n % tn == 0 and tn % LANE == 0 and tm % (2 * SUBLANE * dil) == 0
    scratch = [pltpu.VMEM((tm, d), BF16)]
    if slab:
        out_shape = jax.ShapeDtypeStruct((bsz, n // LANE, dil, t_len // dil, LANE), out_dtype)
        out_spec = pl.BlockSpec((1, tn // LANE, dil, tm // dil, LANE),
                                lambda b, i, j: (b, j, 0, i, 0))
        if dil > 1:
            assert dil <= NMM_MAX_STRIDE or dil % NMM_MAX_STRIDE == 0
            scratch += [pltpu.VMEM((tn // LANE, tm, LANE), F32), pltpu.VMEM((tm, LANE), F32)]
    else:
        out_shape = jax.ShapeDtypeStruct((bsz, t_len, n), out_dtype)
        out_spec = pl.BlockSpec((1, tm, tn), lambda b, i, j: (b, i, j))
    return pl.pallas_call(
        functools.partial(_nmm_body, slab=slab, dil=dil),
        out_shape=out_shape,
        grid=(bsz, t_len // tm, n // tn),
        in_specs=[
            pl.BlockSpec((1, tm, d), lambda b, i, j: (b, i, 0)),
            pl.BlockSpec((1, d), lambda b, i, j: (0, 0)),
            _const_spec((d, n)) if tn == n else pl.BlockSpec((d, tn), lambda b, i, j: (0, j)),
        ],
        out_specs=out_spec,
        scratch_shapes=scratch,
        compiler_params=_cparams(("parallel", "parallel", "arbitrary")),
        name=name,
    )(h, g.reshape(1, d), w)


def _mmr_body(x_ref, w_ref, r_ref, o_ref, *, slab):
    if slab:
        x = jnp.concatenate([x_ref[0, c] for c in range(x_ref.shape[1])], axis=-1)
    else:
        x = x_ref[0]
    o_ref[0] = r_ref[0] + jnp.dot(x, w_ref[...], preferred_element_type=F32)


def _matmul_residual(x, w, res, *, tm, slab=False, name):
    bsz, t_len, n = res.shape
    k = w.shape[0]
    if slab:
        x_spec = pl.BlockSpec((1, k // LANE, tm, LANE), lambda b, i: (b, 0, i, 0))
    else:
        x_spec = pl.BlockSpec((1, tm, k), lambda b, i: (b, i, 0))
    return pl.pallas_call(
        functools.partial(_mmr_body, slab=slab),
        out_shape=jax.ShapeDtypeStruct((bsz, t_len, n), F32),
        grid=(bsz, t_len // tm),
        in_specs=[x_spec, _const_spec((k, n)),
                  pl.BlockSpec((1, tm, n), lambda b, i: (b, i, 0))],
        out_specs=pl.BlockSpec((1, tm, n), lambda b, i: (b, i, 0)),
        compiler_params=_cparams(("parallel", "parallel")),
        name=name,
    )(x, w, res)


def _ffn_body(h_ref, hp_ref, g_ref, wup_ref, cw_ref, cb_ref, wdn_ref, gf_ref, o_ref,
              xn_ref, acc_ref, *, tm, f_hidden, final_norm):
    halo = SUBLANE
    g = g_ref[...]
    keep = (pl.program_id(1) > 0).astype(F32)
    xn_ref[0:halo, :] = (_rmsnorm_rows(hp_ref[0], g) * keep).astype(BF16)
    xn_ref[halo:, :] = _rmsnorm_rows(h_ref[0], g).astype(BF16)
    rows = tm // F_SPLIT
    xns = [xn_ref[s * rows:s * rows + rows + halo, :] for s in range(F_SPLIT)]

    def conv(u, col0):
        w = cw_ref[:, col0:col0 + F_CHUNK]
        b = cb_ref[:, col0:col0 + F_CHUNK]
        return (w[2:3] * u[halo:halo + rows] + w[1:2] * u[halo - 1:halo - 1 + rows]
                + w[0:1] * u[halo - 2:halo - 2 + rows] + b)

    def up(c):
        c0 = c * F_CHUNK
        return [(jnp.dot(xn, wup_ref[:, c0:c0 + F_CHUNK], preferred_element_type=F32),
                 jnp.dot(xn, wup_ref[:, f_hidden + c0:f_hidden + c0 + F_CHUNK],
                         preferred_element_type=F32)) for xn in xns]

    n_chunks = f_hidden // F_CHUNK
    nxt = up(0)
    for c in range(n_chunks):
        c0 = c * F_CHUNK
        cur = nxt
        if c + 1 < n_chunks:
            nxt = up(c + 1)
        for s, (ug, uv) in enumerate(cur):
            gate = conv(ug, c0)
            val = conv(uv, f_hidden + c0)
            act = (gate * _sigmoid(gate) * val).astype(BF16)
            part = jnp.dot(act, wdn_ref[c0:c0 + F_CHUNK, :], preferred_element_type=F32)
            if c == 0:
                acc_ref[s * rows:(s + 1) * rows, :] = part
            else:
                acc_ref[s * rows:(s + 1) * rows, :] += part

    out = h_ref[0] + acc_ref[...]
    if final_norm:
        out = _rmsnorm_rows(out, gf_ref[...])
    o_ref[0] = out


def _conv_ffn(h, g, w_up, conv_w, conv_b, w_down, g_final, *, tm, final_norm, name):
    bsz, t_len, d = h.shape
    f2 = w_up.shape[1]
    f_hidden = f2 // 2
    assert f_hidden % F_CHUNK == 0 and t_len % tm == 0 and tm % SUBLANE == 0
    rows_per_tile = tm // SUBLANE
    return pl.pallas_call(
        functools.partial(_ffn_body, tm=tm, f_hidden=f_hidden, final_norm=final_norm),
        out_shape=jax.ShapeDtypeStruct((bsz, t_len, d), F32),
        grid=(bsz, t_len // tm),
        in_specs=[
            pl.BlockSpec((1, tm, d), lambda b, i: (b, i, 0)),
            pl.BlockSpec((1, SUBLANE, d),
                         lambda b, i: (b, jnp.maximum(i * rows_per_tile - 1, 0), 0)),
            _const_spec((1, d)),
            _const_spec((d, f2)),
            _const_spec((3, f2)),
            _const_spec((1, f2)),
            _const_spec((f_hidden, d)),
            _const_spec((1, d)),
        ],
        out_specs=pl.BlockSpec((1, tm, d), lambda b, i: (b, i, 0)),
        scratch_shapes=[pltpu.VMEM((tm + SUBLANE, d), BF16), pltpu.VMEM((tm, d), F32)],
        compiler_params=_cparams(("parallel", "parallel")),
        name=name,
    )(h, h, g.reshape(1, d), w_up, conv_w, conv_b.reshape(1, f2), w_down, g_final.reshape(1, d))


def _mlstm_body(h_ref, g_ref, win_ref, bg_ref, cw_ref, cb_ref, wq_ref, wk_ref, wv_ref,
                gn_ref, sk_ref, wout_ref, o_ref, c_ref, n_ref, m_ref, xp_ref, y_ref,
                *, chunk, dh):
    L = chunk
    halo = SUBLANE
    inner = A_HEADS * dh

    @pl.when(pl.program_id(1) == 0)
    def _():
        c_ref[...] = jnp.zeros_like(c_ref)
        n_ref[...] = jnp.zeros_like(n_ref)
        m_ref[...] = jnp.zeros_like(m_ref)
        xp_ref[...] = jnp.zeros_like(xp_ref)

    xn = _rmsnorm_rows(h_ref[0], g_ref[...]).astype(BF16)
    xm = jnp.dot(xn, win_ref[:, 0:inner], preferred_element_type=F32)
    gates = jnp.dot(xn, win_ref[:, 2 * inner:], preferred_element_type=F32) + bg_ref[...]
    heads = [dict(lo=hd * dh, hi=(hd + 1) * dh) for hd in range(A_HEADS)]
    for hd, st in enumerate(heads):
        st["v"] = jnp.dot(xm[:, st["lo"]:st["hi"]].astype(BF16), wv_ref[hd],
                          preferred_element_type=F32)
    o_pre = jnp.dot(xn, win_ref[:, inner:2 * inner], preferred_element_type=F32)

    xe = jnp.concatenate([xp_ref[...], xm], axis=0)
    cw = cw_ref[...]
    xc = (cw[3:4] * xe[halo:halo + L] + cw[2:3] * xe[halo - 1:halo - 1 + L]
          + cw[1:2] * xe[halo - 2:halo - 2 + L] + cw[0:1] * xe[halo - 3:halo - 3 + L]
          + cb_ref[...])
    xc = xc * _sigmoid(xc)
    xp_ref[...] = xm[L - halo:L]

    ii = lax.broadcasted_iota(jnp.int32, (L, L), 0)
    jj = lax.broadcasted_iota(jnp.int32, (L, L), 1)
    eye = ii == jj
    tril = jj <= ii

    def to_row(col):
        return jnp.sum(jnp.where(eye, col, 0.0), axis=0, keepdims=True)

    for hd, st in enumerate(heads):
        st["xc"] = xc[:, st["lo"]:st["hi"]]
        xcb = st["xc"].astype(BF16)
        st["q"] = jnp.dot(xcb, wq_ref[hd], preferred_element_type=F32)
        st["k"] = jnp.dot(xcb, wk_ref[hd], preferred_element_type=F32)

    for hd, st in enumerate(heads):
        li = gates[:, hd:hd + 1]
        fpre = gates[:, A_HEADS + hd:A_HEADS + hd + 1]
        lf = jnp.minimum(fpre, 0.0) - jnp.log1p(jnp.exp(-jnp.abs(fpre)))
        li_row = to_row(li)
        lf_row = to_row(lf)
        b_col = jnp.sum(jnp.where(tril, lf_row, 0.0), axis=1, keepdims=True)
        b_row = to_row(b_col)
        m_st = m_ref[hd][:, 0:1]
        log_d = jnp.where(tril, b_col - b_row + li_row, NEG)
        log_inter = b_col + m_st
        m = jnp.maximum(log_inter, jnp.max(log_d, axis=1, keepdims=True))
        st["m"] = m
        st["w_intra"] = jnp.exp(log_d - m)
        st["w_inter"] = jnp.exp(log_inter - m)
        b_last = b_col[L - 1:L, :]
        log_w = b_last - b_col + li
        m_new = jnp.maximum(b_last + m_st, jnp.max(log_w, axis=0, keepdims=True))
        st["decay"] = jnp.exp(b_last + m_st - m_new)
        st["w_k"] = jnp.exp(log_w - m_new)
        m_ref[hd] = jnp.broadcast_to(m_new, m_ref.shape[1:])

    for hd, st in enumerate(heads):
        st["k"] = st["k"] * (dh ** -0.5)
        st["qb"] = st["q"].astype(BF16)
        st["vb"] = st["v"].astype(BF16)
        st["s"] = lax.dot_general(st["qb"], st["k"].astype(BF16), (((1,), (1,)), ((), ())),
                                  preferred_element_type=F32)
        st["qc"] = jnp.dot(st["qb"], c_ref[hd].astype(BF16), preferred_element_type=F32)

    for hd, st in enumerate(heads):
        s = st["s"] * st["w_intra"]
        st["den"] = (st["w_inter"] * jnp.sum(st["q"] * n_ref[hd], axis=1, keepdims=True)
                     + jnp.sum(s, axis=1, keepdims=True))
        wk = st["w_k"] * st["k"]
        n_ref[hd] = st["decay"] * n_ref[hd] + jnp.sum(wk, axis=0, keepdims=True)
        st["sv"] = jnp.dot(s.astype(BF16), st["vb"], preferred_element_type=F32)
        st["kv"] = lax.dot_general(wk.astype(BF16), st["vb"], (((0,), (0,)), ((), ())),
                                   preferred_element_type=F32)

    for hd, st in enumerate(heads):
        lo, hi = st["lo"], st["hi"]
        c_ref[hd] = st["decay"] * c_ref[hd] + st["kv"]
        num = st["w_inter"] * st["qc"] + st["sv"]
        hh = num / jnp.maximum(jnp.abs(st["den"]), jnp.exp(-st["m"]))
        mu = jnp.mean(hh, axis=-1, keepdims=True)
        ctr = hh - mu
        var = jnp.mean(ctr * ctr, axis=-1, keepdims=True)
        hn = ctr * lax.rsqrt(var + NORM_EPS) * gn_ref[:, lo:hi]
        y = _sigmoid(o_pre[:, lo:hi]) * hn + sk_ref[:, lo:hi] * st["xc"]
        y_ref[:, lo:hi] = y.astype(y_ref.dtype)

    o_ref[0] = h_ref[0] + jnp.dot(y_ref[...], wout_ref[...], preferred_element_type=F32)


def _mlstm_layer(h, g, w_in, b_gate, conv_w, conv_b, wq, wk, wv, gn, skip, w_out):
    bsz, t_len, d = h.shape
    inner = w_out.shape[0]
    dh = inner // A_HEADS
    L = A_CHUNK
    w_a = w_in.astype(BF16)
    bg = b_gate.reshape(1, 2 * A_HEADS)
    return pl.pallas_call(
        functools.partial(_mlstm_body, chunk=L, dh=dh),
        out_shape=jax.ShapeDtypeStruct((bsz, t_len, d), F32),
        grid=(bsz, t_len // L),
        in_specs=[
            pl.BlockSpec((1, L, d), lambda b, c: (b, c, 0)),
            _const_spec((1, d)),
            _const_spec((d, 2 * inner + 2 * A_HEADS)),
            _const_spec((1, 2 * A_HEADS)),
            _const_spec((A_CONV, inner)),
            _const_spec((1, inner)),
            _const_spec((A_HEADS, dh, dh)),
            _const_spec((A_HEADS, dh, dh)),
            _const_spec((A_HEADS, dh, dh)),
            _const_spec((1, inner)),
            _const_spec((1, inner)),
            _const_spec((inner, d)),
        ],
        out_specs=pl.BlockSpec((1, L, d), lambda b, c: (b, c, 0)),
        scratch_shapes=[
            pltpu.VMEM((A_HEADS, dh, dh), F32),
            pltpu.VMEM((A_HEADS, 1, dh), F32),
            pltpu.VMEM((A_HEADS, 1, LANE), F32),
            pltpu.VMEM((SUBLANE, inner), F32),
            pltpu.VMEM((L, inner), BF16),
        ],
        compiler_params=_cparams(("parallel", "arbitrary")),
        name="mlstm_layer",
    )(h, g.reshape(1, d), w_a, bg, conv_w, conv_b.reshape(1, inner), wq.astype(BF16),
      wk.astype(BF16), wv.astype(BF16), gn.reshape(1, inner), skip.reshape(1, inner),
      w_out.astype(BF16))


def _cmp_body(x_ref, pe_ref, w1_ref, b1_ref, w2_ref, o_ref):
    half = x_ref.shape[-1]
    x = x_ref[0, 0, 0]
    pe = pe_ref[0]
    a = (x + pe[:, :half]).astype(BF16)
    b = (x + pe[:, half:]).astype(BF16)
    w1 = w1_ref[0]
    pa = jnp.dot(a, w1[:half], preferred_element_type=F32)
    pb = jnp.dot(b, w1[half:], preferred_element_type=F32)
    pb_next = jnp.concatenate([pb[1:], jnp.zeros((1, pb.shape[1]), F32)], axis=0)
    pre = pa + pb_next + b1_ref[0]
    hid = (pre * _sigmoid(pre)).astype(BF16)
    out = jnp.dot(hid, w2_ref[0], preferred_element_type=F32)
    row = lax.broadcasted_iota(jnp.int32, out.shape, 0)
    o_ref[0, 0, 0] = jnp.where(row < out.shape[0] - 1, out, 0.0).astype(o_ref.dtype)


def _nsa_compress(kv_cmp, pe, w1, b1, w2):
    two, bsz, g, t_len, dh = kv_cmp.shape
    n_half = t_len // B_CMP_STRIDE
    half = B_CMP_STRIDE * dh
    x = kv_cmp.reshape(two, bsz, g, n_half, half)
    hidden = w1.shape[-1]
    return pl.pallas_call(
        _cmp_body,
        out_shape=jax.ShapeDtypeStruct((two, bsz, g, n_half, dh), BF16),
        grid=(two, bsz, g),
        in_specs=[
            pl.BlockSpec((1, 1, 1, n_half, half), lambda s, b, gi: (s, b, gi, 0, 0)),
            pl.BlockSpec((1, 1, 2 * half), lambda s, b, gi: (s, 0, 0)),
            pl.BlockSpec((1, 2 * half, hidden), lambda s, b, gi: (s, 0, 0)),
            pl.BlockSpec((1, 1, hidden), lambda s, b, gi: (s, 0, 0)),
            pl.BlockSpec((1, hidden, dh), lambda s, b, gi: (s, 0, 0)),
        ],
        out_specs=pl.BlockSpec((1, 1, 1, n_half, dh), lambda s, b, gi: (s, b, gi, 0, 0)),
        compiler_params=_cparams(("parallel", "parallel", "parallel")),
        name="nsa_compress",
    )(x, pe.reshape(two, 1, 2 * half), w1, b1.reshape(two, 1, hidden), w2)


def _flash_t(q_ts, k_ref, vt_ref, acc_ref, m_ref, kt_lo, kt_hi, bias_fn, last_bias_fn, tk):
    acc_ref[...] = jnp.zeros_like(acc_ref)
    m_ref[...] = jnp.full_like(m_ref, NEG)
    groups = range(len(q_ts))

    def tile(kt, bias_of):
        ks = pl.multiple_of(kt * tk, tk)
        ss = [jnp.dot(k_ref[0, g, pl.ds(ks, tk), :], q_ts[g], preferred_element_type=F32)
              for g in groups]
        bias = None if bias_of is None else bias_of(kt, ks)
        ps, alphas = [], []
        for g in groups:
            s = ss[g] if bias is None else ss[g] + bias
            m = m_ref[g]
            m_new = jnp.maximum(m, jnp.max(s, axis=0, keepdims=True))
            alphas.append(jnp.exp2(m - m_new))
            ps.append(jnp.exp2(s - m_new).astype(BF16))
            m_ref[g] = m_new
        for g in groups:
            v_t = vt_ref[0, g, :, pl.ds(ks, tk)]
            acc_ref[g] = alphas[g] * acc_ref[g] + jnp.dot(v_t, ps[g],
                                                          preferred_element_type=F32)

    def body(kt, carry):
        tile(kt, bias_fn)
        return carry

    lax.fori_loop(kt_lo, kt_hi - 1, body, 0)
    tile(kt_hi - 1, last_bias_fn)
    dh = acc_ref.shape[1] - 16
    return [acc_ref[g, 0:dh, :] / acc_ref[g, dh:dh + 1, :] for g in groups]


def _nsa_body(q_ref, kc_ref, vc_ref, ks_ref, vs_ref, kw_ref, vw_ref, gp_ref, bg_ref,
              ovl_ref, wout_ref, h_ref, o_ref, acc_ref, m_ref, og_ref, *, tq, tk, t_len):
    qi = pl.program_id(1)
    t0 = qi * tq
    gw = B_QPG * B_DH
    ncp = kc_ref.shape[2]
    nsel = t_len // B_SEL_LEN
    n_top = min(B_TOPN, nsel)

    def lanes4(x):
        return jnp.concatenate([x] * B_QPG, axis=1)

    n_idx = lax.broadcasted_iota(jnp.int32, (ncp, tq), 0)
    t_col = t0 + lax.broadcasted_iota(jnp.int32, (ncp, tq), 1)
    cmask = (n_idx * B_CMP_STRIDE + (B_CMP_LEN - 1) <= t_col) & (n_idx < ncp - 1)
    cbias = lanes4(jnp.where(cmask, 0.0, NEG))
    ckeep = lanes4(cmask.astype(F32))
    blk = lax.broadcasted_iota(jnp.int32, (nsel, tq), 0)
    cur = (t0 + lax.broadcasted_iota(jnp.int32, (nsel, tq), 1)) // B_SEL_LEN
    valid = blk <= cur
    forced = ((blk == 0) | (cur - blk < B_N_LOCAL)) & valid

    q_ts, q_sels, o_cmps = [], [], []
    for g in range(B_GROUPS):
        q_full = q_ref[0, :, g * gw:(g + 1) * gw].astype(F32).T.astype(BF16)
        q_t = jnp.concatenate([q_full[r * B_DH:(r + 1) * B_DH, :] for r in range(B_QPG)], axis=1)
        q_ts.append(q_t)

        s = jnp.dot(kc_ref[0, g], q_t, preferred_element_type=F32) + cbias
        m = jnp.max(s, axis=0, keepdims=True)
        e = jnp.exp2(s - m) * ckeep
        den = jnp.sum(e, axis=0, keepdims=True)
        p = e / jnp.where(den > 0, den, 1.0)
        o_cmps.append(lax.dot_general(vc_ref[0, g], p.astype(BF16), (((0,), (0,)), ((), ())),
                                      preferred_element_type=F32))

        psum = p[:, 0:tq]
        for r in range(1, B_QPG):
            psum = psum + p[:, r * tq:(r + 1) * tq]
        p_hi = psum.astype(BF16)
        r1 = psum - p_hi.astype(F32)
        p_mid = r1.astype(BF16)
        p_lo = (r1 - p_mid.astype(F32)).astype(BF16)
        ovl = ovl_ref[...]
        imp = (jnp.dot(ovl, p_hi, preferred_element_type=F32)
               + jnp.dot(ovl, p_mid, preferred_element_type=F32)
               + jnp.dot(ovl, p_lo, preferred_element_type=F32))
        imp = imp[0:nsel]
        score = jnp.where(forced, FORCE_SCORE, jnp.where(valid, imp, -FORCE_SCORE))
        sel_bias = jnp.full((nsel, tq), NEG, F32)
        for j in range(nsel):
            sj = score[j:j + 1, :]
            ahead = (score > sj) | ((score == sj) & (blk < j))
            cnt = jnp.sum(ahead.astype(F32), axis=0, keepdims=True)
            sel_bias = jnp.where((blk == j) & (cnt < n_top) & valid, 0.0, sel_bias)
        q_sels.append(jnp.concatenate(
            [q_t, lanes4(sel_bias).astype(BF16),
             jnp.zeros((LANE - B_DH - nsel, B_QPG * tq), BF16)], axis=0))

    k_off = lax.broadcasted_iota(jnp.int32, (tk, tq), 0)
    t_pos = t0 + lax.broadcasted_iota(jnp.int32, (tk, tq), 1)

    def causal_bias(kt, ks):
        return lanes4(jnp.where(ks + k_off <= t_pos, 0.0, NEG))

    kt_hi = (t0 + tq + tk - 1) // tk
    o_sels = _flash_t(q_sels, ks_ref, vs_ref, acc_ref, m_ref, 0, kt_hi, None, causal_bias, tk)

    def win_bias(kt, ks):
        diff = t_pos - (ks + k_off)
        return lanes4(jnp.where((diff >= 0) & (diff <= B_WINDOW - 1), 0.0, NEG))

    w_lo = jnp.maximum(t0 - (B_WINDOW - 1), 0) // tk
    o_wins = _flash_t(q_ts, kw_ref, vw_ref, acc_ref, m_ref, w_lo, kt_hi, win_bias, win_bias, tk)

    for g in range(B_GROUPS):
        gate = _sigmoid(gp_ref[0, g] + bg_ref[g])
        outs = []
        for r in range(B_QPG):
            sl = slice(r * tq, (r + 1) * tq)
            outs.append(gate[r:r + 1, :] * o_cmps[g][:, sl]
                        + gate[B_QPG + r:B_QPG + r + 1, :] * o_sels[g][:, sl]
                        + gate[2 * B_QPG + r:2 * B_QPG + r + 1, :] * o_wins[g][:, sl])
        og_ref[:, g * gw:(g + 1) * gw] = jnp.concatenate(outs, axis=0).T.astype(og_ref.dtype)

    o_ref[0] = h_ref[0] + jnp.dot(og_ref[...], wout_ref[...], preferred_element_type=F32)


def _nsa_attention(q, kc, vc, k_sel, vt_sel, k_win, vt_win, gpre_t, bg, w_out, h):
    bsz, t_len, _ = q.shape
    d = h.shape[-1]
    tq, tk = B_TQ, B_TK
    ncp = kc.shape[2]
    nsel = t_len // B_SEL_LEN
    gw = B_QPG * B_DH
    c_start = jnp.arange(ncp)[None, :] * B_CMP_STRIDE
    s_start = jnp.arange(LANE)[:, None] * B_SEL_LEN
    ovl = ((c_start < s_start + B_SEL_LEN) & (c_start + B_CMP_LEN > s_start)
           & (jnp.arange(LANE)[:, None] < nsel) & (jnp.arange(ncp)[None, :] < ncp - 1)).astype(BF16)

    def k_spec(n_rows, width=B_DH):
        return pl.BlockSpec((1, B_GROUPS, n_rows, width), lambda b, i: (b, 0, 0, 0))

    vt_spec = pl.BlockSpec((1, B_GROUPS, B_DH + 16, t_len), lambda b, i: (b, 0, 0, 0))
    n_q = B_QPG * tq

    return pl.pallas_call(
        functools.partial(_nsa_body, tq=tq, tk=tk, t_len=t_len),
        out_shape=jax.ShapeDtypeStruct((bsz, t_len, d), F32),
        grid=(bsz, t_len // tq),
        in_specs=[
            pl.BlockSpec((1, tq, B_GROUPS * gw), lambda b, i: (b, i, 0)),
            k_spec(ncp), k_spec(ncp), k_spec(t_len, LANE), vt_spec, k_spec(t_len), vt_spec,
            pl.BlockSpec((1, B_GROUPS, 16, tq), lambda b, i: (b, 0, 0, i)),
            _const_spec((B_GROUPS, 16, tq)),
            _const_spec((LANE, ncp)),
            _const_spec((B_GROUPS * gw, d)),
            pl.BlockSpec((1, tq, d), lambda b, i: (b, i, 0)),
        ],
        out_specs=pl.BlockSpec((1, tq, d), lambda b, i: (b, i, 0)),
        scratch_shapes=[pltpu.VMEM((B_GROUPS, B_DH + 16, n_q), F32),
                        pltpu.VMEM((B_GROUPS, 1, n_q), F32),
                        pltpu.VMEM((tq, B_GROUPS * gw), BF16)],
        compiler_params=_cparams(("parallel", "parallel")),
        name="nsa_attention",
    )(q, kc, vc, k_sel, vt_sel, k_win, vt_win, gpre_t, bg, ovl, w_out, h)


def _nsa_in_body(x_ref, g_ref, w_ref, q_ref, kvc_ref, ks_ref, vs_ref, kw_ref, vw_ref, gp_ref):
    nq = B_HEADS * B_DH
    gw = B_GROUPS * B_DH
    xn = _rmsnorm_rows(x_ref[0], g_ref[...]).astype(BF16)
    q_ref[0] = jnp.dot(xn, w_ref[:, 0:nq], preferred_element_type=F32).astype(q_ref.dtype)

    def seg(i):
        return jnp.dot(xn, w_ref[:, nq + i * gw:nq + (i + 1) * gw], preferred_element_type=F32)

    def put_rows(ref, lead, r):
        for gi in range(B_GROUPS):
            ref[lead + (gi,)] = r[:, gi * B_DH:(gi + 1) * B_DH].astype(ref.dtype)

    def put_cols(ref, r, width, ones_rows=0):
        r_t = r.T
        tm = r.shape[0]
        for gi in range(B_GROUPS):
            ref[0, gi, 0:width, :] = r_t[gi * width:(gi + 1) * width, :].astype(ref.dtype)
            if ones_rows:
                ref[0, gi, width:width + ones_rows, :] = jnp.ones((ones_rows, tm), ref.dtype)

    put_rows(kvc_ref, (0, 0), seg(0))
    put_rows(kvc_ref, (1, 0), seg(1))
    r = seg(2)
    tm = r.shape[0]
    blk = (pl.program_id(1) * tm + lax.broadcasted_iota(jnp.int32, (tm, LANE - B_DH), 0)) // B_SEL_LEN
    onehot = (blk == lax.broadcasted_iota(jnp.int32, (tm, LANE - B_DH), 1)).astype(F32)
    for gi in range(B_GROUPS):
        ks_ref[0, gi] = jnp.concatenate([r[:, gi * B_DH:(gi + 1) * B_DH], onehot],
                                        axis=1).astype(ks_ref.dtype)
    put_cols(vs_ref, seg(3), B_DH, ones_rows=16)
    put_rows(kw_ref, (0,), seg(4))
    put_cols(vw_ref, seg(5), B_DH, ones_rows=16)
    gates = jnp.dot(xn, w_ref[:, nq + 6 * gw:], preferred_element_type=F32)
    put_cols(gp_ref, gates, 16)


def _nsa_in(h, g, w):
    bsz, t_len, d = h.shape
    tm = 512
    nq = B_HEADS * B_DH
    G = B_GROUPS
    sds = jax.ShapeDtypeStruct
    k_shape = sds((bsz, G, t_len, B_DH), BF16)
    vt_shape = sds((bsz, G, B_DH + 16, t_len), BF16)
    k_spec = pl.BlockSpec((1, G, tm, B_DH), lambda b, i: (b, 0, i, 0))
    vt_spec = pl.BlockSpec((1, G, B_DH + 16, tm), lambda b, i: (b, 0, 0, i))
    return pl.pallas_call(
        _nsa_in_body,
        out_shape=(sds((bsz, t_len, nq), BF16), sds((2, bsz, G, t_len, B_DH), F32),
                   sds((bsz, G, t_len, LANE), BF16), vt_shape, k_shape, vt_shape,
                   sds((bsz, G, 16, t_len), F32)),
        grid=(bsz, t_len // tm),
        in_specs=[pl.BlockSpec((1, tm, d), lambda b, i: (b, i, 0)),
                  _const_spec((1, d)), _const_spec(w.shape)],
        out_specs=(pl.BlockSpec((1, tm, nq), lambda b, i: (b, i, 0)),
                   pl.BlockSpec((2, 1, G, tm, B_DH), lambda b, i: (0, b, 0, i, 0)),
                   pl.BlockSpec((1, G, tm, LANE), lambda b, i: (b, 0, i, 0)), vt_spec,
                   k_spec, vt_spec,
                   pl.BlockSpec((1, G, 16, tm), lambda b, i: (b, 0, 0, i))),
        compiler_params=_cparams(("parallel", "parallel")),
        name="nsa_in",
    )(h, g.reshape(1, d), w)


def _nsa_layer(h, g, w_in, b_gate, pe_k, pe_v, phik_w1, phik_b1, phik_w2,
               phiv_w1, phiv_b1, phiv_w2, w_out):
    bsz, t_len, d = h.shape
    nq = B_HEADS * B_DH
    nkv = 6 * B_GROUPS * B_DH
    wg = w_in[:, nq + nkv:].reshape(d, B_GROUPS, B_QPG, 3).transpose(0, 1, 3, 2)
    wg = jnp.pad(wg.reshape(d, B_GROUPS, 3 * B_QPG), ((0, 0), (0, 0), (0, 16 - 3 * B_QPG)))
    wg = jnp.pad(wg.reshape(d, B_GROUPS * 16), ((0, 0), (0, LANE - B_GROUPS * 16)))
    bg = b_gate.reshape(B_GROUPS, B_QPG, 3).transpose(0, 2, 1).reshape(B_GROUPS, 3 * B_QPG)
    bg = jnp.pad(bg, ((0, 0), (0, 16 - 3 * B_QPG)))
    bg = jnp.broadcast_to(bg[:, :, None], (B_GROUPS, 16, B_TQ))
    w_b = jnp.concatenate([w_in[:, :nq] * (B_DH ** -0.5 * LOG2E), w_in[:, nq:nq + nkv], wg],
                          axis=1).astype(BF16)
    q, kv_cmp, k_sel, vt_sel, k_win, vt_win, gpre_t = _nsa_in(h, g, w_b)

    cmp = _nsa_compress(
        kv_cmp, jnp.stack([pe_k, pe_v]), jnp.stack([phik_w1, phiv_w1]).astype(BF16),
        jnp.stack([phik_b1, phiv_b1]), jnp.stack([phik_w2, phiv_w2]).astype(BF16))
    return _nsa_attention(q, cmp[0], cmp[1], k_sel, vt_sel, k_win, vt_win, gpre_t, bg,
                          w_out.astype(BF16), h)


def _dil_body(*refs, t_len):
    qkv = refs[:9]
    o_ref = refs[9]
    og_ref, lg_ref = refs[10], refs[11]
    qb = C_QBLK
    assert qb == LANE and 2 * C_DH == LANE
    top = lax.broadcasted_iota(jnp.int32, (LANE, qb), 0) < C_DH

    n_blocks = t_len // qb
    per_iter = C_BLOCKS_PER_ITER
    assert n_blocks % per_iter == 0

    def body(it, carry):
        blocks = []
        for gi, (window, dil) in enumerate(C_PATTERNS):
            sub = t_len // dil
            nb = sub // qb
            nk = qb if nb == 1 else 2 * qb
            for j in range(per_iter):
                i = it * per_iter + j
                rc = i // nb
                bq = i % nb
                bk = jnp.maximum(bq - 1, 0)
                q0 = pl.multiple_of(rc * sub + bq * qb, qb)
                k0 = pl.multiple_of(rc * sub + bk * qb, qb)
                if dil == 1:
                    rows = pl.ds(q0, qb)
                else:
                    rows = pl.ds(rc + bq * (qb * dil), qb, stride=dil)
                blocks.append(dict(gi=gi, nk=nk, k0=k0, rows=rows, shift=(bq - bk) * qb,
                                   max_dist=window // dil,
                                   q2=qkv[3 * gi][0, 0, pl.ds(q0, qb), :]))
        for b in blocks:
            q_t = b["q2"].astype(F32).T
            b["q_bd"] = jnp.concatenate([jnp.where(top, q_t, 0.0), jnp.where(top, 0.0, q_t)],
                                        axis=1).astype(BF16)
        for b in blocks:
            k2 = qkv[3 * b["gi"] + 1][0, 0, pl.ds(b["k0"], b["nk"]), :]
            b["s"] = jnp.dot(k2, b["q_bd"], preferred_element_type=F32)
        for b in blocks:
            nk = b["nk"]
            diff = (b["shift"] + lax.broadcasted_iota(jnp.int32, (nk, qb), 1)
                    - lax.broadcasted_iota(jnp.int32, (nk, qb), 0))
            bias = jnp.where((diff >= 0) & (diff <= b["max_dist"]), 0.0, NEG)
            s = b["s"] + jnp.concatenate([bias, bias], axis=1)
            m = jnp.max(s, axis=0, keepdims=True)
            e = jnp.exp2(s - m)
            den = jnp.sum(e, axis=0, keepdims=True)
            b["e"] = e.astype(BF16)
            b["inv"] = 1.0 / den
            b["lse"] = m + jnp.log2(den)
        for b in blocks:
            v2 = qkv[3 * b["gi"] + 2][0, 0, pl.ds(b["k0"], b["nk"]), :]
            b["pv"] = lax.dot_general(v2, b["e"], (((0,), (0,)), ((), ())),
                                      preferred_element_type=F32)
        for b in blocks:
            pv, inv, lse = b["pv"], b["inv"], b["lse"]
            o_t = jnp.concatenate([pv[0:C_DH, 0:qb] * inv[:, 0:qb],
                                   pv[C_DH:, qb:] * inv[:, qb:]], axis=0)
            l_t = jnp.concatenate([jnp.broadcast_to(lse[:, 0:qb], (C_DH, qb)),
                                   jnp.broadcast_to(lse[:, qb:], (C_DH, qb))], axis=0)
            og_ref[b["gi"], b["rows"], :] = o_t.T
            lg_ref[b["gi"], b["rows"], :] = l_t.T
        return carry

    lax.fori_loop(0, n_blocks // per_iter, body, 0)

    lse = lg_ref[...]
    mx = jnp.max(lse, axis=0, keepdims=True)
    w = jnp.exp2(lse - mx)
    alpha = w / jnp.sum(w, axis=0, keepdims=True)
    o_ref[0, 0] = jnp.sum(alpha * og_ref[...], axis=0).astype(o_ref.dtype)


def _dilated_attention(projs):
    bsz, _, t_len, _ = projs[0].shape
    npairs = C_HEADS * C_DH // LANE
    ng = len(C_PATTERNS)

    def spec(col):
        return pl.BlockSpec((1, 1, t_len, LANE), lambda b, hp, col=col: (b, col * npairs + hp, 0, 0))

    return pl.pallas_call(
        functools.partial(_dil_body, t_len=t_len),
        out_shape=jax.ShapeDtypeStruct((bsz, npairs, t_len, LANE), BF16),
        grid=(bsz, npairs),
        in_specs=[spec(c) for _ in range(ng) for c in range(3)],
        out_specs=pl.BlockSpec((1, 1, t_len, LANE), lambda b, hp: (b, hp, 0, 0)),
        scratch_shapes=[pltpu.VMEM((ng, t_len, LANE), F32), pltpu.VMEM((ng, t_len, LANE), F32)],
        compiler_params=_cparams(("parallel", "parallel")),
        name="dilated_attention",
    )(*[projs[gi] for gi in range(ng) for _ in range(3)])


def _dilated_layer(h, g, w_in, w_out):
    bsz, t_len, d = h.shape
    gw = C_HEADS * C_DH
    scale = jnp.ones((3, 1), F32).at[0, :].set(C_DH ** -0.5 * LOG2E)
    w_c = w_in.reshape(d, len(C_PATTERNS), 3, gw)
    projs = []
    for gi, (_, dil) in enumerate(C_PATTERNS):
        w_g = (w_c[:, gi] * scale).reshape(d, 3 * gw).astype(BF16)
        p = _norm_matmul(h, g, w_g, out_dtype=BF16, tm=512, tn=3 * gw, slab=True, dil=dil,
                         name=f"dilated_in{gi}")
        projs.append(p.reshape(bsz, 3 * gw // LANE, t_len, LANE))
    o = _dilated_attention(projs)
    return _matmul_residual(o, w_out.astype(BF16), h, tm=512, slab=True, name="dilated_out")


def kernel(x, norm_mix, norm_ffn, norm_final, a_w_in, a_b_gate, a_conv_w, a_conv_b, a_wq, a_wk, a_wv, a_gn, a_skip, a_w_out, b_w_in, b_b_gate, b_pe_k, b_pe_v, b_phik_w1, b_phik_b1, b_phik_w2, b_phiv_w1, b_phiv_b1, b_phiv_w2, b_w_out, c_w_in, c_w_out, f_w_up, f_conv_w, f_conv_b, f_w_down):
    depth = norm_mix.shape[0]
    h = x
    for i in range(depth):
        kind, j = i % 3, i // 3
        if kind == 0:
            h = _mlstm_layer(h, norm_mix[i], a_w_in[j], a_b_gate[j], a_conv_w[j], a_conv_b[j],
                             a_wq[j], a_wk[j], a_wv[j], a_gn[j], a_skip[j], a_w_out[j])
        elif kind == 1:
            h = _nsa_layer(h, norm_mix[i], b_w_in[j], b_b_gate[j], b_pe_k[j], b_pe_v[j],
                           b_phik_w1[j], b_phik_b1[j], b_phik_w2[j],
                           b_phiv_w1[j], b_phiv_b1[j], b_phiv_w2[j], b_w_out[j])
        else:
            h = _dilated_layer(h, norm_mix[i], c_w_in[j], c_w_out[j])
        h = _conv_ffn(h, norm_ffn[i], f_w_up[i].astype(BF16), f_conv_w[i], f_conv_b[i],
                      f_w_down[i].astype(BF16), norm_final, tm=512,
                      final_norm=(i == depth - 1), name="conv_ffn")
    return h
```

```python
import functools

import jax
import jax.numpy as jnp
from jax import lax
from jax.experimental import pallas as pl
from jax.experimental.pallas import tpu as pltpu

F32 = jnp.float32
BF16 = jnp.bfloat16

NORM_EPS = 1e-6
NEG = -1e30
LOG2E = 1.4426950408889634
LANE = 128
SUBLANE = 8
VMEM_LIMIT = 56 * 1024 * 1024
NMM_COLS = 512
NMM_MAX_STRIDE = 4

A_HEADS = 4
A_CHUNK = 256
A_CONV = 4
B_HEADS = 16
B_DH = 64
B_GROUPS = 4
B_QPG = 4
B_CMP_LEN = 32
B_CMP_STRIDE = 16
B_SEL_LEN = 64
B_TOPN = 16
B_N_LOCAL = 2
B_WINDOW = 512
FORCE_SCORE = 1e9
B_TQ = 128
B_TK = 256
C_PATTERNS = ((128, 1), (512, 4), (2048, 16))
C_HEADS = 16
C_DH = 64
C_QBLK = 128
C_BLOCKS_PER_ITER = 8
F_CHUNK = 256
F_SPLIT = 2


def _cparams(sem):
    return pltpu.CompilerParams(dimension_semantics=sem, vmem_limit_bytes=VMEM_LIMIT)


def _const_spec(shape):
    nd = len(shape)
    return pl.BlockSpec(shape, lambda *_: (0,) * nd, pipeline_mode=pl.Buffered(1))


def _rmsnorm_rows(x, g):
    ms = jnp.mean(x * x, axis=-1, keepdims=True)
    return x * lax.rsqrt(ms + NORM_EPS) * g


def _sigmoid(x):
    return 1.0 / (1.0 + jnp.exp(-x))


def _nmm_body(x_ref, g_ref, w_ref, o_ref, xn_ref, *rest, slab, dil):
    @pl.when(pl.program_id(2) == 0)
    def _():
        xn_ref[...] = _rmsnorm_rows(x_ref[0], g_ref[...]).astype(BF16)

    if not slab:
        o_ref[0] = jnp.dot(xn_ref[...], w_ref[...],
                           preferred_element_type=F32).astype(o_ref.dtype)
        return
    tm, tn = xn_ref.shape[0], w_ref.shape[1]
    step = min(tn, NMM_COLS)
    for c0 in range(0, tn, step):
        r = jnp.dot(xn_ref[...], w_ref[:, c0:c0 + step], preferred_element_type=F32)
        for cl in range(step // LANE):
            c = c0 // LANE + cl
            piece = r[:, cl * LANE:(cl + 1) * LANE]
            if dil == 1:
                o_ref[0, c, 0] = piece.astype(o_ref.dtype)
            else:
                r_ref, t_ref = rest
                r_ref[c] = piece
                if dil <= NMM_MAX_STRIDE:
                    for rc in range(dil):
                        o_ref[0, c, rc] = r_ref[c, pl.ds(rc, tm // dil, stride=dil),
                                                :].astype(o_ref.dtype)
                else:
                    d1 = NMM_MAX_STRIDE
                    d2 = dil // d1
                    run = tm // d1
                    for a in range(d1):
                        t_ref[a * run:(a + 1) * run, :] = r_ref[c, pl.ds(a, run, stride=d1), :]
                    for a in range(d1):
                        for j in range(d2):
                            o_ref[0, c, d1 * j + a] = t_ref[
                                pl.ds(a * run + j, run // d2, stride=d2), :].astype(o_ref.dtype)


def _norm_matmul(h, g, w, *, out_dtype, tm, tn, slab=False, dil=1, name):
    bsz, t_len, d = h.shape
    n = w.shape[1]
    assert t_len % tm == 0 and n % tn == 0 and tn % LANE == 0 and tm % (2 * SUBLANE * dil) == 0
    scratch = [pltpu.VMEM((tm, d), BF16)]
    if slab:
        out_shape = jax.ShapeDtypeStruct((bsz, n // LANE, dil, t_len // dil, LANE), out_dtype)
        out_spec = pl.BlockSpec((1, tn // LANE, dil, tm // dil, LANE),
                                lambda b, i, j: (b, j, 0, i, 0))
        if dil > 1:
            assert dil <= NMM_MAX_STRIDE or dil % NMM_MAX_STRIDE == 0
            scratch += [pltpu.VMEM((tn // LANE, tm, LANE), F32), pltpu.VMEM((tm, LANE), F32)]
    else:
        out_shape = jax.ShapeDtypeStruct((bsz, t_len, n), out_dtype)
        out_spec = pl.BlockSpec((1, tm, tn), lambda b, i, j: (b, i, j))
    return pl.pallas_call(
        functools.partial(_nmm_body, slab=slab, dil=dil),
        out_shape=out_shape,
        grid=(bsz, t_len // tm, n // tn),
        in_specs=[
            pl.BlockSpec((1, tm, d), lambda b, i, j: (b, i, 0)),
            pl.BlockSpec((1, d), lambda b, i, j: (0, 0)),
            _const_spec((d, n)) if tn == n else pl.BlockSpec((d, tn), lambda b, i, j: (0, j)),
        ],
        out_specs=out_spec,
        scratch_shapes=scratch,
        compiler_params=_cparams(("parallel", "parallel", "arbitrary")),
        name=name,
    )(h, g.reshape(1, d), w)


def _mmr_body(x_ref, w_ref, r_ref, o_ref, *, slab):
    if slab:
        x = jnp.concatenate([x_ref[0, c] for c in range(x_ref.shape[1])], axis=-1)
    else:
        x = x_ref[0]
    o_ref[0] = r_ref[0] + jnp.dot(x, w_ref[...], preferred_element_type=F32)


def _matmul_residual(x, w, res, *, tm, slab=False, name):
    bsz, t_len, n = res.shape
    k = w.shape[0]
    if slab:
        x_spec = pl.BlockSpec((1, k // LANE, tm, LANE), lambda b, i: (b, 0, i, 0))
    else:
        x_spec = pl.BlockSpec((1, tm, k), lambda b, i: (b, i, 0))
    return pl.pallas_call(
        functools.partial(_mmr_body, slab=slab),
        out_shape=jax.ShapeDtypeStruct((bsz, t_len, n), F32),
        grid=(bsz, t_len // tm),
        in_specs=[x_spec, _const_spec((k, n)),
                  pl.BlockSpec((1, tm, n), lambda b, i: (b, i, 0))],
        out_specs=pl.BlockSpec((1, tm, n), lambda b, i: (b, i, 0)),
        compiler_params=_cparams(("parallel", "parallel")),
        name=name,
    )(x, w, res)


def _ffn_body(h_ref, hp_ref, g_ref, wup_ref, cw_ref, cb_ref, wdn_ref, gf_ref, o_ref,
              xn_ref, acc_ref, *, tm, f_hidden, final_norm):
    halo = SUBLANE
    g = g_ref[...]
    keep = (pl.program_id(1) > 0).astype(F32)
    xn_ref[0:halo, :] = (_rmsnorm_rows(hp_ref[0], g) * keep).astype(BF16)
    xn_ref[halo:, :] = _rmsnorm_rows(h_ref[0], g).astype(BF16)
    rows = tm // F_SPLIT
    xns = [xn_ref[s * rows:s * rows + rows + halo, :] for s in range(F_SPLIT)]

    def conv(u, col0):
        w = cw_ref[:, col0:col0 + F_CHUNK]
        b = cb_ref[:, col0:col0 + F_CHUNK]
        return (w[2:3] * u[halo:halo + rows] + w[1:2] * u[halo - 1:halo - 1 + rows]
                + w[0:1] * u[halo - 2:halo - 2 + rows] + b)

    def up(c):
        c0 = c * F_CHUNK
        return [(jnp.dot(xn, wup_ref[:, c0:c0 + F_CHUNK], preferred_element_type=F32),
                 jnp.dot(xn, wup_ref[:, f_hidden + c0:f_hidden + c0 + F_CHUNK],
                         preferred_element_type=F32)) for xn in xns]

    n_chunks = f_hidden // F_CHUNK
    nxt = up(0)
    accs = [None] * F_SPLIT
    for c in range(n_chunks):
        c0 = c * F_CHUNK
        cur = nxt
        if c + 1 < n_chunks:
            nxt = up(c + 1)
        for s, (ug, uv) in enumerate(cur):
            gate = conv(ug, c0)
            val = conv(uv, f_hidden + c0)
            act = (gate * _sigmoid(gate) * val).astype(BF16)
            part = jnp.dot(act, wdn_ref[c0:c0 + F_CHUNK, :], preferred_element_type=F32)
            accs[s] = part if c == 0 else accs[s] + part
    for s in range(F_SPLIT):
        acc_ref[s * rows:(s + 1) * rows, :] = accs[s]

    out = h_ref[0] + acc_ref[...]
    if final_norm:
        out = _rmsnorm_rows(out, gf_ref[...])
    o_ref[0] = out


def _conv_ffn(h, g, w_up, conv_w, conv_b, w_down, g_final, *, tm, final_norm, name):
    bsz, t_len, d = h.shape
    f2 = w_up.shape[1]
    f_hidden = f2 // 2
    assert f_hidden % F_CHUNK == 0 and t_len % tm == 0 and tm % SUBLANE == 0
    rows_per_tile = tm // SUBLANE
    return pl.pallas_call(
        functools.partial(_ffn_body, tm=tm, f_hidden=f_hidden, final_norm=final_norm),
        out_shape=jax.ShapeDtypeStruct((bsz, t_len, d), F32),
        grid=(bsz, t_len // tm),
        in_specs=[
            pl.BlockSpec((1, tm, d), lambda b, i: (b, i, 0)),
            pl.BlockSpec((1, SUBLANE, d),
                         lambda b, i: (b, jnp.maximum(i * rows_per_tile - 1, 0), 0)),
            _const_spec((1, d)),
            _const_spec((d, f2)),
            _const_spec((3, f2)),
            _const_spec((1, f2)),
            _const_spec((f_hidden, d)),
            _const_spec((1, d)),
        ],
        out_specs=pl.BlockSpec((1, tm, d), lambda b, i: (b, i, 0)),
        scratch_shapes=[pltpu.VMEM((tm + SUBLANE, d), BF16), pltpu.VMEM((tm, d), F32)],
        compiler_params=_cparams(("parallel", "parallel")),
        name=name,
    )(h, h, g.reshape(1, d), w_up, conv_w, conv_b.reshape(1, f2), w_down, g_final.reshape(1, d))


def _mlstm_body(h_ref, g_ref, win_ref, bg_ref, cw_ref, cb_ref, wq_ref, wk_ref, wv_ref,
                gn_ref, sk_ref, wout_ref, o_ref, c_ref, n_ref, m_ref, xp_ref, y_ref,
                *, chunk, dh):
    L = chunk
    halo = SUBLANE
    inner = A_HEADS * dh

    @pl.when(pl.program_id(1) == 0)
    def _():
        c_ref[...] = jnp.zeros_like(c_ref)
        n_ref[...] = jnp.zeros_like(n_ref)
        m_ref[...] = jnp.zeros_like(m_ref)
        xp_ref[...] = jnp.zeros_like(xp_ref)

    xn = _rmsnorm_rows(h_ref[0], g_ref[...]).astype(BF16)
    xm = jnp.dot(xn, win_ref[:, 0:inner], preferred_element_type=F32)
    gates = jnp.dot(xn, win_ref[:, 2 * inner:], preferred_element_type=F32) + bg_ref[...]
    heads = [dict(lo=hd * dh, hi=(hd + 1) * dh) for hd in range(A_HEADS)]
    for hd, st in enumerate(heads):
        st["v"] = jnp.dot(xm[:, st["lo"]:st["hi"]].astype(BF16), wv_ref[hd],
                          preferred_element_type=F32)
    o_pre = jnp.dot(xn, win_ref[:, inner:2 * inner], preferred_element_type=F32)

    xe = jnp.concatenate([xp_ref[...], xm], axis=0)
    cw = cw_ref[...]
    xc = (cw[3:4] * xe[halo:halo + L] + cw[2:3] * xe[halo - 1:halo - 1 + L]
          + cw[1:2] * xe[halo - 2:halo - 2 + L] + cw[0:1] * xe[halo - 3:halo - 3 + L]
          + cb_ref[...])
    xc = xc * _sigmoid(xc)
    xp_ref[...] = xm[L - halo:L]

    ii = lax.broadcasted_iota(jnp.int32, (L, L), 0)
    jj = lax.broadcasted_iota(jnp.int32, (L, L), 1)
    eye = ii == jj
    tril = jj <= ii

    def to_row(col):
        return jnp.sum(jnp.where(eye, col, 0.0), axis=0, keepdims=True)

    for hd, st in enumerate(heads):
        st["xc"] = xc[:, st["lo"]:st["hi"]]
        xcb = st["xc"].astype(BF16)
        st["q"] = jnp.dot(xcb, wq_ref[hd], preferred_element_type=F32)
        st["k"] = jnp.dot(xcb, wk_ref[hd], preferred_element_type=F32)

    for hd, st in enumerate(heads):
        li = gates[:, hd:hd + 1]
        fpre = gates[:, A_HEADS + hd:A_HEADS + hd + 1]
        lf = jnp.minimum(fpre, 0.0) - jnp.log1p(jnp.exp(-jnp.abs(fpre)))
        li_row = to_row(li)
        lf_row = to_row(lf)
        b_col = jnp.sum(jnp.where(tril, lf_row, 0.0), axis=1, keepdims=True)
        b_row = to_row(b_col)
        m_st = m_ref[hd][:, 0:1]
        log_d = jnp.where(tril, b_col - b_row + li_row, NEG)
        log_inter = b_col + m_st
        m = jnp.maximum(log_inter, jnp.max(log_d, axis=1, keepdims=True))
        st["m"] = m
        st["w_intra"] = jnp.exp(log_d - m)
        st["w_inter"] = jnp.exp(log_inter - m)
        b_last = b_col[L - 1:L, :]
        log_w = b_last - b_col + li
        m_new = jnp.maximum(b_last + m_st, jnp.max(log_w, axis=0, keepdims=True))
        st["decay"] = jnp.exp(b_last + m_st - m_new)
        st["w_k"] = jnp.exp(log_w - m_new)
        m_ref[hd] = jnp.broadcast_to(m_new, m_ref.shape[1:])

    for hd, st in enumerate(heads):
        st["k"] = st["k"] * (dh ** -0.5)
        st["qb"] = st["q"].astype(BF16)
        st["vb"] = st["v"].astype(BF16)
        st["s"] = lax.dot_general(st["qb"], st["k"].astype(BF16), (((1,), (1,)), ((), ())),
                                  preferred_element_type=F32)
        st["qc"] = jnp.dot(st["qb"], c_ref[hd].astype(BF16), preferred_element_type=F32)

    for hd, st in enumerate(heads):
        s = st["s"] * st["w_intra"]
        st["den"] = (st["w_inter"] * jnp.sum(st["q"] * n_ref[hd], axis=1, keepdims=True)
                     + jnp.sum(s, axis=1, keepdims=True))
        wk = st["w_k"] * st["k"]
        n_ref[hd] = st["decay"] * n_ref[hd] + jnp.sum(wk, axis=0, keepdims=True)
        st["sv"] = jnp.dot(s.astype(BF16), st["vb"], preferred_element_type=F32)
        st["kv"] = lax.dot_general(wk.astype(BF16), st["vb"], (((0,), (0,)), ((), ())),
                                   preferred_element_type=F32)

    for hd, st in enumerate(heads):
        lo, hi = st["lo"], st["hi"]
        c_ref[hd] = st["decay"] * c_ref[hd] + st["kv"]
        num = st["w_inter"] * st["qc"] + st["sv"]
        hh = num / jnp.maximum(jnp.abs(st["den"]), jnp.exp(-st["m"]))
        mu = jnp.mean(hh, axis=-1, keepdims=True)
        ctr = hh - mu
        var = jnp.mean(ctr * ctr, axis=-1, keepdims=True)
        hn = ctr * lax.rsqrt(var + NORM_EPS) * gn_ref[:, lo:hi]
        y = _sigmoid(o_pre[:, lo:hi]) * hn + sk_ref[:, lo:hi] * st["xc"]
        y_ref[:, lo:hi] = y.astype(y_ref.dtype)

    o_ref[0] = h_ref[0] + jnp.dot(y_ref[...], wout_ref[...], preferred_element_type=F32)


def _mlstm_layer(h, g, w_in, b_gate, conv_w, conv_b, wq, wk, wv, gn, skip, w_out):
    bsz, t_len, d = h.shape
    inner = w_out.shape[0]
    dh = inner // A_HEADS
    L = A_CHUNK
    w_a = w_in.astype(BF16)
    bg = b_gate.reshape(1, 2 * A_HEADS)
    return pl.pallas_call(
        functools.partial(_mlstm_body, chunk=L, dh=dh),
        out_shape=jax.ShapeDtypeStruct((bsz, t_len, d), F32),
        grid=(bsz, t_len // L),
        in_specs=[
            pl.BlockSpec((1, L, d), lambda b, c: (b, c, 0)),
            _const_spec((1, d)),
            _const_spec((d, 2 * inner + 2 * A_HEADS)),
            _const_spec((1, 2 * A_HEADS)),
            _const_spec((A_CONV, inner)),
            _const_spec((1, inner)),
            _const_spec((A_HEADS, dh, dh)),
            _const_spec((A_HEADS, dh, dh)),
            _const_spec((A_HEADS, dh, dh)),
            _const_spec((1, inner)),
            _const_spec((1, inner)),
            _const_spec((inner, d)),
        ],
        out_specs=pl.BlockSpec((1, L, d), lambda b, c: (b, c, 0)),
        scratch_shapes=[
            pltpu.VMEM((A_HEADS, dh, dh), F32),
            pltpu.VMEM((A_HEADS, 1, dh), F32),
            pltpu.VMEM((A_HEADS, 1, LANE), F32),
            pltpu.VMEM((SUBLANE, inner), F32),
            pltpu.VMEM((L, inner), BF16),
        ],
        compiler_params=_cparams(("parallel", "arbitrary")),
        name="mlstm_layer",
    )(h, g.reshape(1, d), w_a, bg, conv_w, conv_b.reshape(1, inner), wq.astype(BF16),
      wk.astype(BF16), wv.astype(BF16), gn.reshape(1, inner), skip.reshape(1, inner),
      w_out.astype(BF16))


def _cmp_body(x_ref, pe_ref, w1_ref, b1_ref, w2_ref, o_ref):
    half = x_ref.shape[-1]
    x = x_ref[0, 0, 0]
    pe = pe_ref[0]
    a = (x + pe[:, :half]).astype(BF16)
    b = (x + pe[:, half:]).astype(BF16)
    w1 = w1_ref[0]
    pa = jnp.dot(a, w1[:half], preferred_element_type=F32)
    pb = jnp.dot(b, w1[half:], preferred_element_type=F32)
    pb_next = jnp.concatenate([pb[1:], jnp.zeros((1, pb.shape[1]), F32)], axis=0)
    pre = pa + pb_next + b1_ref[0]
    hid = (pre * _sigmoid(pre)).astype(BF16)
    out = jnp.dot(hid, w2_ref[0], preferred_element_type=F32)
    row = lax.broadcasted_iota(jnp.int32, out.shape, 0)
    o_ref[0, 0, 0] = jnp.where(row < out.shape[0] - 1, out, 0.0).astype(o_ref.dtype)


def _nsa_compress(kv_cmp, pe, w1, b1, w2):
    two, bsz, g, t_len, dh = kv_cmp.shape
    n_half = t_len // B_CMP_STRIDE
    half = B_CMP_STRIDE * dh
    x = kv_cmp.reshape(two, bsz, g, n_half, half)
    hidden = w1.shape[-1]
    return pl.pallas_call(
        _cmp_body,
        out_shape=jax.ShapeDtypeStruct((two, bsz, g, n_half, dh), BF16),
        grid=(two, bsz, g),
        in_specs=[
            pl.BlockSpec((1, 1, 1, n_half, half), lambda s, b, gi: (s, b, gi, 0, 0)),
            pl.BlockSpec((1, 1, 2 * half), lambda s, b, gi: (s, 0, 0)),
            pl.BlockSpec((1, 2 * half, hidden), lambda s, b, gi: (s, 0, 0)),
            pl.BlockSpec((1, 1, hidden), lambda s, b, gi: (s, 0, 0)),
            pl.BlockSpec((1, hidden, dh), lambda s, b, gi: (s, 0, 0)),
        ],
        out_specs=pl.BlockSpec((1, 1, 1, n_half, dh), lambda s, b, gi: (s, b, gi, 0, 0)),
        compiler_params=_cparams(("parallel", "parallel", "parallel")),
        name="nsa_compress",
    )(x, pe.reshape(two, 1, 2 * half), w1, b1.reshape(two, 1, hidden), w2)


def _flash_t(q_ts, k_ref, vt_ref, acc_ref, m_ref, kt_lo, kt_hi, bias_fn, last_bias_fn, tk):
    acc_ref[...] = jnp.zeros_like(acc_ref)
    m_ref[...] = jnp.full_like(m_ref, NEG)
    groups = range(len(q_ts))

    def tile(kt, bias_of):
        ks = pl.multiple_of(kt * tk, tk)
        ss = [jnp.dot(k_ref[0, g, pl.ds(ks, tk), :], q_ts[g], preferred_element_type=F32)
              for g in groups]
        bias = None if bias_of is None else bias_of(kt, ks)
        ps, alphas = [], []
        for g in groups:
            s = ss[g] if bias is None else ss[g] + bias
            m = m_ref[g]
            m_new = jnp.maximum(m, jnp.max(s, axis=0, keepdims=True))
            alphas.append(jnp.exp2(m - m_new))
            ps.append(jnp.exp2(s - m_new).astype(BF16))
            m_ref[g] = m_new
        for g in groups:
            v_t = vt_ref[0, g, :, pl.ds(ks, tk)]
            acc_ref[g] = alphas[g] * acc_ref[g] + jnp.dot(v_t, ps[g],
                                                          preferred_element_type=F32)

    def body(kt, carry):
        tile(kt, bias_fn)
        return carry

    lax.fori_loop(kt_lo, kt_hi - 1, body, 0)
    tile(kt_hi - 1, last_bias_fn)
    dh = acc_ref.shape[1] - 16
    return [acc_ref[g, 0:dh, :] / acc_ref[g, dh:dh + 1, :] for g in groups]


def _nsa_body(q_ref, kc_ref, vc_ref, ks_ref, vs_ref, kw_ref, vw_ref, gp_ref, bg_ref,
              ovl_ref, wout_ref, h_ref, o_ref, acc_ref, m_ref, og_ref, *, tq, tk, t_len):
    qi = pl.program_id(1)
    t0 = qi * tq
    gw = B_QPG * B_DH
    ncp = kc_ref.shape[2]
    nsel = t_len // B_SEL_LEN
    n_top = min(B_TOPN, nsel)

    def lanes4(x):
        return jnp.concatenate([x] * B_QPG, axis=1)

    n_idx = lax.broadcasted_iota(jnp.int32, (ncp, tq), 0)
    t_col = t0 + lax.broadcasted_iota(jnp.int32, (ncp, tq), 1)
    cmask = (n_idx * B_CMP_STRIDE + (B_CMP_LEN - 1) <= t_col) & (n_idx < ncp - 1)
    cbias = lanes4(jnp.where(cmask, 0.0, NEG))
    ckeep = lanes4(cmask.astype(F32))
    blk = lax.broadcasted_iota(jnp.int32, (nsel, tq), 0)
    cur = (t0 + lax.broadcasted_iota(jnp.int32, (nsel, tq), 1)) // B_SEL_LEN
    valid = blk <= cur
    forced = ((blk == 0) | (cur - blk < B_N_LOCAL)) & valid

    q_ts, q_sels, o_cmps = [], [], []
    for g in range(B_GROUPS):
        q_full = q_ref[0, :, g * gw:(g + 1) * gw].astype(F32).T.astype(BF16)
        q_t = jnp.concatenate([q_full[r * B_DH:(r + 1) * B_DH, :] for r in range(B_QPG)], axis=1)
        q_ts.append(q_t)

        s = jnp.dot(kc_ref[0, g], q_t, preferred_element_type=F32) + cbias
        m = jnp.max(s, axis=0, keepdims=True)
        e = jnp.exp2(s - m) * ckeep
        den = jnp.sum(e, axis=0, keepdims=True)
        p = e / jnp.where(den > 0, den, 1.0)
        o_cmps.append(lax.dot_general(vc_ref[0, g], p.astype(BF16), (((0,), (0,)), ((), ())),
                                      preferred_element_type=F32))

        psum = p[:, 0:tq]
        for r in range(1, B_QPG):
            psum = psum + p[:, r * tq:(r + 1) * tq]
        p_hi = psum.astype(BF16)
        r1 = psum - p_hi.astype(F32)
        p_mid = r1.astype(BF16)
        p_lo = (r1 - p_mid.astype(F32)).astype(BF16)
        ovl = ovl_ref[...]
        imp = (jnp.dot(ovl, p_hi, preferred_element_type=F32)
               + jnp.dot(ovl, p_mid, preferred_element_type=F32)
               + jnp.dot(ovl, p_lo, preferred_element_type=F32))
        imp = imp[0:nsel]
        score = jnp.where(forced, FORCE_SCORE, jnp.where(valid, imp, -FORCE_SCORE))
        slabs = [score[r0:r0 + SUBLANE] for r0 in range(0, nsel, SUBLANE)]
        ranks = [jnp.zeros((SUBLANE, tq), F32) for _ in slabs]
        row = lax.broadcasted_iota(jnp.int32, (SUBLANE, tq), 0)
        for j in range(nsel):
            sj = score[j:j + 1, :]
            for si, sc in enumerate(slabs):
                r0 = si * SUBLANE
                ge = jnp.where(sj >= sc, 1.0, 0.0)
                gt = jnp.where(sj > sc, 1.0, 0.0)
                if r0 > j:
                    ahead = ge
                elif r0 + SUBLANE - 1 < j:
                    ahead = gt
                else:
                    ahead = jnp.where(row + r0 > j, ge, gt)
                ranks[si] = ranks[si] + ahead
        rank = jnp.concatenate(ranks, axis=0)
        sel_bias = jnp.where((rank < n_top) & valid, 0.0, NEG)
        q_sels.append(jnp.concatenate(
            [q_t, lanes4(sel_bias).astype(BF16),
             jnp.zeros((LANE - B_DH - nsel, B_QPG * tq), BF16)], axis=0))

    k_off = lax.broadcasted_iota(jnp.int32, (tk, tq), 0)
    t_pos = t0 + lax.broadcasted_iota(jnp.int32, (tk, tq), 1)

    def causal_bias(kt, ks):
        return lanes4(jnp.where(ks + k_off <= t_pos, 0.0, NEG))

    kt_hi = (t0 + tq + tk - 1) // tk
    o_sels = _flash_t(q_sels, ks_ref, vs_ref, acc_ref, m_ref, 0, kt_hi, None, causal_bias, tk)

    def win_bias(kt, ks):
        diff = t_pos - (ks + k_off)
        return lanes4(jnp.where((diff >= 0) & (diff <= B_WINDOW - 1), 0.0, NEG))

    w_lo = jnp.maximum(t0 - (B_WINDOW - 1), 0) // tk
    o_wins = _flash_t(q_ts, kw_ref, vw_ref, acc_ref, m_ref, w_lo, kt_hi, win_bias, win_bias, tk)

    for g in range(B_GROUPS):
        gate = _sigmoid(gp_ref[0, g] + bg_ref[g])
        outs = []
        for r in range(B_QPG):
            sl = slice(r * tq, (r + 1) * tq)
            outs.append(gate[r:r + 1, :] * o_cmps[g][:, sl]
                        + gate[B_QPG + r:B_QPG + r + 1, :] * o_sels[g][:, sl]
                        + gate[2 * B_QPG + r:2 * B_QPG + r + 1, :] * o_wins[g][:, sl])
        og_ref[:, g * gw:(g + 1) * gw] = jnp.concatenate(outs, axis=0).T.astype(og_ref.dtype)

    o_ref[0] = h_ref[0] + jnp.dot(og_ref[...], wout_ref[...], preferred_element_type=F32)


def _nsa_attention(q, kc, vc, k_sel, vt_sel, k_win, vt_win, gpre_t, bg, w_out, h):
    bsz, t_len, _ = q.shape
    d = h.shape[-1]
    tq, tk = B_TQ, B_TK
    ncp = kc.shape[2]
    nsel = t_len // B_SEL_LEN
    gw = B_QPG * B_DH
    c_start = jnp.arange(ncp)[None, :] * B_CMP_STRIDE
    s_start = jnp.arange(LANE)[:, None] * B_SEL_LEN
    ovl = ((c_start < s_start + B_SEL_LEN) & (c_start + B_CMP_LEN > s_start)
           & (jnp.arange(LANE)[:, None] < nsel) & (jnp.arange(ncp)[None, :] < ncp - 1)).astype(BF16)

    def k_spec(n_rows, width=B_DH):
        return pl.BlockSpec((1, B_GROUPS, n_rows, width), lambda b, i: (b, 0, 0, 0))

    vt_spec = pl.BlockSpec((1, B_GROUPS, B_DH + 16, t_len), lambda b, i: (b, 0, 0, 0))
    n_q = B_QPG * tq

    return pl.pallas_call(
        functools.partial(_nsa_body, tq=tq, tk=tk, t_len=t_len),
        out_shape=jax.ShapeDtypeStruct((bsz, t_len, d), F32),
        grid=(bsz, t_len // tq),
        in_specs=[
            pl.BlockSpec((1, tq, B_GROUPS * gw), lambda b, i: (b, i, 0)),
            k_spec(ncp), k_spec(ncp), k_spec(t_len, LANE), vt_spec, k_spec(t_len), vt_spec,
            pl.BlockSpec((1, B_GROUPS, 16, tq), lambda b, i: (b, 0, 0, i)),
            _const_spec((B_GROUPS, 16, tq)),
            _const_spec((LANE, ncp)),
            _const_spec((B_GROUPS * gw, d)),
            pl.BlockSpec((1, tq, d), lambda b, i: (b, i, 0)),
        ],
        out_specs=pl.BlockSpec((1, tq, d), lambda b, i: (b, i, 0)),
        scratch_shapes=[pltpu.VMEM((B_GROUPS, B_DH + 16, n_q), F32),
                        pltpu.VMEM((B_GROUPS, 1, n_q), F32),
                        pltpu.VMEM((tq, B_GROUPS * gw), BF16)],
        compiler_params=_cparams(("parallel", "parallel")),
        name="nsa_attention",
    )(q, kc, vc, k_sel, vt_sel, k_win, vt_win, gpre_t, bg, ovl, w_out, h)


def _nsa_in_body(x_ref, g_ref, w_ref, q_ref, kvc_ref, ks_ref, vs_ref, kw_ref, vw_ref, gp_ref):
    nq = B_HEADS * B_DH
    gw = B_GROUPS * B_DH
    xn = _rmsnorm_rows(x_ref[0], g_ref[...]).astype(BF16)
    q_ref[0] = jnp.dot(xn, w_ref[:, 0:nq], preferred_element_type=F32).astype(q_ref.dtype)

    def seg(i):
        return jnp.dot(xn, w_ref[:, nq + i * gw:nq + (i + 1) * gw], preferred_element_type=F32)

    def put_rows(ref, lead, r):
        for gi in range(B_GROUPS):
            ref[lead + (gi,)] = r[:, gi * B_DH:(gi + 1) * B_DH].astype(ref.dtype)

    def put_cols(ref, r, width, ones_rows=0):
        r_t = r.T
        tm = r.shape[0]
        for gi in range(B_GROUPS):
            ref[0, gi, 0:width, :] = r_t[gi * width:(gi + 1) * width, :].astype(ref.dtype)
            if ones_rows:
                ref[0, gi, width:width + ones_rows, :] = jnp.ones((ones_rows, tm), ref.dtype)

    put_rows(kvc_ref, (0, 0), seg(0))
    put_rows(kvc_ref, (1, 0), seg(1))
    r = seg(2)
    tm = r.shape[0]
    blk = (pl.program_id(1) * tm + lax.broadcasted_iota(jnp.int32, (tm, LANE - B_DH), 0)) // B_SEL_LEN
    onehot = (blk == lax.broadcasted_iota(jnp.int32, (tm, LANE - B_DH), 1)).astype(F32)
    for gi in range(B_GROUPS):
        ks_ref[0, gi] = jnp.concatenate([r[:, gi * B_DH:(gi + 1) * B_DH], onehot],
                                        axis=1).astype(ks_ref.dtype)
    put_cols(vs_ref, seg(3), B_DH, ones_rows=16)
    put_rows(kw_ref, (0,), seg(4))
    put_cols(vw_ref, seg(5), B_DH, ones_rows=16)
    gates = jnp.dot(xn, w_ref[:, nq + 6 * gw:], preferred_element_type=F32)
    put_cols(gp_ref, gates, 16)


def _nsa_in(h, g, w):
    bsz, t_len, d = h.shape
    tm = 512
    nq = B_HEADS * B_DH
    G = B_GROUPS
    sds = jax.ShapeDtypeStruct
    k_shape = sds((bsz, G, t_len, B_DH), BF16)
    vt_shape = sds((bsz, G, B_DH + 16, t_len), BF16)
    k_spec = pl.BlockSpec((1, G, tm, B_DH), lambda b, i: (b, 0, i, 0))
    vt_spec = pl.BlockSpec((1, G, B_DH + 16, tm), lambda b, i: (b, 0, 0, i))
    return pl.pallas_call(
        _nsa_in_body,
        out_shape=(sds((bsz, t_len, nq), BF16), sds((2, bsz, G, t_len, B_DH), F32),
                   sds((bsz, G, t_len, LANE), BF16), vt_shape, k_shape, vt_shape,
                   sds((bsz, G, 16, t_len), F32)),
        grid=(bsz, t_len // tm),
        in_specs=[pl.BlockSpec((1, tm, d), lambda b, i: (b, i, 0)),
                  _const_spec((1, d)), _const_spec(w.shape)],
        out_specs=(pl.BlockSpec((1, tm, nq), lambda b, i: (b, i, 0)),
                   pl.BlockSpec((2, 1, G, tm, B_DH), lambda b, i: (0, b, 0, i, 0)),
                   pl.BlockSpec((1, G, tm, LANE), lambda b, i: (b, 0, i, 0)), vt_spec,
                   k_spec, vt_spec,
                   pl.BlockSpec((1, G, 16, tm), lambda b, i: (b, 0, 0, i))),
        compiler_params=_cparams(("parallel", "parallel")),
        name="nsa_in",
    )(h, g.reshape(1, d), w)


def _nsa_layer(h, g, w_in, b_gate, pe_k, pe_v, phik_w1, phik_b1, phik_w2,
               phiv_w1, phiv_b1, phiv_w2, w_out):
    bsz, t_len, d = h.shape
    nq = B_HEADS * B_DH
    nkv = 6 * B_GROUPS * B_DH
    wg = w_in[:, nq + nkv:].reshape(d, B_GROUPS, B_QPG, 3).transpose(0, 1, 3, 2)
    wg = jnp.pad(wg.reshape(d, B_GROUPS, 3 * B_QPG), ((0, 0), (0, 0), (0, 16 - 3 * B_QPG)))
    wg = jnp.pad(wg.reshape(d, B_GROUPS * 16), ((0, 0), (0, LANE - B_GROUPS * 16)))
    bg = b_gate.reshape(B_GROUPS, B_QPG, 3).transpose(0, 2, 1).reshape(B_GROUPS, 3 * B_QPG)
    bg = jnp.pad(bg, ((0, 0), (0, 16 - 3 * B_QPG)))
    bg = jnp.broadcast_to(bg[:, :, None], (B_GROUPS, 16, B_TQ))
    w_b = jnp.concatenate([w_in[:, :nq] * (B_DH ** -0.5 * LOG2E), w_in[:, nq:nq + nkv], wg],
                          axis=1).astype(BF16)
    q, kv_cmp, k_sel, vt_sel, k_win, vt_win, gpre_t = _nsa_in(h, g, w_b)

    cmp = _nsa_compress(
        kv_cmp, jnp.stack([pe_k, pe_v]), jnp.stack([phik_w1, phiv_w1]).astype(BF16),
        jnp.stack([phik_b1, phiv_b1]), jnp.stack([phik_w2, phiv_w2]).astype(BF16))
    return _nsa_attention(q, cmp[0], cmp[1], k_sel, vt_sel, k_win, vt_win, gpre_t, bg,
                          w_out.astype(BF16), h)


def _dil_body(*refs, t_len):
    qkv = refs[:9]
    o_ref = refs[9]
    og_ref, lg_ref = refs[10], refs[11]
    qb = C_QBLK
    assert qb == LANE and 2 * C_DH == LANE
    top = lax.broadcasted_iota(jnp.int32, (LANE, qb), 0) < C_DH

    n_blocks = t_len // qb
    per_iter = C_BLOCKS_PER_ITER
    assert n_blocks % per_iter == 0

    def body(it, carry):
        blocks = []
        for gi, (window, dil) in enumerate(C_PATTERNS):
            sub = t_len // dil
            nb = sub // qb
            nk = qb if nb == 1 else 2 * qb
            for j in range(per_iter):
                i = it * per_iter + j
                rc = i // nb
                bq = i % nb
                bk = jnp.maximum(bq - 1, 0)
                q0 = pl.multiple_of(rc * sub + bq * qb, qb)
                k0 = pl.multiple_of(rc * sub + bk * qb, qb)
                if dil == 1:
                    rows = pl.ds(q0, qb)
                else:
                    rows = pl.ds(rc + bq * (qb * dil), qb, stride=dil)
                blocks.append(dict(gi=gi, nk=nk, k0=k0, rows=rows, shift=(bq - bk) * qb,
                                   max_dist=window // dil,
                                   q2=qkv[3 * gi][0, 0, pl.ds(q0, qb), :]))
        for b in blocks:
            q_t = b["q2"].astype(F32).T
            b["q_bd"] = jnp.concatenate([jnp.where(top, q_t, 0.0), jnp.where(top, 0.0, q_t)],
                                        axis=1).astype(BF16)
        for b in blocks:
            k2 = qkv[3 * b["gi"] + 1][0, 0, pl.ds(b["k0"], b["nk"]), :]
            b["s"] = jnp.dot(k2, b["q_bd"], preferred_element_type=F32)
        for b in blocks:
            nk = b["nk"]
            diff = (b["shift"] + lax.broadcasted_iota(jnp.int32, (nk, qb), 1)
                    - lax.broadcasted_iota(jnp.int32, (nk, qb), 0))
            bias = jnp.where((diff >= 0) & (diff <= b["max_dist"]), 0.0, NEG)
            s = b["s"] + jnp.concatenate([bias, bias], axis=1)
            m = jnp.max(s, axis=0, keepdims=True)
            e = jnp.exp2(s - m)
            den = jnp.sum(e, axis=0, keepdims=True)
            b["e"] = e.astype(BF16)
            b["inv"] = 1.0 / den
            b["lse"] = m + jnp.log2(den)
        for b in blocks:
            v2 = qkv[3 * b["gi"] + 2][0, 0, pl.ds(b["k0"], b["nk"]), :]
            b["pv"] = lax.dot_general(v2, b["e"], (((0,), (0,)), ((), ())),
                                      preferred_element_type=F32)
        for b in blocks:
            pv, inv, lse = b["pv"], b["inv"], b["lse"]
            o_t = jnp.concatenate([pv[0:C_DH, 0:qb] * inv[:, 0:qb],
                                   pv[C_DH:, qb:] * inv[:, qb:]], axis=0)
            l_t = jnp.concatenate([jnp.broadcast_to(lse[:, 0:qb], (C_DH, qb)),
                                   jnp.broadcast_to(lse[:, qb:], (C_DH, qb))], axis=0)
            og_ref[b["gi"], b["rows"], :] = o_t.T
            lg_ref[b["gi"], b["rows"], :] = l_t.T
        return carry

    lax.fori_loop(0, n_blocks // per_iter, body, 0)

    lse = lg_ref[...]
    mx = jnp.max(lse, axis=0, keepdims=True)
    w = jnp.exp2(lse - mx)
    alpha = w / jnp.sum(w, axis=0, keepdims=True)
    o_ref[0, 0] = jnp.sum(alpha * og_ref[...], axis=0).astype(o_ref.dtype)


def _dilated_attention(projs):
    bsz, _, t_len, _ = projs[0].shape
    npairs = C_HEADS * C_DH // LANE
    ng = len(C_PATTERNS)

    def spec(col):
        return pl.BlockSpec((1, 1, t_len, LANE), lambda b, hp, col=col: (b, col * npairs + hp, 0, 0))

    return pl.pallas_call(
        functools.partial(_dil_body, t_len=t_len),
        out_shape=jax.ShapeDtypeStruct((bsz, npairs, t_len, LANE), BF16),
        grid=(bsz, npairs),
        in_specs=[spec(c) for _ in range(ng) for c in range(3)],
        out_specs=pl.BlockSpec((1, 1, t_len, LANE), lambda b, hp: (b, hp, 0, 0)),
        scratch_shapes=[pltpu.VMEM((ng, t_len, LANE), F32), pltpu.VMEM((ng, t_len, LANE), F32)],
        compiler_params=_cparams(("parallel", "parallel")),
        name="dilated_attention",
    )(*[projs[gi] for gi in range(ng) for _ in range(3)])


def _dilated_layer(h, g, w_in, w_out):
    bsz, t_len, d = h.shape
    gw = C_HEADS * C_DH
    scale = jnp.ones((3, 1), F32).at[0, :].set(C_DH ** -0.5 * LOG2E)
    w_c = w_in.reshape(d, len(C_PATTERNS), 3, gw)
    projs = []
    for gi, (_, dil) in enumerate(C_PATTERNS):
        w_g = (w_c[:, gi] * scale).reshape(d, 3 * gw).astype(BF16)
        p = _norm_matmul(h, g, w_g, out_dtype=BF16, tm=512, tn=3 * gw, slab=True, dil=dil,
                         name=f"dilated_in{gi}")
        projs.append(p.reshape(bsz, 3 * gw // LANE, t_len, LANE))
    o = _dilated_attention(projs)
    return _matmul_residual(o, w_out.astype(BF16), h, tm=512, slab=True, name="dilated_out")


def kernel(x, norm_mix, norm_ffn, norm_final, a_w_in, a_b_gate, a_conv_w, a_conv_b, a_wq, a_wk, a_wv, a_gn, a_skip, a_w_out, b_w_in, b_b_gate, b_pe_k, b_pe_v, b_phik_w1, b_phik_b1, b_phik_w2, b_phiv_w1, b_phiv_b1, b_phiv_w2, b_w_out, c_w_in, c_w_out, f_w_up, f_conv_w, f_conv_b, f_w_down):
    depth = norm_mix.shape[0]
    h = x
    for i in range(depth):
        kind, j = i % 3, i // 3
        if kind == 0:
            h = _mlstm_layer(h, norm_mix[i], a_w_in[j], a_b_gate[j], a_conv_w[j], a_conv_b[j],
                             a_wq[j], a_wk[j], a_wv[j], a_gn[j], a_skip[j], a_w_out[j])
        elif kind == 1:
            h = _nsa_layer(h, norm_mix[i], b_w_in[j], b_b_gate[j], b_pe_k[j], b_pe_v[j],
                           b_phik_w1[j], b_phik_b1[j], b_phik_w2[j],
                           b_phiv_w1[j], b_phiv_b1[j], b_phiv_w2[j], b_w_out[j])
        else:
            h = _dilated_layer(h, norm_mix[i], c_w_in[j], c_w_out[j])
        h = _conv_ffn(h, norm_ffn[i], f_w_up[i].astype(BF16), f_conv_w[i], f_conv_b[i],
                      f_w_down[i].astype(BF16), norm_final, tm=512,
                      final_norm=(i == depth - 1), name="conv_ffn")
    return h
```

```python
import functools

import jax
import jax.numpy as jnp
from jax import lax
from jax.experimental import pallas as pl
from jax.experimental.pallas import tpu as pltpu

F32 = jnp.float32
BF16 = jnp.bfloat16

NORM_EPS = 1e-6
NEG = -1e30
LOG2E = 1.4426950408889634
LANE = 128
SUBLANE = 8
VMEM_LIMIT = 56 * 1024 * 1024
NMM_COLS = 512
NMM_MAX_STRIDE = 4

A_HEADS = 4
A_CHUNK = 256
A_CONV = 4
B_HEADS = 16
B_DH = 64
B_GROUPS = 4
B_QPG = 4
B_CMP_LEN = 32
B_CMP_STRIDE = 16
B_SEL_LEN = 64
B_TOPN = 16
B_N_LOCAL = 2
B_WINDOW = 512
FORCE_SCORE = 1e9
B_TQ = 256
B_TK = 256
C_PATTERNS = ((128, 1), (512, 4), (2048, 16))
C_HEADS = 16
C_DH = 64
C_QBLK = 128
C_BLOCKS_PER_ITER = 8
F_CHUNK = 256
F_SPLIT = 2


def _cparams(sem):
    return pltpu.CompilerParams(dimension_semantics=sem, vmem_limit_bytes=VMEM_LIMIT)


def _const_spec(shape):
    nd = len(shape)
    return pl.BlockSpec(shape, lambda *_: (0,) * nd, pipeline_mode=pl.Buffered(1))


def _rmsnorm_rows(x, g):
    ms = jnp.mean(x * x, axis=-1, keepdims=True)
    return x * lax.rsqrt(ms + NORM_EPS) * g


def _sigmoid(x):
    return 1.0 / (1.0 + jnp.exp(-x))


def _nmm_body(x_ref, g_ref, w_ref, o_ref, xn_ref, *rest, slab, dil):
    @pl.when(pl.program_id(2) == 0)
    def _():
        xn_ref[...] = _rmsnorm_rows(x_ref[0], g_ref[...]).astype(BF16)

    if not slab:
        o_ref[0] = jnp.dot(xn_ref[...], w_ref[...],
                           preferred_element_type=F32).astype(o_ref.dtype)
        return
    tm, tn = xn_ref.shape[0], w_ref.shape[1]
    step = min(tn, NMM_COLS)
    for c0 in range(0, tn, step):
        r = jnp.dot(xn_ref[...], w_ref[:, c0:c0 + step], preferred_element_type=F32)
        for cl in range(step // LANE):
            c = c0 // LANE + cl
            piece = r[:, cl * LANE:(cl + 1) * LANE]
            if dil == 1:
                o_ref[0, c, 0] = piece.astype(o_ref.dtype)
            else:
                r_ref, t_ref = rest
                r_ref[c] = piece
                if dil <= NMM_MAX_STRIDE:
                    for rc in range(dil):
                        o_ref[0, c, rc] = r_ref[c, pl.ds(rc, tm // dil, stride=dil),
                                                :].astype(o_ref.dtype)
                else:
                    d1 = NMM_MAX_STRIDE
                    d2 = dil // d1
                    run = tm // d1
                    for a in range(d1):
                        t_ref[a * run:(a + 1) * run, :] = r_ref[c, pl.ds(a, run, stride=d1), :]
                    for a in range(d1):
                        for j in range(d2):
                            o_ref[0, c, d1 * j + a] = t_ref[
                                pl.ds(a * run + j, run // d2, stride=d2), :].astype(o_ref.dtype)


def _norm_matmul(h, g, w, *, out_dtype, tm, tn, slab=False, dil=1, name):
    bsz, t_len, d = h.shape
    n = w.shape[1]
    assert t_len % tm == 0 and n % tn == 0 and tn % LANE == 0 and tm % (2 * SUBLANE * dil) == 0
    scratch = [pltpu.VMEM((tm, d), BF16)]
    if slab:
        out_shape = jax.ShapeDtypeStruct((bsz, n // LANE, dil, t_len // dil, LANE), out_dtype)
        out_spec = pl.BlockSpec((1, tn // LANE, dil, tm // dil, LANE),
                                lambda b, i, j: (b, j, 0, i, 0))
        if dil > 1:
            assert dil <= NMM_MAX_STRIDE or dil % NMM_MAX_STRIDE == 0
            scratch += [pltpu.VMEM((tn // LANE, tm, LANE), F32), pltpu.VMEM((tm, LANE), F32)]
    else:
        out_shape = jax.ShapeDtypeStruct((bsz, t_len, n), out_dtype)
        out_spec = pl.BlockSpec((1, tm, tn), lambda b, i, j: (b, i, j))
    return pl.pallas_call(
        functools.partial(_nmm_body, slab=slab, dil=dil),
        out_shape=out_shape,
        grid=(bsz, t_len // tm, n // tn),
        in_specs=[
            pl.BlockSpec((1, tm, d), lambda b, i, j: (b, i, 0)),
            pl.BlockSpec((1, d), lambda b, i, j: (0, 0)),
            _const_spec((d, n)) if tn == n else pl.BlockSpec((d, tn), lambda b, i, j: (0, j)),
        ],
        out_specs=out_spec,
        scratch_shapes=scratch,
        compiler_params=_cparams(("parallel", "parallel", "arbitrary")),
        name=name,
    )(h, g.reshape(1, d), w)


def _ffn_body(h_ref, hp_ref, g_ref, wup_ref, cw_ref, cb_ref, wdn_ref, gf_ref, *rest,
              tm, f_hidden, final_norm, mixed):
    halo = SUBLANE
    g = g_ref[...]
    if mixed:
        ox_ref, oxp_ref, wmix_ref, o_ref, xn_ref, acc_ref, hres_ref = rest
        slabs = range(ox_ref.shape[1])
        o_cur = jnp.concatenate([ox_ref[0, c] for c in slabs], axis=-1)
        o_prev = jnp.concatenate([oxp_ref[0, c] for c in slabs], axis=-1)
        hres_ref[...] = h_ref[0] + jnp.dot(o_cur, wmix_ref[...], preferred_element_type=F32)
        h_cur = hres_ref[...]
        h_prev = hp_ref[0] + jnp.dot(o_prev, wmix_ref[...],
                                     preferred_element_type=F32)[halo:2 * halo]
    else:
        o_ref, xn_ref, acc_ref = rest
        h_cur, h_prev = h_ref[0], hp_ref[0]
    keep = (pl.program_id(1) > 0).astype(F32)
    xn_ref[0:halo, :] = (_rmsnorm_rows(h_prev, g) * keep).astype(BF16)
    xn_ref[halo:, :] = _rmsnorm_rows(h_cur, g).astype(BF16)
    rows = tm // F_SPLIT
    xns = [xn_ref[s * rows:s * rows + rows + halo, :] for s in range(F_SPLIT)]

    def conv(u, col0):
        w = cw_ref[:, col0:col0 + F_CHUNK]
        b = cb_ref[:, col0:col0 + F_CHUNK]
        return (w[2:3] * u[halo:halo + rows] + w[1:2] * u[halo - 1:halo - 1 + rows]
                + w[0:1] * u[halo - 2:halo - 2 + rows] + b)

    def up(c):
        c0 = c * F_CHUNK
        return [(jnp.dot(xn, wup_ref[:, c0:c0 + F_CHUNK], preferred_element_type=F32),
                 jnp.dot(xn, wup_ref[:, f_hidden + c0:f_hidden + c0 + F_CHUNK],
                         preferred_element_type=F32)) for xn in xns]

    n_chunks = f_hidden // F_CHUNK
    nxt = up(0)
    accs = [None] * F_SPLIT
    for c in range(n_chunks):
        c0 = c * F_CHUNK
        cur = nxt
        if c + 1 < n_chunks:
            nxt = up(c + 1)
        for s, (ug, uv) in enumerate(cur):
            gate = conv(ug, c0)
            val = conv(uv, f_hidden + c0)
            act = (gate * _sigmoid(gate) * val).astype(BF16)
            part = jnp.dot(act, wdn_ref[c0:c0 + F_CHUNK, :], preferred_element_type=F32)
            accs[s] = part if c == 0 else accs[s] + part
    for s in range(F_SPLIT):
        acc_ref[s * rows:(s + 1) * rows, :] = accs[s]

    out = (hres_ref[...] if mixed else h_ref[0]) + acc_ref[...]
    if final_norm:
        out = _rmsnorm_rows(out, gf_ref[...])
    o_ref[0] = out


def _conv_ffn(h, g, w_up, conv_w, conv_b, w_down, g_final, *, tm, final_norm, name, mix=None):
    bsz, t_len, d = h.shape
    f2 = w_up.shape[1]
    f_hidden = f2 // 2
    assert f_hidden % F_CHUNK == 0 and t_len % tm == 0 and tm % (2 * SUBLANE) == 0
    rows_per_tile = tm // SUBLANE
    in_specs = [
        pl.BlockSpec((1, tm, d), lambda b, i: (b, i, 0)),
        pl.BlockSpec((1, SUBLANE, d), lambda b, i: (b, jnp.maximum(i * rows_per_tile - 1, 0), 0)),
        _const_spec((1, d)),
        _const_spec((d, f2)),
        _const_spec((3, f2)),
        _const_spec((1, f2)),
        _const_spec((f_hidden, d)),
        _const_spec((1, d)),
    ]
    args = [h, h, g.reshape(1, d), w_up, conv_w, conv_b.reshape(1, f2), w_down,
            g_final.reshape(1, d)]
    scratch = [pltpu.VMEM((tm + SUBLANE, d), BF16), pltpu.VMEM((tm, d), F32)]
    if mix is not None:
        o, w_mix = mix
        n_slab = o.shape[1]
        in_specs += [
            pl.BlockSpec((1, n_slab, tm, LANE), lambda b, i: (b, 0, i, 0)),
            pl.BlockSpec((1, n_slab, 2 * SUBLANE, LANE),
                         lambda b, i: (b, 0, jnp.maximum(i * (rows_per_tile // 2) - 1, 0), 0)),
            _const_spec(w_mix.shape),
        ]
        args += [o, o, w_mix]
        scratch.append(pltpu.VMEM((tm, d), F32))
    return pl.pallas_call(
        functools.partial(_ffn_body, tm=tm, f_hidden=f_hidden, final_norm=final_norm,
                          mixed=mix is not None),
        out_shape=jax.ShapeDtypeStruct((bsz, t_len, d), F32),
        grid=(bsz, t_len // tm),
        in_specs=in_specs,
        out_specs=pl.BlockSpec((1, tm, d), lambda b, i: (b, i, 0)),
        scratch_shapes=scratch,
        compiler_params=_cparams(("parallel", "parallel")),
        name=name,
    )(*args)


def _mlstm_body(h_ref, g_ref, win_ref, wgate_ref, bg_ref, cw_ref, cb_ref, wq_ref, wk_ref, wv_ref,
                gn_ref, sk_ref, wout_ref, o_ref, c_ref, n_ref, m_ref, xp_ref, y_ref,
                *, chunk, dh):
    L = chunk
    halo = SUBLANE
    inner = A_HEADS * dh

    @pl.when(pl.program_id(1) == 0)
    def _():
        c_ref[...] = jnp.zeros_like(c_ref)
        n_ref[...] = jnp.zeros_like(n_ref)
        m_ref[...] = jnp.zeros_like(m_ref)
        xp_ref[...] = jnp.zeros_like(xp_ref)

    xn = _rmsnorm_rows(h_ref[0], g_ref[...]).astype(BF16)
    xm = jnp.dot(xn, win_ref[:, 0:inner], preferred_element_type=F32)
    gates = jnp.dot(xn, wgate_ref[...], preferred_element_type=F32) + bg_ref[...]
    heads = [dict(lo=hd * dh, hi=(hd + 1) * dh) for hd in range(A_HEADS)]
    for hd, st in enumerate(heads):
        st["v"] = jnp.dot(xm[:, st["lo"]:st["hi"]].astype(BF16), wv_ref[hd],
                          preferred_element_type=F32)
    o_pre = jnp.dot(xn, win_ref[:, inner:2 * inner], preferred_element_type=F32)

    xe = jnp.concatenate([xp_ref[...], xm], axis=0)
    cw = cw_ref[...]
    xc = (cw[3:4] * xe[halo:halo + L] + cw[2:3] * xe[halo - 1:halo - 1 + L]
          + cw[1:2] * xe[halo - 2:halo - 2 + L] + cw[0:1] * xe[halo - 3:halo - 3 + L]
          + cb_ref[...])
    xc = xc * _sigmoid(xc)
    xp_ref[...] = xm[L - halo:L]

    ii = lax.broadcasted_iota(jnp.int32, (L, L), 0)
    jj = lax.broadcasted_iota(jnp.int32, (L, L), 1)
    eye = ii == jj
    tril = jj <= ii

    def to_row(col):
        return jnp.sum(jnp.where(eye, col, 0.0), axis=0, keepdims=True)

    for hd, st in enumerate(heads):
        st["xc"] = xc[:, st["lo"]:st["hi"]]
        xcb = st["xc"].astype(BF16)
        st["q"] = jnp.dot(xcb, wq_ref[hd], preferred_element_type=F32)
        st["k"] = jnp.dot(xcb, wk_ref[hd], preferred_element_type=F32)

    for hd, st in enumerate(heads):
        li = gates[:, hd:hd + 1]
        fpre = gates[:, A_HEADS + hd:A_HEADS + hd + 1]
        lf = jnp.minimum(fpre, 0.0) - jnp.log1p(jnp.exp(-jnp.abs(fpre)))
        li_row = to_row(li)
        lf_row = to_row(lf)
        b_col = jnp.sum(jnp.where(tril, lf_row, 0.0), axis=1, keepdims=True)
        b_row = to_row(b_col)
        m_st = m_ref[hd][:, 0:1]
        log_d = jnp.where(tril, b_col - b_row + li_row, NEG)
        log_inter = b_col + m_st
        m = jnp.maximum(log_inter, jnp.max(log_d, axis=1, keepdims=True))
        st["m"] = m
        st["w_intra"] = jnp.exp(log_d - m)
        st["w_inter"] = jnp.exp(log_inter - m)
        b_last = b_col[L - 1:L, :]
        log_w = b_last - b_col + li
        m_new = jnp.maximum(b_last + m_st, jnp.max(log_w, axis=0, keepdims=True))
        st["decay"] = jnp.exp(b_last + m_st - m_new)
        st["w_k"] = jnp.exp(log_w - m_new)
        m_ref[hd] = jnp.broadcast_to(m_new, m_ref.shape[1:])

    for hd, st in enumerate(heads):
        st["k"] = st["k"] * (dh ** -0.5)
        st["qb"] = st["q"].astype(BF16)
        st["vb"] = st["v"].astype(BF16)
        st["s"] = lax.dot_general(st["qb"], st["k"].astype(BF16), (((1,), (1,)), ((), ())),
                                  preferred_element_type=F32)
        st["qc"] = jnp.dot(st["qb"], c_ref[hd].astype(BF16), preferred_element_type=F32)

    for hd, st in enumerate(heads):
        s = st["s"] * st["w_intra"]
        st["den"] = (st["w_inter"] * jnp.sum(st["q"] * n_ref[hd], axis=1, keepdims=True)
                     + jnp.sum(s, axis=1, keepdims=True))
        wk = st["w_k"] * st["k"]
        n_ref[hd] = st["decay"] * n_ref[hd] + jnp.sum(wk, axis=0, keepdims=True)
        st["sv"] = jnp.dot(s.astype(BF16), st["vb"], preferred_element_type=F32)
        st["kv"] = lax.dot_general(wk.astype(BF16), st["vb"], (((0,), (0,)), ((), ())),
                                   preferred_element_type=F32)

    for hd, st in enumerate(heads):
        lo, hi = st["lo"], st["hi"]
        c_ref[hd] = st["decay"] * c_ref[hd] + st["kv"]
        num = st["w_inter"] * st["qc"] + st["sv"]
        hh = num / jnp.maximum(jnp.abs(st["den"]), jnp.exp(-st["m"]))
        mu = jnp.mean(hh, axis=-1, keepdims=True)
        ctr = hh - mu
        var = jnp.mean(ctr * ctr, axis=-1, keepdims=True)
        hn = ctr * lax.rsqrt(var + NORM_EPS) * gn_ref[:, lo:hi]
        y = _sigmoid(o_pre[:, lo:hi]) * hn + sk_ref[:, lo:hi] * st["xc"]
        y_ref[:, lo:hi] = y.astype(y_ref.dtype)

    o_ref[0] = h_ref[0] + jnp.dot(y_ref[...], wout_ref[...], preferred_element_type=F32)


def _mlstm_layer(h, g, w_in, b_gate, conv_w, conv_b, wq, wk, wv, gn, skip, w_out):
    bsz, t_len, d = h.shape
    inner = w_out.shape[0]
    dh = inner // A_HEADS
    L = A_CHUNK
    w_a = w_in[:, :2 * inner].astype(BF16)
    w_gate = w_in[:, 2 * inner:].astype(BF16)
    bg = b_gate.reshape(1, 2 * A_HEADS)
    return pl.pallas_call(
        functools.partial(_mlstm_body, chunk=L, dh=dh),
        out_shape=jax.ShapeDtypeStruct((bsz, t_len, d), F32),
        grid=(bsz, t_len // L),
        in_specs=[
            pl.BlockSpec((1, L, d), lambda b, c: (b, c, 0)),
            _const_spec((1, d)),
            _const_spec((d, 2 * inner)),
            _const_spec((d, 2 * A_HEADS)),
            _const_spec((1, 2 * A_HEADS)),
            _const_spec((A_CONV, inner)),
            _const_spec((1, inner)),
            _const_spec((A_HEADS, dh, dh)),
            _const_spec((A_HEADS, dh, dh)),
            _const_spec((A_HEADS, dh, dh)),
            _const_spec((1, inner)),
            _const_spec((1, inner)),
            _const_spec((inner, d)),
        ],
        out_specs=pl.BlockSpec((1, L, d), lambda b, c: (b, c, 0)),
        scratch_shapes=[
            pltpu.VMEM((A_HEADS, dh, dh), F32),
            pltpu.VMEM((A_HEADS, 1, dh), F32),
            pltpu.VMEM((A_HEADS, 1, LANE), F32),
            pltpu.VMEM((SUBLANE, inner), F32),
            pltpu.VMEM((L, inner), BF16),
        ],
        compiler_params=_cparams(("parallel", "arbitrary")),
        name="mlstm_layer",
    )(h, g.reshape(1, d), w_a, w_gate, bg, conv_w, conv_b.reshape(1, inner), wq.astype(BF16),
      wk.astype(BF16), wv.astype(BF16), gn.reshape(1, inner), skip.reshape(1, inner),
      w_out.astype(BF16))


def _cmp_body(x_ref, pe_ref, w1_ref, b1_ref, w2_ref, o_ref):
    half = x_ref.shape[-1]
    x = x_ref[0, 0, 0]
    pe = pe_ref[0]
    a = (x + pe[:, :half]).astype(BF16)
    b = (x + pe[:, half:]).astype(BF16)
    w1 = w1_ref[0]
    pa = jnp.dot(a, w1[:half], preferred_element_type=F32)
    pb = jnp.dot(b, w1[half:], preferred_element_type=F32)
    pb_next = jnp.concatenate([pb[1:], jnp.zeros((1, pb.shape[1]), F32)], axis=0)
    pre = pa + pb_next + b1_ref[0]
    hid = (pre * _sigmoid(pre)).astype(BF16)
    out = jnp.dot(hid, w2_ref[0], preferred_element_type=F32)
    row = lax.broadcasted_iota(jnp.int32, out.shape, 0)
    o_ref[0, 0, 0] = jnp.where(row < out.shape[0] - 1, out, 0.0).astype(o_ref.dtype)


def _nsa_compress(kv_cmp, pe, w1, b1, w2):
    two, bsz, g, t_len, dh = kv_cmp.shape
    n_half = t_len // B_CMP_STRIDE
    half = B_CMP_STRIDE * dh
    x = kv_cmp.reshape(two, bsz, g, n_half, half)
    hidden = w1.shape[-1]
    return pl.pallas_call(
        _cmp_body,
        out_shape=jax.ShapeDtypeStruct((two, bsz, g, n_half, dh), BF16),
        grid=(two, bsz, g),
        in_specs=[
            pl.BlockSpec((1, 1, 1, n_half, half), lambda s, b, gi: (s, b, gi, 0, 0)),
            pl.BlockSpec((1, 1, 2 * half), lambda s, b, gi: (s, 0, 0)),
            pl.BlockSpec((1, 2 * half, hidden), lambda s, b, gi: (s, 0, 0)),
            pl.BlockSpec((1, 1, hidden), lambda s, b, gi: (s, 0, 0)),
            pl.BlockSpec((1, hidden, dh), lambda s, b, gi: (s, 0, 0)),
        ],
        out_specs=pl.BlockSpec((1, 1, 1, n_half, dh), lambda s, b, gi: (s, b, gi, 0, 0)),
        compiler_params=_cparams(("parallel", "parallel", "parallel")),
        name="nsa_compress",
    )(x, pe.reshape(two, 1, 2 * half), w1, b1.reshape(two, 1, hidden), w2)


def _flash_t(q_ts, k_ref, vt_ref, acc_ref, m_ref, kt_lo, kt_hi, bias_fn, last_bias_fn, tk):
    acc_ref[...] = jnp.zeros_like(acc_ref)
    m_ref[...] = jnp.full_like(m_ref, NEG)
    groups = range(len(q_ts))

    def tile(kt, bias_of):
        ks = pl.multiple_of(kt * tk, tk)
        ss = [jnp.dot(k_ref[0, g, pl.ds(ks, tk), :], q_ts[g], preferred_element_type=F32)
              for g in groups]
        bias = None if bias_of is None else bias_of(kt, ks)
        ps, alphas = [], []
        for g in groups:
            s = ss[g] if bias is None else ss[g] + bias
            m = m_ref[g]
            m_new = jnp.maximum(m, jnp.max(s, axis=0, keepdims=True))
            alphas.append(jnp.exp2(m - m_new))
            ps.append(jnp.exp2(s - m_new).astype(BF16))
            m_ref[g] = m_new
        for g in groups:
            v_t = vt_ref[0, g, :, pl.ds(ks, tk)]
            acc_ref[g] = alphas[g] * acc_ref[g] + jnp.dot(v_t, ps[g],
                                                          preferred_element_type=F32)

    def body(kt, carry):
        tile(kt, bias_fn)
        return carry

    lax.fori_loop(kt_lo, kt_hi - 1, body, 0)
    tile(kt_hi - 1, last_bias_fn)
    dh = acc_ref.shape[1] - 16
    return [acc_ref[g, 0:dh, :] / acc_ref[g, dh:dh + 1, :] for g in groups]


def _nsa_body(q_ref, kc_ref, vc_ref, ks_ref, vs_ref, kw_ref, vw_ref, gp_ref, bg_ref,
              ovl_ref, wout_ref, h_ref, o_ref, acc_ref, m_ref, og_ref, *, tq, tk, t_len):
    qi = pl.program_id(1)
    t0 = qi * tq
    gw = B_QPG * B_DH
    ncp = kc_ref.shape[2]
    nsel = t_len // B_SEL_LEN
    n_top = min(B_TOPN, nsel)

    def lanes4(x):
        return jnp.concatenate([x] * B_QPG, axis=1)

    n_idx = lax.broadcasted_iota(jnp.int32, (ncp, tq), 0)
    t_col = t0 + lax.broadcasted_iota(jnp.int32, (ncp, tq), 1)
    cmask = (n_idx * B_CMP_STRIDE + (B_CMP_LEN - 1) <= t_col) & (n_idx < ncp - 1)
    cbias = lanes4(jnp.where(cmask, 0.0, NEG))
    ckeep = lanes4(cmask.astype(F32))
    blk = lax.broadcasted_iota(jnp.int32, (nsel, tq), 0)
    cur = (t0 + lax.broadcasted_iota(jnp.int32, (nsel, tq), 1)) // B_SEL_LEN
    valid = blk <= cur
    forced = ((blk == 0) | (cur - blk < B_N_LOCAL)) & valid

    q_ts, q_sels, o_cmps = [], [], []
    for g in range(B_GROUPS):
        q_full = q_ref[0, :, g * gw:(g + 1) * gw].astype(F32).T.astype(BF16)
        q_t = jnp.concatenate([q_full[r * B_DH:(r + 1) * B_DH, :] for r in range(B_QPG)], axis=1)
        q_ts.append(q_t)

        s = jnp.dot(kc_ref[0, g], q_t, preferred_element_type=F32) + cbias
        m = jnp.max(s, axis=0, keepdims=True)
        e = jnp.exp2(s - m) * ckeep
        den = jnp.sum(e, axis=0, keepdims=True)
        p = e / jnp.where(den > 0, den, 1.0)
        o_cmps.append(lax.dot_general(vc_ref[0, g], p.astype(BF16), (((0,), (0,)), ((), ())),
                                      preferred_element_type=F32))

        psum = p[:, 0:tq]
        for r in range(1, B_QPG):
            psum = psum + p[:, r * tq:(r + 1) * tq]
        p_hi = psum.astype(BF16)
        r1 = psum - p_hi.astype(F32)
        p_mid = r1.astype(BF16)
        p_lo = (r1 - p_mid.astype(F32)).astype(BF16)
        ovl = ovl_ref[...]
        imp = (jnp.dot(ovl, p_hi, preferred_element_type=F32)
               + jnp.dot(ovl, p_mid, preferred_element_type=F32)
               + jnp.dot(ovl, p_lo, preferred_element_type=F32))
        imp = imp[0:nsel]
        score = jnp.where(forced, FORCE_SCORE, jnp.where(valid, imp, -FORCE_SCORE))
        slabs = [score[r0:r0 + SUBLANE] for r0 in range(0, nsel, SUBLANE)]
        ranks = [jnp.zeros((SUBLANE, tq), F32) for _ in slabs]
        row = lax.broadcasted_iota(jnp.int32, (SUBLANE, tq), 0)
        for j in range(nsel):
            sj = score[j:j + 1, :]
            for si, sc in enumerate(slabs):
                r0 = si * SUBLANE
                ge = jnp.where(sj >= sc, 1.0, 0.0)
                gt = jnp.where(sj > sc, 1.0, 0.0)
                if r0 > j:
                    ahead = ge
                elif r0 + SUBLANE - 1 < j:
                    ahead = gt
                else:
                    ahead = jnp.where(row + r0 > j, ge, gt)
                ranks[si] = ranks[si] + ahead
        rank = jnp.concatenate(ranks, axis=0)
        sel_bias = jnp.where((rank < n_top) & valid, 0.0, NEG)
        q_sels.append(jnp.concatenate(
            [q_t, lanes4(sel_bias).astype(BF16),
             jnp.zeros((LANE - B_DH - nsel, B_QPG * tq), BF16)], axis=0))

    k_off = lax.broadcasted_iota(jnp.int32, (tk, tq), 0)
    t_pos = t0 + lax.broadcasted_iota(jnp.int32, (tk, tq), 1)

    def causal_bias(kt, ks):
        return lanes4(jnp.where(ks + k_off <= t_pos, 0.0, NEG))

    kt_hi = (t0 + tq + tk - 1) // tk
    o_sels = _flash_t(q_sels, ks_ref, vs_ref, acc_ref, m_ref, 0, kt_hi, None, causal_bias, tk)

    def win_bias(kt, ks):
        diff = t_pos - (ks + k_off)
        return lanes4(jnp.where((diff >= 0) & (diff <= B_WINDOW - 1), 0.0, NEG))

    w_lo = jnp.maximum(t0 - (B_WINDOW - 1), 0) // tk
    o_wins = _flash_t(q_ts, kw_ref, vw_ref, acc_ref, m_ref, w_lo, kt_hi, win_bias, win_bias, tk)

    for g in range(B_GROUPS):
        gate = _sigmoid(gp_ref[0, g] + bg_ref[g])
        outs = []
        for r in range(B_QPG):
            sl = slice(r * tq, (r + 1) * tq)
            outs.append(gate[r:r + 1, :] * o_cmps[g][:, sl]
                        + gate[B_QPG + r:B_QPG + r + 1, :] * o_sels[g][:, sl]
                        + gate[2 * B_QPG + r:2 * B_QPG + r + 1, :] * o_wins[g][:, sl])
        og_ref[:, g * gw:(g + 1) * gw] = jnp.concatenate(outs, axis=0).T.astype(og_ref.dtype)

    o_ref[0] = h_ref[0] + jnp.dot(og_ref[...], wout_ref[...], preferred_element_type=F32)


def _nsa_attention(q, kc, vc, k_sel, vt_sel, k_win, vt_win, gpre_t, bg, w_out, h):
    bsz, t_len, _ = q.shape
    d = h.shape[-1]
    tq, tk = B_TQ, B_TK
    ncp = kc.shape[2]
    nsel = t_len // B_SEL_LEN
    gw = B_QPG * B_DH
    c_start = jnp.arange(ncp)[None, :] * B_CMP_STRIDE
    s_start = jnp.arange(LANE)[:, None] * B_SEL_LEN
    ovl = ((c_start < s_start + B_SEL_LEN) & (c_start + B_CMP_LEN > s_start)
           & (jnp.arange(LANE)[:, None] < nsel) & (jnp.arange(ncp)[None, :] < ncp - 1)).astype(BF16)

    def k_spec(n_rows, width=B_DH):
        return pl.BlockSpec((1, B_GROUPS, n_rows, width), lambda b, i: (b, 0, 0, 0))

    vt_spec = pl.BlockSpec((1, B_GROUPS, B_DH + 16, t_len), lambda b, i: (b, 0, 0, 0))
    n_q = B_QPG * tq

    return pl.pallas_call(
        functools.partial(_nsa_body, tq=tq, tk=tk, t_len=t_len),
        out_shape=jax.ShapeDtypeStruct((bsz, t_len, d), F32),
        grid=(bsz, t_len // tq),
        in_specs=[
            pl.BlockSpec((1, tq, B_GROUPS * gw), lambda b, i: (b, i, 0)),
            k_spec(ncp), k_spec(ncp), k_spec(t_len, LANE), vt_spec, k_spec(t_len), vt_spec,
            pl.BlockSpec((1, B_GROUPS, 16, tq), lambda b, i: (b, 0, 0, i)),
            _const_spec((B_GROUPS, 16, tq)),
            _const_spec((LANE, ncp)),
            _const_spec((B_GROUPS * gw, d)),
            pl.BlockSpec((1, tq, d), lambda b, i: (b, i, 0)),
        ],
        out_specs=pl.BlockSpec((1, tq, d), lambda b, i: (b, i, 0)),
        scratch_shapes=[pltpu.VMEM((B_GROUPS, B_DH + 16, n_q), F32),
                        pltpu.VMEM((B_GROUPS, 1, n_q), F32),
                        pltpu.VMEM((tq, B_GROUPS * gw), BF16)],
        compiler_params=_cparams(("parallel", "parallel")),
        name="nsa_attention",
    )(q, kc, vc, k_sel, vt_sel, k_win, vt_win, gpre_t, bg, ovl, w_out, h)


def _nsa_in_body(x_ref, g_ref, w_ref, q_ref, kvc_ref, ks_ref, vs_ref, kw_ref, vw_ref, gp_ref):
    nq = B_HEADS * B_DH
    gw = B_GROUPS * B_DH
    xn = _rmsnorm_rows(x_ref[0], g_ref[...]).astype(BF16)
    q_ref[0] = jnp.dot(xn, w_ref[:, 0:nq], preferred_element_type=F32).astype(q_ref.dtype)

    def seg(i):
        return jnp.dot(xn, w_ref[:, nq + i * gw:nq + (i + 1) * gw], preferred_element_type=F32)

    def put_rows(ref, lead, r):
        for gi in range(B_GROUPS):
            ref[lead + (gi,)] = r[:, gi * B_DH:(gi + 1) * B_DH].astype(ref.dtype)

    def put_cols(ref, r, width, ones_rows=0):
        r_t = r.T
        tm = r.shape[0]
        for gi in range(B_GROUPS):
            ref[0, gi, 0:width, :] = r_t[gi * width:(gi + 1) * width, :].astype(ref.dtype)
            if ones_rows:
                ref[0, gi, width:width + ones_rows, :] = jnp.ones((ones_rows, tm), ref.dtype)

    put_rows(kvc_ref, (0, 0), seg(0))
    put_rows(kvc_ref, (1, 0), seg(1))
    r = seg(2)
    tm = r.shape[0]
    blk = (pl.program_id(1) * tm + lax.broadcasted_iota(jnp.int32, (tm, LANE - B_DH), 0)) // B_SEL_LEN
    onehot = (blk == lax.broadcasted_iota(jnp.int32, (tm, LANE - B_DH), 1)).astype(F32)
    for gi in range(B_GROUPS):
        ks_ref[0, gi] = jnp.concatenate([r[:, gi * B_DH:(gi + 1) * B_DH], onehot],
                                        axis=1).astype(ks_ref.dtype)
    put_cols(vs_ref, seg(3), B_DH, ones_rows=16)
    put_rows(kw_ref, (0,), seg(4))
    put_cols(vw_ref, seg(5), B_DH, ones_rows=16)
    gates = jnp.dot(xn, w_ref[:, nq + 6 * gw:], preferred_element_type=F32)
    put_cols(gp_ref, gates, 16)


def _nsa_in(h, g, w):
    bsz, t_len, d = h.shape
    tm = 512
    nq = B_HEADS * B_DH
    G = B_GROUPS
    sds = jax.ShapeDtypeStruct
    k_shape = sds((bsz, G, t_len, B_DH), BF16)
    vt_shape = sds((bsz, G, B_DH + 16, t_len), BF16)
    k_spec = pl.BlockSpec((1, G, tm, B_DH), lambda b, i: (b, 0, i, 0))
    vt_spec = pl.BlockSpec((1, G, B_DH + 16, tm), lambda b, i: (b, 0, 0, i))
    return pl.pallas_call(
        _nsa_in_body,
        out_shape=(sds((bsz, t_len, nq), BF16), sds((2, bsz, G, t_len, B_DH), F32),
                   sds((bsz, G, t_len, LANE), BF16), vt_shape, k_shape, vt_shape,
                   sds((bsz, G, 16, t_len), F32)),
        grid=(bsz, t_len // tm),
        in_specs=[pl.BlockSpec((1, tm, d), lambda b, i: (b, i, 0)),
                  _const_spec((1, d)), _const_spec(w.shape)],
        out_specs=(pl.BlockSpec((1, tm, nq), lambda b, i: (b, i, 0)),
                   pl.BlockSpec((2, 1, G, tm, B_DH), lambda b, i: (0, b, 0, i, 0)),
                   pl.BlockSpec((1, G, tm, LANE), lambda b, i: (b, 0, i, 0)), vt_spec,
                   k_spec, vt_spec,
                   pl.BlockSpec((1, G, 16, tm), lambda b, i: (b, 0, 0, i))),
        compiler_params=_cparams(("parallel", "parallel")),
        name="nsa_in",
    )(h, g.reshape(1, d), w)


def _nsa_layer(h, g, w_in, b_gate, pe_k, pe_v, phik_w1, phik_b1, phik_w2,
               phiv_w1, phiv_b1, phiv_w2, w_out):
    bsz, t_len, d = h.shape
    nq = B_HEADS * B_DH
    nkv = 6 * B_GROUPS * B_DH
    wg = w_in[:, nq + nkv:].reshape(d, B_GROUPS, B_QPG, 3).transpose(0, 1, 3, 2)
    wg = jnp.pad(wg.reshape(d, B_GROUPS, 3 * B_QPG), ((0, 0), (0, 0), (0, 16 - 3 * B_QPG)))
    wg = jnp.pad(wg.reshape(d, B_GROUPS * 16), ((0, 0), (0, LANE - B_GROUPS * 16)))
    bg = b_gate.reshape(B_GROUPS, B_QPG, 3).transpose(0, 2, 1).reshape(B_GROUPS, 3 * B_QPG)
    bg = jnp.pad(bg, ((0, 0), (0, 16 - 3 * B_QPG)))
    bg = jnp.broadcast_to(bg[:, :, None], (B_GROUPS, 16, B_TQ))
    w_b = jnp.concatenate([w_in[:, :nq] * (B_DH ** -0.5 * LOG2E), w_in[:, nq:nq + nkv], wg],
                          axis=1).astype(BF16)
    q, kv_cmp, k_sel, vt_sel, k_win, vt_win, gpre_t = _nsa_in(h, g, w_b)

    cmp = _nsa_compress(
        kv_cmp, jnp.stack([pe_k, pe_v]), jnp.stack([phik_w1, phiv_w1]).astype(BF16),
        jnp.stack([phik_b1, phiv_b1]), jnp.stack([phik_w2, phiv_w2]).astype(BF16))
    return _nsa_attention(q, cmp[0], cmp[1], k_sel, vt_sel, k_win, vt_win, gpre_t, bg,
                          w_out.astype(BF16), h)


def _dil_body(*refs, t_len):
    qkv = refs[:9]
    o_ref = refs[9]
    og_ref, lg_ref = refs[10], refs[11]
    qb = C_QBLK
    assert qb == LANE and 2 * C_DH == LANE
    top = lax.broadcasted_iota(jnp.int32, (LANE, qb), 0) < C_DH

    n_blocks = t_len // qb
    per_iter = C_BLOCKS_PER_ITER
    assert n_blocks % per_iter == 0

    def body(it, carry):
        blocks = []
        for gi, (window, dil) in enumerate(C_PATTERNS):
            sub = t_len // dil
            nb = sub // qb
            nk = qb if nb == 1 else 2 * qb
            for j in range(per_iter):
                i = it * per_iter + j
                rc = i // nb
                bq = i % nb
                bk = jnp.maximum(bq - 1, 0)
                q0 = pl.multiple_of(rc * sub + bq * qb, qb)
                k0 = pl.multiple_of(rc * sub + bk * qb, qb)
                if dil == 1:
                    rows = pl.ds(q0, qb)
                else:
                    rows = pl.ds(rc + bq * (qb * dil), qb, stride=dil)
                blocks.append(dict(gi=gi, nk=nk, k0=k0, rows=rows, shift=(bq - bk) * qb,
                                   max_dist=window // dil,
                                   q2=qkv[3 * gi][0, 0, pl.ds(q0, qb), :]))
        for b in blocks:
            q_t = b["q2"].astype(F32).T
            b["q_bd"] = jnp.concatenate([jnp.where(top, q_t, 0.0), jnp.where(top, 0.0, q_t)],
                                        axis=1).astype(BF16)
        for b in blocks:
            k2 = qkv[3 * b["gi"] + 1][0, 0, pl.ds(b["k0"], b["nk"]), :]
            b["s"] = jnp.dot(k2, b["q_bd"], preferred_element_type=F32)
        for b in blocks:
            nk = b["nk"]
            diff = (b["shift"] + lax.broadcasted_iota(jnp.int32, (nk, qb), 1)
                    - lax.broadcasted_iota(jnp.int32, (nk, qb), 0))
            bias = jnp.where((diff >= 0) & (diff <= b["max_dist"]), 0.0, NEG)
            s = b["s"] + jnp.concatenate([bias, bias], axis=1)
            m = jnp.max(s, axis=0, keepdims=True)
            e = jnp.exp2(s - m)
            den = jnp.sum(e, axis=0, keepdims=True)
            b["e"] = e.astype(BF16)
            b["inv"] = 1.0 / den
            b["lse"] = m + jnp.log2(den)
        for b in blocks:
            v2 = qkv[3 * b["gi"] + 2][0, 0, pl.ds(b["k0"], b["nk"]), :]
            b["pv"] = lax.dot_general(v2, b["e"], (((0,), (0,)), ((), ())),
                                      preferred_element_type=F32)
        for b in blocks:
            pv, inv, lse = b["pv"], b["inv"], b["lse"]
            o_t = jnp.concatenate([pv[0:C_DH, 0:qb] * inv[:, 0:qb],
                                   pv[C_DH:, qb:] * inv[:, qb:]], axis=0)
            l_t = jnp.concatenate([jnp.broadcast_to(lse[:, 0:qb], (C_DH, qb)),
                                   jnp.broadcast_to(lse[:, qb:], (C_DH, qb))], axis=0)
            og_ref[b["gi"], b["rows"], :] = o_t.T
            lg_ref[b["gi"], b["rows"], :] = l_t.T
        return carry

    lax.fori_loop(0, n_blocks // per_iter, body, 0)

    lse = lg_ref[...]
    mx = jnp.max(lse, axis=0, keepdims=True)
    w = jnp.exp2(lse - mx)
    alpha = w / jnp.sum(w, axis=0, keepdims=True)
    o_ref[0, 0] = jnp.sum(alpha * og_ref[...], axis=0).astype(o_ref.dtype)


def _dilated_attention(projs):
    bsz, _, t_len, _ = projs[0].shape
    npairs = C_HEADS * C_DH // LANE
    ng = len(C_PATTERNS)

    def spec(col):
        return pl.BlockSpec((1, 1, t_len, LANE), lambda b, hp, col=col: (b, col * npairs + hp, 0, 0))

    return pl.pallas_call(
        functools.partial(_dil_body, t_len=t_len),
        out_shape=jax.ShapeDtypeStruct((bsz, npairs, t_len, LANE), BF16),
        grid=(bsz, npairs),
        in_specs=[spec(c) for _ in range(ng) for c in range(3)],
        out_specs=pl.BlockSpec((1, 1, t_len, LANE), lambda b, hp: (b, hp, 0, 0)),
        scratch_shapes=[pltpu.VMEM((ng, t_len, LANE), F32), pltpu.VMEM((ng, t_len, LANE), F32)],
        compiler_params=_cparams(("parallel", "parallel")),
        name="dilated_attention",
    )(*[projs[gi] for gi in range(ng) for _ in range(3)])


def _dilated_layer(h, g, w_in, w_out):
    bsz, t_len, d = h.shape
    gw = C_HEADS * C_DH
    scale = jnp.ones((3, 1), F32).at[0, :].set(C_DH ** -0.5 * LOG2E)
    w_c = w_in.reshape(d, len(C_PATTERNS), 3, gw)
    projs = []
    for gi, (_, dil) in enumerate(C_PATTERNS):
        w_g = (w_c[:, gi] * scale).reshape(d, 3 * gw).astype(BF16)
        p = _norm_matmul(h, g, w_g, out_dtype=BF16, tm=512, tn=3 * gw, slab=True, dil=dil,
                         name=f"dilated_in{gi}")
        projs.append(p.reshape(bsz, 3 * gw // LANE, t_len, LANE))
    return _dilated_attention(projs), w_out.astype(BF16)


def kernel(x, norm_mix, norm_ffn, norm_final, a_w_in, a_b_gate, a_conv_w, a_conv_b, a_wq, a_wk, a_wv, a_gn, a_skip, a_w_out, b_w_in, b_b_gate, b_pe_k, b_pe_v, b_phik_w1, b_phik_b1, b_phik_w2, b_phiv_w1, b_phiv_b1, b_phiv_w2, b_w_out, c_w_in, c_w_out, f_w_up, f_conv_w, f_conv_b, f_w_down):
    depth = norm_mix.shape[0]
    h = x
    for i in range(depth):
        kind, j = i % 3, i // 3
        mix = None
        if kind == 0:
            h = _mlstm_layer(h, norm_mix[i], a_w_in[j], a_b_gate[j], a_conv_w[j], a_conv_b[j],
                             a_wq[j], a_wk[j], a_wv[j], a_gn[j], a_skip[j], a_w_out[j])
        elif kind == 1:
            h = _nsa_layer(h, norm_mix[i], b_w_in[j], b_b_gate[j], b_pe_k[j], b_pe_v[j],
                           b_phik_w1[j], b_phik_b1[j], b_phik_w2[j],
                           b_phiv_w1[j], b_phiv_b1[j], b_phiv_w2[j], b_w_out[j])
        else:
            mix = _dilated_layer(h, norm_mix[i], c_w_in[j], c_w_out[j])
        h = _conv_ffn(h, norm_ffn[i], f_w_up[i].astype(BF16), f_conv_w[i], f_conv_b[i],
                      f_w_down[i].astype(BF16), norm_final, tm=512,
                      final_norm=(i == depth - 1), name="conv_ffn", mix=mix)
    return h
```

```python
import functools

import jax
import jax.numpy as jnp
from jax import lax
from jax.experimental import pallas as pl
from jax.experimental.pallas import tpu as pltpu

F32 = jnp.float32
BF16 = jnp.bfloat16

NORM_EPS = 1e-6
NEG = -1e30
LOG2E = 1.4426950408889634
LANE = 128
SUBLANE = 8
VMEM_LIMIT = 56 * 1024 * 1024
NMM_COLS = 512
NMM_MAX_STRIDE = 4

A_HEADS = 4
A_CHUNK = 256
A_CONV = 4
B_HEADS = 16
B_DH = 64
B_GROUPS = 4
B_QPG = 4
B_CMP_LEN = 32
B_CMP_STRIDE = 16
B_SEL_LEN = 64
B_TOPN = 16
B_N_LOCAL = 2
B_WINDOW = 512
FORCE_SCORE = 1e9
B_TQ = 256
B_TK = 256
C_PATTERNS = ((128, 1), (512, 4), (2048, 16))
C_HEADS = 16
C_DH = 64
C_QBLK = 128
C_BLOCKS_PER_ITER = 16
F_CHUNK = 256
F_SPLIT = 2


def _cparams(sem):
    return pltpu.CompilerParams(dimension_semantics=sem, vmem_limit_bytes=VMEM_LIMIT)


def _const_spec(shape):
    nd = len(shape)
    return pl.BlockSpec(shape, lambda *_: (0,) * nd, pipeline_mode=pl.Buffered(1))


def _rmsnorm_rows(x, g):
    ms = jnp.mean(x * x, axis=-1, keepdims=True)
    return x * lax.rsqrt(ms + NORM_EPS) * g


def _sigmoid(x):
    return 1.0 / (1.0 + jnp.exp(-x))


def _nmm_body(x_ref, g_ref, w_ref, o_ref, xn_ref, *rest, slab, dil):
    @pl.when(pl.program_id(2) == 0)
    def _():
        xn_ref[...] = _rmsnorm_rows(x_ref[0], g_ref[...]).astype(BF16)

    if not slab:
        o_ref[0] = jnp.dot(xn_ref[...], w_ref[...],
                           preferred_element_type=F32).astype(o_ref.dtype)
        return
    tm, tn = xn_ref.shape[0], w_ref.shape[1]
    step = min(tn, NMM_COLS)
    for c0 in range(0, tn, step):
        r = jnp.dot(xn_ref[...], w_ref[:, c0:c0 + step], preferred_element_type=F32)
        for cl in range(step // LANE):
            c = c0 // LANE + cl
            piece = r[:, cl * LANE:(cl + 1) * LANE]
            if dil == 1:
                o_ref[0, c, 0] = piece.astype(o_ref.dtype)
            else:
                r_ref, t_ref = rest
                r_ref[c] = piece
                if dil <= NMM_MAX_STRIDE:
                    for rc in range(dil):
                        o_ref[0, c, rc] = r_ref[c, pl.ds(rc, tm // dil, stride=dil),
                                                :].astype(o_ref.dtype)
                else:
                    d1 = NMM_MAX_STRIDE
                    d2 = dil // d1
                    run = tm // d1
                    for a in range(d1):
                        t_ref[a * run:(a + 1) * run, :] = r_ref[c, pl.ds(a, run, stride=d1), :]
                    for a in range(d1):
                        for j in range(d2):
                            o_ref[0, c, d1 * j + a] = t_ref[
                                pl.ds(a * run + j, run // d2, stride=d2), :].astype(o_ref.dtype)


def _norm_matmul(h, g, w, *, out_dtype, tm, tn, slab=False, dil=1, name):
    bsz, t_len, d = h.shape
    n = w.shape[1]
    assert t_len % tm == 0 and n % tn == 0 and tn % LANE == 0 and tm % (2 * SUBLANE * dil) == 0
    scratch = [pltpu.VMEM((tm, d), BF16)]
    if slab:
        out_shape = jax.ShapeDtypeStruct((bsz, n // LANE, dil, t_len // dil, LANE), out_dtype)
        out_spec = pl.BlockSpec((1, tn // LANE, dil, tm // dil, LANE),
                                lambda b, i, j: (b, j, 0, i, 0))
        if dil > 1:
            assert dil <= NMM_MAX_STRIDE or dil % NMM_MAX_STRIDE == 0
            scratch += [pltpu.VMEM((tn // LANE, tm, LANE), F32), pltpu.VMEM((tm, LANE), F32)]
    else:
        out_shape = jax.ShapeDtypeStruct((bsz, t_len, n), out_dtype)
        out_spec = pl.BlockSpec((1, tm, tn), lambda b, i, j: (b, i, j))
    return pl.pallas_call(
        functools.partial(_nmm_body, slab=slab, dil=dil),
        out_shape=out_shape,
        grid=(bsz, t_len // tm, n // tn),
        in_specs=[
            pl.BlockSpec((1, tm, d), lambda b, i, j: (b, i, 0)),
            pl.BlockSpec((1, d), lambda b, i, j: (0, 0)),
            _const_spec((d, n)) if tn == n else pl.BlockSpec((d, tn), lambda b, i, j: (0, j)),
        ],
        out_specs=out_spec,
        scratch_shapes=scratch,
        compiler_params=_cparams(("parallel", "parallel", "arbitrary")),
        name=name,
    )(h, g.reshape(1, d), w)


def _ffn_body(h_ref, hp_ref, g_ref, wup_ref, cw_ref, cb_ref, wdn_ref, gf_ref, *rest,
              tm, f_hidden, final_norm, mixed):
    halo = SUBLANE
    g = g_ref[...]
    if mixed:
        ox_ref, oxp_ref, wmix_ref, o_ref, xn_ref, acc_ref, hres_ref = rest
        slabs = range(ox_ref.shape[1])
        o_cur = jnp.concatenate([ox_ref[0, c] for c in slabs], axis=-1)
        o_prev = jnp.concatenate([oxp_ref[0, c] for c in slabs], axis=-1)
        hres_ref[...] = h_ref[0] + jnp.dot(o_cur, wmix_ref[...], preferred_element_type=F32)
        h_cur = hres_ref[...]
        h_prev = hp_ref[0] + jnp.dot(o_prev, wmix_ref[...],
                                     preferred_element_type=F32)[halo:2 * halo]
    else:
        o_ref, xn_ref, acc_ref = rest
        h_cur, h_prev = h_ref[0], hp_ref[0]
    keep = (pl.program_id(1) > 0).astype(F32)
    xn_ref[0:halo, :] = (_rmsnorm_rows(h_prev, g) * keep).astype(BF16)
    xn_ref[halo:, :] = _rmsnorm_rows(h_cur, g).astype(BF16)
    rows = tm // F_SPLIT
    xns = [xn_ref[s * rows:s * rows + rows + halo, :] for s in range(F_SPLIT)]

    def conv(u, col0):
        w = cw_ref[:, col0:col0 + F_CHUNK]
        b = cb_ref[:, col0:col0 + F_CHUNK]
        return (w[2:3] * u[halo:halo + rows] + w[1:2] * u[halo - 1:halo - 1 + rows]
                + w[0:1] * u[halo - 2:halo - 2 + rows] + b)

    def up(c):
        c0 = c * F_CHUNK
        return [(jnp.dot(xn, wup_ref[:, c0:c0 + F_CHUNK], preferred_element_type=F32),
                 jnp.dot(xn, wup_ref[:, f_hidden + c0:f_hidden + c0 + F_CHUNK],
                         preferred_element_type=F32)) for xn in xns]

    n_chunks = f_hidden // F_CHUNK
    nxt = up(0)
    accs = [None] * F_SPLIT
    for c in range(n_chunks):
        c0 = c * F_CHUNK
        cur = nxt
        if c + 1 < n_chunks:
            nxt = up(c + 1)
        for s, (ug, uv) in enumerate(cur):
            gate = conv(ug, c0)
            val = conv(uv, f_hidden + c0)
            act = (gate * _sigmoid(gate) * val).astype(BF16)
            part = jnp.dot(act, wdn_ref[c0:c0 + F_CHUNK, :], preferred_element_type=F32)
            accs[s] = part if c == 0 else accs[s] + part
    for s in range(F_SPLIT):
        acc_ref[s * rows:(s + 1) * rows, :] = accs[s]

    out = (hres_ref[...] if mixed else h_ref[0]) + acc_ref[...]
    if final_norm:
        out = _rmsnorm_rows(out, gf_ref[...])
    o_ref[0] = out


def _conv_ffn(h, g, w_up, conv_w, conv_b, w_down, g_final, *, tm, final_norm, name, mix=None):
    bsz, t_len, d = h.shape
    f2 = w_up.shape[1]
    f_hidden = f2 // 2
    assert f_hidden % F_CHUNK == 0 and t_len % tm == 0 and tm % (2 * SUBLANE) == 0
    rows_per_tile = tm // SUBLANE
    in_specs = [
        pl.BlockSpec((1, tm, d), lambda b, i: (b, i, 0)),
        pl.BlockSpec((1, SUBLANE, d), lambda b, i: (b, jnp.maximum(i * rows_per_tile - 1, 0), 0)),
        _const_spec((1, d)),
        _const_spec((d, f2)),
        _const_spec((3, f2)),
        _const_spec((1, f2)),
        _const_spec((f_hidden, d)),
        _const_spec((1, d)),
    ]
    args = [h, h, g.reshape(1, d), w_up, conv_w, conv_b.reshape(1, f2), w_down,
            g_final.reshape(1, d)]
    scratch = [pltpu.VMEM((tm + SUBLANE, d), BF16), pltpu.VMEM((tm, d), F32)]
    if mix is not None:
        o, w_mix = mix
        n_slab = o.shape[1]
        in_specs += [
            pl.BlockSpec((1, n_slab, tm, LANE), lambda b, i: (b, 0, i, 0)),
            pl.BlockSpec((1, n_slab, 2 * SUBLANE, LANE),
                         lambda b, i: (b, 0, jnp.maximum(i * (rows_per_tile // 2) - 1, 0), 0)),
            _const_spec(w_mix.shape),
        ]
        args += [o, o, w_mix]
        scratch.append(pltpu.VMEM((tm, d), F32))
    return pl.pallas_call(
        functools.partial(_ffn_body, tm=tm, f_hidden=f_hidden, final_norm=final_norm,
                          mixed=mix is not None),
        out_shape=jax.ShapeDtypeStruct((bsz, t_len, d), F32),
        grid=(bsz, t_len // tm),
        in_specs=in_specs,
        out_specs=pl.BlockSpec((1, tm, d), lambda b, i: (b, i, 0)),
        scratch_shapes=scratch,
        compiler_params=_cparams(("parallel", "parallel")),
        name=name,
    )(*args)


def _mlstm_body(h_ref, g_ref, win_ref, bg_ref, cw_ref, cb_ref, wq_ref, wk_ref, wv_ref,
                gn_ref, sk_ref, wout_ref, o_ref, c_ref, n_ref, m_ref, xp_ref, y_ref,
                *, chunk, dh):
    L = chunk
    halo = SUBLANE
    inner = A_HEADS * dh

    @pl.when(pl.program_id(1) == 0)
    def _():
        c_ref[...] = jnp.zeros_like(c_ref)
        n_ref[...] = jnp.zeros_like(n_ref)
        m_ref[...] = jnp.zeros_like(m_ref)
        xp_ref[...] = jnp.zeros_like(xp_ref)

    xn = _rmsnorm_rows(h_ref[0], g_ref[...]).astype(BF16)
    xm = jnp.dot(xn, win_ref[:, 0:inner], preferred_element_type=F32)
    gates = jnp.dot(xn, win_ref[:, 2 * inner:], preferred_element_type=F32) + bg_ref[...]
    heads = [dict(lo=hd * dh, hi=(hd + 1) * dh) for hd in range(A_HEADS)]
    for hd, st in enumerate(heads):
        st["v"] = jnp.dot(xm[:, st["lo"]:st["hi"]].astype(BF16), wv_ref[hd],
                          preferred_element_type=F32)
    o_pre = jnp.dot(xn, win_ref[:, inner:2 * inner], preferred_element_type=F32)

    xe = jnp.concatenate([xp_ref[...], xm], axis=0)
    cw = cw_ref[...]
    xc = (cw[3:4] * xe[halo:halo + L] + cw[2:3] * xe[halo - 1:halo - 1 + L]
          + cw[1:2] * xe[halo - 2:halo - 2 + L] + cw[0:1] * xe[halo - 3:halo - 3 + L]
          + cb_ref[...])
    xc = xc * _sigmoid(xc)
    xp_ref[...] = xm[L - halo:L]

    ii = lax.broadcasted_iota(jnp.int32, (L, L), 0)
    jj = lax.broadcasted_iota(jnp.int32, (L, L), 1)
    eye = ii == jj
    tril = jj <= ii

    def to_row(col):
        return jnp.sum(jnp.where(eye, col, 0.0), axis=0, keepdims=True)

    for hd, st in enumerate(heads):
        st["xc"] = xc[:, st["lo"]:st["hi"]]
        xcb = st["xc"].astype(BF16)
        st["q"] = jnp.dot(xcb, wq_ref[hd], preferred_element_type=F32)
        st["k"] = jnp.dot(xcb, wk_ref[hd], preferred_element_type=F32)

    for hd, st in enumerate(heads):
        li = gates[:, hd:hd + 1]
        fpre = gates[:, A_HEADS + hd:A_HEADS + hd + 1]
        lf = jnp.minimum(fpre, 0.0) - jnp.log1p(jnp.exp(-jnp.abs(fpre)))
        li_row = to_row(li)
        lf_row = to_row(lf)
        b_col = jnp.sum(jnp.where(tril, lf_row, 0.0), axis=1, keepdims=True)
        b_row = to_row(b_col)
        m_st = m_ref[hd][:, 0:1]
        log_d = jnp.where(tril, b_col - b_row + li_row, NEG)
        log_inter = b_col + m_st
        m = jnp.maximum(log_inter, jnp.max(log_d, axis=1, keepdims=True))
        st["m"] = m
        st["w_intra"] = jnp.exp(log_d - m)
        st["w_inter"] = jnp.exp(log_inter - m)
        b_last = b_col[L - 1:L, :]
        log_w = b_last - b_col + li
        m_new = jnp.maximum(b_last + m_st, jnp.max(log_w, axis=0, keepdims=True))
        st["decay"] = jnp.exp(b_last + m_st - m_new)
        st["w_k"] = jnp.exp(log_w - m_new)
        m_ref[hd] = jnp.broadcast_to(m_new, m_ref.shape[1:])

    for hd, st in enumerate(heads):
        st["k"] = st["k"] * (dh ** -0.5)
        st["qb"] = st["q"].astype(BF16)
        st["vb"] = st["v"].astype(BF16)
        st["s"] = lax.dot_general(st["qb"], st["k"].astype(BF16), (((1,), (1,)), ((), ())),
                                  preferred_element_type=F32)
        st["qc"] = jnp.dot(st["qb"], c_ref[hd].astype(BF16), preferred_element_type=F32)

    for hd, st in enumerate(heads):
        s = st["s"] * st["w_intra"]
        st["den"] = (st["w_inter"] * jnp.sum(st["q"] * n_ref[hd], axis=1, keepdims=True)
                     + jnp.sum(s, axis=1, keepdims=True))
        wk = st["w_k"] * st["k"]
        n_ref[hd] = st["decay"] * n_ref[hd] + jnp.sum(wk, axis=0, keepdims=True)
        st["sv"] = jnp.dot(s.astype(BF16), st["vb"], preferred_element_type=F32)
        st["kv"] = lax.dot_general(wk.astype(BF16), st["vb"], (((0,), (0,)), ((), ())),
                                   preferred_element_type=F32)

    for hd, st in enumerate(heads):
        lo, hi = st["lo"], st["hi"]
        c_ref[hd] = st["decay"] * c_ref[hd] + st["kv"]
        num = st["w_inter"] * st["qc"] + st["sv"]
        hh = num / jnp.maximum(jnp.abs(st["den"]), jnp.exp(-st["m"]))
        mu = jnp.mean(hh, axis=-1, keepdims=True)
        ctr = hh - mu
        var = jnp.mean(ctr * ctr, axis=-1, keepdims=True)
        hn = ctr * lax.rsqrt(var + NORM_EPS) * gn_ref[:, lo:hi]
        y = _sigmoid(o_pre[:, lo:hi]) * hn + sk_ref[:, lo:hi] * st["xc"]
        y_ref[:, lo:hi] = y.astype(y_ref.dtype)

    o_ref[0] = h_ref[0] + jnp.dot(y_ref[...], wout_ref[...], preferred_element_type=F32)


def _mlstm_layer(h, g, w_in, b_gate, conv_w, conv_b, wq, wk, wv, gn, skip, w_out):
    bsz, t_len, d = h.shape
    inner = w_out.shape[0]
    dh = inner // A_HEADS
    L = A_CHUNK
    w_a = w_in.astype(BF16)
    bg = b_gate.reshape(1, 2 * A_HEADS)
    return pl.pallas_call(
        functools.partial(_mlstm_body, chunk=L, dh=dh),
        out_shape=jax.ShapeDtypeStruct((bsz, t_len, d), F32),
        grid=(bsz, t_len // L),
        in_specs=[
            pl.BlockSpec((1, L, d), lambda b, c: (b, c, 0)),
            _const_spec((1, d)),
            _const_spec((d, 2 * inner + 2 * A_HEADS)),
            _const_spec((1, 2 * A_HEADS)),
            _const_spec((A_CONV, inner)),
            _const_spec((1, inner)),
            _const_spec((A_HEADS, dh, dh)),
            _const_spec((A_HEADS, dh, dh)),
            _const_spec((A_HEADS, dh, dh)),
            _const_spec((1, inner)),
            _const_spec((1, inner)),
            _const_spec((inner, d)),
        ],
        out_specs=pl.BlockSpec((1, L, d), lambda b, c: (b, c, 0)),
        scratch_shapes=[
            pltpu.VMEM((A_HEADS, dh, dh), F32),
            pltpu.VMEM((A_HEADS, 1, dh), F32),
            pltpu.VMEM((A_HEADS, 1, LANE), F32),
            pltpu.VMEM((SUBLANE, inner), F32),
            pltpu.VMEM((L, inner), BF16),
        ],
        compiler_params=_cparams(("parallel", "arbitrary")),
        name="mlstm_layer",
    )(h, g.reshape(1, d), w_a, bg, conv_w, conv_b.reshape(1, inner), wq.astype(BF16),
      wk.astype(BF16), wv.astype(BF16), gn.reshape(1, inner), skip.reshape(1, inner),
      w_out.astype(BF16))


def _cmp_body(x_ref, pe_ref, w1_ref, b1_ref, w2_ref, o_ref):
    half = x_ref.shape[-1]
    x = x_ref[0, 0, 0]
    pe = pe_ref[0]
    a = (x + pe[:, :half]).astype(BF16)
    b = (x + pe[:, half:]).astype(BF16)
    w1 = w1_ref[0]
    pa = jnp.dot(a, w1[:half], preferred_element_type=F32)
    pb = jnp.dot(b, w1[half:], preferred_element_type=F32)
    pb_next = jnp.concatenate([pb[1:], jnp.zeros((1, pb.shape[1]), F32)], axis=0)
    pre = pa + pb_next + b1_ref[0]
    hid = (pre * _sigmoid(pre)).astype(BF16)
    out = jnp.dot(hid, w2_ref[0], preferred_element_type=F32)
    row = lax.broadcasted_iota(jnp.int32, out.shape, 0)
    o_ref[0, 0, 0] = jnp.where(row < out.shape[0] - 1, out, 0.0).astype(o_ref.dtype)


def _nsa_compress(kv_cmp, pe, w1, b1, w2):
    two, bsz, g, t_len, dh = kv_cmp.shape
    n_half = t_len // B_CMP_STRIDE
    half = B_CMP_STRIDE * dh
    x = kv_cmp.reshape(two, bsz, g, n_half, half)
    hidden = w1.shape[-1]
    return pl.pallas_call(
        _cmp_body,
        out_shape=jax.ShapeDtypeStruct((two, bsz, g, n_half, dh), BF16),
        grid=(two, bsz, g),
        in_specs=[
            pl.BlockSpec((1, 1, 1, n_half, half), lambda s, b, gi: (s, b, gi, 0, 0)),
            pl.BlockSpec((1, 1, 2 * half), lambda s, b, gi: (s, 0, 0)),
            pl.BlockSpec((1, 2 * half, hidden), lambda s, b, gi: (s, 0, 0)),
            pl.BlockSpec((1, 1, hidden), lambda s, b, gi: (s, 0, 0)),
            pl.BlockSpec((1, hidden, dh), lambda s, b, gi: (s, 0, 0)),
        ],
        out_specs=pl.BlockSpec((1, 1, 1, n_half, dh), lambda s, b, gi: (s, b, gi, 0, 0)),
        compiler_params=_cparams(("parallel", "parallel", "parallel")),
        name="nsa_compress",
    )(x, pe.reshape(two, 1, 2 * half), w1, b1.reshape(two, 1, hidden), w2)


def _flash_t(q_ts, k_ref, vt_ref, acc_ref, m_ref, kt_lo, kt_hi, bias_fn, last_bias_fn, tk):
    acc_ref[...] = jnp.zeros_like(acc_ref)
    m_ref[...] = jnp.full_like(m_ref, NEG)
    groups = range(len(q_ts))

    def tile(kt, bias_of):
        ks = pl.multiple_of(kt * tk, tk)
        ss = [jnp.dot(k_ref[0, g, pl.ds(ks, tk), :], q_ts[g], preferred_element_type=F32)
              for g in groups]
        bias = None if bias_of is None else bias_of(kt, ks)
        ps, alphas = [], []
        for g in groups:
            s = ss[g] if bias is None else ss[g] + bias
            m = m_ref[g]
            m_new = jnp.maximum(m, jnp.max(s, axis=0, keepdims=True))
            alphas.append(jnp.exp2(m - m_new))
            ps.append(jnp.exp2(s - m_new).astype(BF16))
            m_ref[g] = m_new
        for g in groups:
            v_t = vt_ref[0, g, :, pl.ds(ks, tk)]
            acc_ref[g] = alphas[g] * acc_ref[g] + jnp.dot(v_t, ps[g],
                                                          preferred_element_type=F32)

    def body(kt, carry):
        tile(kt, bias_fn)
        return carry

    lax.fori_loop(kt_lo, kt_hi - 1, body, 0)
    tile(kt_hi - 1, last_bias_fn)
    dh = acc_ref.shape[1] - 16
    return [acc_ref[g, 0:dh, :] / acc_ref[g, dh:dh + 1, :] for g in groups]


def _nsa_body(q_ref, kc_ref, vc_ref, ks_ref, vs_ref, kw_ref, vw_ref, gp_ref, bg_ref,
              ovl_ref, wout_ref, h_ref, o_ref, acc_ref, m_ref, og_ref, *, tq, tk, t_len):
    qi = pl.program_id(1)
    t0 = qi * tq
    gw = B_QPG * B_DH
    ncp = kc_ref.shape[2]
    nsel = t_len // B_SEL_LEN
    n_top = min(B_TOPN, nsel)

    def lanes4(x):
        return jnp.concatenate([x] * B_QPG, axis=1)

    n_idx = lax.broadcasted_iota(jnp.int32, (ncp, tq), 0)
    t_col = t0 + lax.broadcasted_iota(jnp.int32, (ncp, tq), 1)
    cmask = (n_idx * B_CMP_STRIDE + (B_CMP_LEN - 1) <= t_col) & (n_idx < ncp - 1)
    cbias = lanes4(jnp.where(cmask, 0.0, NEG))
    ckeep = lanes4(cmask.astype(F32))
    blk = lax.broadcasted_iota(jnp.int32, (nsel, tq), 0)
    cur = (t0 + lax.broadcasted_iota(jnp.int32, (nsel, tq), 1)) // B_SEL_LEN
    valid = blk <= cur
    forced = ((blk == 0) | (cur - blk < B_N_LOCAL)) & valid

    q_ts, q_sels, o_cmps = [], [], []
    for g in range(B_GROUPS):
        q_full = q_ref[0, :, g * gw:(g + 1) * gw].astype(F32).T.astype(BF16)
        q_t = jnp.concatenate([q_full[r * B_DH:(r + 1) * B_DH, :] for r in range(B_QPG)], axis=1)
        q_ts.append(q_t)

        s = jnp.dot(kc_ref[0, g], q_t, preferred_element_type=F32) + cbias
        m = jnp.max(s, axis=0, keepdims=True)
        e = jnp.exp2(s - m) * ckeep
        den = jnp.sum(e, axis=0, keepdims=True)
        p = e / jnp.where(den > 0, den, 1.0)
        o_cmps.append(lax.dot_general(vc_ref[0, g], p.astype(BF16), (((0,), (0,)), ((), ())),
                                      preferred_element_type=F32))

        psum = p[:, 0:tq]
        for r in range(1, B_QPG):
            psum = psum + p[:, r * tq:(r + 1) * tq]
        p_hi = psum.astype(BF16)
        r1 = psum - p_hi.astype(F32)
        p_mid = r1.astype(BF16)
        p_lo = (r1 - p_mid.astype(F32)).astype(BF16)
        ovl = ovl_ref[...]
        imp = (jnp.dot(ovl, p_hi, preferred_element_type=F32)
               + jnp.dot(ovl, p_mid, preferred_element_type=F32)
               + jnp.dot(ovl, p_lo, preferred_element_type=F32))
        imp = imp[0:nsel]
        score = jnp.where(forced, FORCE_SCORE, jnp.where(valid, imp, -FORCE_SCORE))
        slabs = [score[r0:r0 + SUBLANE] for r0 in range(0, nsel, SUBLANE)]
        ranks = [jnp.zeros((SUBLANE, tq), F32) for _ in slabs]
        row = lax.broadcasted_iota(jnp.int32, (SUBLANE, tq), 0)
        for j in range(nsel):
            sj = score[j:j + 1, :]
            for si, sc in enumerate(slabs):
                r0 = si * SUBLANE
                ge = jnp.where(sj >= sc, 1.0, 0.0)
                gt = jnp.where(sj > sc, 1.0, 0.0)
                if r0 > j:
                    ahead = ge
                elif r0 + SUBLANE - 1 < j:
                    ahead = gt
                else:
                    ahead = jnp.where(row + r0 > j, ge, gt)
                ranks[si] = ranks[si] + ahead
        rank = jnp.concatenate(ranks, axis=0)
        sel_bias = jnp.where((rank < n_top) & valid, 0.0, NEG)
        q_sels.append(jnp.concatenate(
            [q_t, lanes4(sel_bias).astype(BF16),
             jnp.zeros((LANE - B_DH - nsel, B_QPG * tq), BF16)], axis=0))

    k_off = lax.broadcasted_iota(jnp.int32, (tk, tq), 0)
    t_pos = t0 + lax.broadcasted_iota(jnp.int32, (tk, tq), 1)

    def causal_bias(kt, ks):
        return lanes4(jnp.where(ks + k_off <= t_pos, 0.0, NEG))

    kt_hi = (t0 + tq + tk - 1) // tk
    o_sels = _flash_t(q_sels, ks_ref, vs_ref, acc_ref, m_ref, 0, kt_hi, None, causal_bias, tk)

    def win_bias(kt, ks):
        diff = t_pos - (ks + k_off)
        return lanes4(jnp.where((diff >= 0) & (diff <= B_WINDOW - 1), 0.0, NEG))

    w_lo = jnp.maximum(t0 - (B_WINDOW - 1), 0) // tk
    o_wins = _flash_t(q_ts, kw_ref, vw_ref, acc_ref, m_ref, w_lo, kt_hi, win_bias, win_bias, tk)

    for g in range(B_GROUPS):
        gate = _sigmoid(gp_ref[0, g] + bg_ref[g])
        outs = []
        for r in range(B_QPG):
            sl = slice(r * tq, (r + 1) * tq)
            outs.append(gate[r:r + 1, :] * o_cmps[g][:, sl]
                        + gate[B_QPG + r:B_QPG + r + 1, :] * o_sels[g][:, sl]
                        + gate[2 * B_QPG + r:2 * B_QPG + r + 1, :] * o_wins[g][:, sl])
        og_ref[:, g * gw:(g + 1) * gw] = jnp.concatenate(outs, axis=0).T.astype(og_ref.dtype)

    o_ref[0] = h_ref[0] + jnp.dot(og_ref[...], wout_ref[...], preferred_element_type=F32)


def _nsa_attention(q, kc, vc, k_sel, vt_sel, k_win, vt_win, gpre_t, bg, w_out, h):
    bsz, t_len, _ = q.shape
    d = h.shape[-1]
    tq, tk = B_TQ, B_TK
    ncp = kc.shape[2]
    nsel = t_len // B_SEL_LEN
    gw = B_QPG * B_DH
    c_start = jnp.arange(ncp)[None, :] * B_CMP_STRIDE
    s_start = jnp.arange(LANE)[:, None] * B_SEL_LEN
    ovl = ((c_start < s_start + B_SEL_LEN) & (c_start + B_CMP_LEN > s_start)
           & (jnp.arange(LANE)[:, None] < nsel) & (jnp.arange(ncp)[None, :] < ncp - 1)).astype(BF16)

    def k_spec(n_rows, width=B_DH):
        return pl.BlockSpec((1, B_GROUPS, n_rows, width), lambda b, i: (b, 0, 0, 0))

    vt_spec = pl.BlockSpec((1, B_GROUPS, B_DH + 16, t_len), lambda b, i: (b, 0, 0, 0))
    n_q = B_QPG * tq

    return pl.pallas_call(
        functools.partial(_nsa_body, tq=tq, tk=tk, t_len=t_len),
        out_shape=jax.ShapeDtypeStruct((bsz, t_len, d), F32),
        grid=(bsz, t_len // tq),
        in_specs=[
            pl.BlockSpec((1, tq, B_GROUPS * gw), lambda b, i: (b, i, 0)),
            k_spec(ncp), k_spec(ncp), k_spec(t_len, LANE), vt_spec, k_spec(t_len), vt_spec,
            pl.BlockSpec((1, B_GROUPS, 16, tq), lambda b, i: (b, 0, 0, i)),
            _const_spec((B_GROUPS, 16, tq)),
            _const_spec((LANE, ncp)),
            _const_spec((B_GROUPS * gw, d)),
            pl.BlockSpec((1, tq, d), lambda b, i: (b, i, 0)),
        ],
        out_specs=pl.BlockSpec((1, tq, d), lambda b, i: (b, i, 0)),
        scratch_shapes=[pltpu.VMEM((B_GROUPS, B_DH + 16, n_q), F32),
                        pltpu.VMEM((B_GROUPS, 1, n_q), F32),
                        pltpu.VMEM((tq, B_GROUPS * gw), BF16)],
        compiler_params=_cparams(("parallel", "parallel")),
        name="nsa_attention",
    )(q, kc, vc, k_sel, vt_sel, k_win, vt_win, gpre_t, bg, ovl, w_out, h)


def _nsa_in_body(x_ref, g_ref, w_ref, q_ref, kvc_ref, ks_ref, vs_ref, kw_ref, vw_ref, gp_ref):
    nq = B_HEADS * B_DH
    gw = B_GROUPS * B_DH
    xn = _rmsnorm_rows(x_ref[0], g_ref[...]).astype(BF16)
    q_ref[0] = jnp.dot(xn, w_ref[:, 0:nq], preferred_element_type=F32).astype(q_ref.dtype)

    def seg(i):
        return jnp.dot(xn, w_ref[:, nq + i * gw:nq + (i + 1) * gw], preferred_element_type=F32)

    def put_rows(ref, lead, r):
        for gi in range(B_GROUPS):
            ref[lead + (gi,)] = r[:, gi * B_DH:(gi + 1) * B_DH].astype(ref.dtype)

    def put_cols(ref, r, width, ones_rows=0):
        r_t = r.T
        tm = r.shape[0]
        for gi in range(B_GROUPS):
            ref[0, gi, 0:width, :] = r_t[gi * width:(gi + 1) * width, :].astype(ref.dtype)
            if ones_rows:
                ref[0, gi, width:width + ones_rows, :] = jnp.ones((ones_rows, tm), ref.dtype)

    put_rows(kvc_ref, (0, 0), seg(0))
    put_rows(kvc_ref, (1, 0), seg(1))
    r = seg(2)
    tm = r.shape[0]
    blk = (pl.program_id(1) * tm + lax.broadcasted_iota(jnp.int32, (tm, LANE - B_DH), 0)) // B_SEL_LEN
    onehot = (blk == lax.broadcasted_iota(jnp.int32, (tm, LANE - B_DH), 1)).astype(F32)
    for gi in range(B_GROUPS):
        ks_ref[0, gi] = jnp.concatenate([r[:, gi * B_DH:(gi + 1) * B_DH], onehot],
                                        axis=1).astype(ks_ref.dtype)
    put_cols(vs_ref, seg(3), B_DH, ones_rows=16)
    put_rows(kw_ref, (0,), seg(4))
    put_cols(vw_ref, seg(5), B_DH, ones_rows=16)
    gates = jnp.dot(xn, w_ref[:, nq + 6 * gw:], preferred_element_type=F32)
    put_cols(gp_ref, gates, 16)


def _nsa_in(h, g, w):
    bsz, t_len, d = h.shape
    tm = 512
    nq = B_HEADS * B_DH
    G = B_GROUPS
    sds = jax.ShapeDtypeStruct
    k_shape = sds((bsz, G, t_len, B_DH), BF16)
    vt_shape = sds((bsz, G, B_DH + 16, t_len), BF16)
    k_spec = pl.BlockSpec((1, G, tm, B_DH), lambda b, i: (b, 0, i, 0))
    vt_spec = pl.BlockSpec((1, G, B_DH + 16, tm), lambda b, i: (b, 0, 0, i))
    return pl.pallas_call(
        _nsa_in_body,
        out_shape=(sds((bsz, t_len, nq), BF16), sds((2, bsz, G, t_len, B_DH), F32),
                   sds((bsz, G, t_len, LANE), BF16), vt_shape, k_shape, vt_shape,
                   sds((bsz, G, 16, t_len), F32)),
        grid=(bsz, t_len // tm),
        in_specs=[pl.BlockSpec((1, tm, d), lambda b, i: (b, i, 0)),
                  _const_spec((1, d)), _const_spec(w.shape)],
        out_specs=(pl.BlockSpec((1, tm, nq), lambda b, i: (b, i, 0)),
                   pl.BlockSpec((2, 1, G, tm, B_DH), lambda b, i: (0, b, 0, i, 0)),
                   pl.BlockSpec((1, G, tm, LANE), lambda b, i: (b, 0, i, 0)), vt_spec,
                   k_spec, vt_spec,
                   pl.BlockSpec((1, G, 16, tm), lambda b, i: (b, 0, 0, i))),
        compiler_params=_cparams(("parallel", "parallel")),
        name="nsa_in",
    )(h, g.reshape(1, d), w)


def _nsa_layer(h, g, w_in, b_gate, pe_k, pe_v, phik_w1, phik_b1, phik_w2,
               phiv_w1, phiv_b1, phiv_w2, w_out):
    bsz, t_len, d = h.shape
    nq = B_HEADS * B_DH
    nkv = 6 * B_GROUPS * B_DH
    wg = w_in[:, nq + nkv:].reshape(d, B_GROUPS, B_QPG, 3).transpose(0, 1, 3, 2)
    wg = jnp.pad(wg.reshape(d, B_GROUPS, 3 * B_QPG), ((0, 0), (0, 0), (0, 16 - 3 * B_QPG)))
    wg = jnp.pad(wg.reshape(d, B_GROUPS * 16), ((0, 0), (0, LANE - B_GROUPS * 16)))
    bg = b_gate.reshape(B_GROUPS, B_QPG, 3).transpose(0, 2, 1).reshape(B_GROUPS, 3 * B_QPG)
    bg = jnp.pad(bg, ((0, 0), (0, 16 - 3 * B_QPG)))
    bg = jnp.broadcast_to(bg[:, :, None], (B_GROUPS, 16, B_TQ))
    w_b = jnp.concatenate([w_in[:, :nq] * (B_DH ** -0.5 * LOG2E), w_in[:, nq:nq + nkv], wg],
                          axis=1).astype(BF16)
    q, kv_cmp, k_sel, vt_sel, k_win, vt_win, gpre_t = _nsa_in(h, g, w_b)

    cmp = _nsa_compress(
        kv_cmp, jnp.stack([pe_k, pe_v]), jnp.stack([phik_w1, phiv_w1]).astype(BF16),
        jnp.stack([phik_b1, phiv_b1]), jnp.stack([phik_w2, phiv_w2]).astype(BF16))
    return _nsa_attention(q, cmp[0], cmp[1], k_sel, vt_sel, k_win, vt_win, gpre_t, bg,
                          w_out.astype(BF16), h)


def _dil_body(*refs, t_len):
    qkv = refs[:9]
    o_ref = refs[9]
    og_ref, lg_ref = refs[10], refs[11]
    qb = C_QBLK
    assert qb == LANE and 2 * C_DH == LANE
    top = lax.broadcasted_iota(jnp.int32, (LANE, qb), 0) < C_DH

    n_blocks = t_len // qb
    per_iter = C_BLOCKS_PER_ITER
    assert n_blocks % per_iter == 0

    def body(it, carry):
        blocks = []
        for gi, (window, dil) in enumerate(C_PATTERNS):
            sub = t_len // dil
            nb = sub // qb
            nk = qb if nb == 1 else 2 * qb
            for j in range(per_iter):
                i = it * per_iter + j
                rc = i // nb
                bq = i % nb
                bk = jnp.maximum(bq - 1, 0)
                q0 = pl.multiple_of(rc * sub + bq * qb, qb)
                k0 = pl.multiple_of(rc * sub + bk * qb, qb)
                if dil == 1:
                    rows = pl.ds(q0, qb)
                else:
                    rows = pl.ds(rc + bq * (qb * dil), qb, stride=dil)
                blocks.append(dict(gi=gi, nk=nk, k0=k0, rows=rows, shift=(bq - bk) * qb,
                                   max_dist=window // dil,
                                   q2=qkv[3 * gi][0, 0, pl.ds(q0, qb), :]))
        for b in blocks:
            q_t = b["q2"].astype(F32).T
            b["q_bd"] = jnp.concatenate([jnp.where(top, q_t, 0.0), jnp.where(top, 0.0, q_t)],
                                        axis=1).astype(BF16)
        for b in blocks:
            k2 = qkv[3 * b["gi"] + 1][0, 0, pl.ds(b["k0"], b["nk"]), :]
            b["s"] = jnp.dot(k2, b["q_bd"], preferred_element_type=F32)
        for b in blocks:
            nk = b["nk"]
            diff = (b["shift"] + lax.broadcasted_iota(jnp.int32, (nk, qb), 1)
                    - lax.broadcasted_iota(jnp.int32, (nk, qb), 0))
            bias = jnp.where((diff >= 0) & (diff <= b["max_dist"]), 0.0, NEG)
            s = b["s"] + jnp.concatenate([bias, bias], axis=1)
            m = jnp.max(s, axis=0, keepdims=True)
            e = jnp.exp2(s - m)
            den = jnp.sum(e, axis=0, keepdims=True)
            b["e"] = e.astype(BF16)
            b["inv"] = 1.0 / den
            b["lse"] = m + jnp.log2(den)
        for b in blocks:
            v2 = qkv[3 * b["gi"] + 2][0, 0, pl.ds(b["k0"], b["nk"]), :]
            b["pv"] = lax.dot_general(v2, b["e"], (((0,), (0,)), ((), ())),
                                      preferred_element_type=F32)
        for b in blocks:
            pv, inv, lse = b["pv"], b["inv"], b["lse"]
            o_t = jnp.concatenate([pv[0:C_DH, 0:qb] * inv[:, 0:qb],
                                   pv[C_DH:, qb:] * inv[:, qb:]], axis=0)
            l_t = jnp.concatenate([jnp.broadcast_to(lse[:, 0:qb], (C_DH, qb)),
                                   jnp.broadcast_to(lse[:, qb:], (C_DH, qb))], axis=0)
            og_ref[b["gi"], b["rows"], :] = o_t.T
            lg_ref[b["gi"], b["rows"], :] = l_t.T
        return carry

    lax.fori_loop(0, n_blocks // per_iter, body, 0)

    lse = lg_ref[...]
    mx = jnp.max(lse, axis=0, keepdims=True)
    w = jnp.exp2(lse - mx)
    alpha = w / jnp.sum(w, axis=0, keepdims=True)
    o_ref[0, 0] = jnp.sum(alpha * og_ref[...], axis=0).astype(o_ref.dtype)


def _dilated_attention(projs):
    bsz, _, t_len, _ = projs[0].shape
    npairs = C_HEADS * C_DH // LANE
    ng = len(C_PATTERNS)

    def spec(col):
        return pl.BlockSpec((1, 1, t_len, LANE), lambda b, hp, col=col: (b, col * npairs + hp, 0, 0))

    return pl.pallas_call(
        functools.partial(_dil_body, t_len=t_len),
        out_shape=jax.ShapeDtypeStruct((bsz, npairs, t_len, LANE), BF16),
        grid=(bsz, npairs),
        in_specs=[spec(c) for _ in range(ng) for c in range(3)],
        out_specs=pl.BlockSpec((1, 1, t_len, LANE), lambda b, hp: (b, hp, 0, 0)),
        scratch_shapes=[pltpu.VMEM((ng, t_len, LANE), F32), pltpu.VMEM((ng, t_len, LANE), F32)],
        compiler_params=_cparams(("parallel", "parallel")),
        name="dilated_attention",
    )(*[projs[gi] for gi in range(ng) for _ in range(3)])


def _dilated_layer(h, g, w_in, w_out):
    bsz, t_len, d = h.shape
    gw = C_HEADS * C_DH
    scale = jnp.ones((3, 1), F32).at[0, :].set(C_DH ** -0.5 * LOG2E)
    w_c = w_in.reshape(d, len(C_PATTERNS), 3, gw)
    projs = []
    for gi, (_, dil) in enumerate(C_PATTERNS):
        w_g = (w_c[:, gi] * scale).reshape(d, 3 * gw).astype(BF16)
        p = _norm_matmul(h, g, w_g, out_dtype=BF16, tm=512, tn=3 * gw, slab=True, dil=dil,
                         name=f"dilated_in{gi}")
        projs.append(p.reshape(bsz, 3 * gw // LANE, t_len, LANE))
    return _dilated_attention(projs), w_out.astype(BF16)


def kernel(x, norm_mix, norm_ffn, norm_final, a_w_in, a_b_gate, a_conv_w, a_conv_b, a_wq, a_wk, a_wv, a_gn, a_skip, a_w_out, b_w_in, b_b_gate, b_pe_k, b_pe_v, b_phik_w1, b_phik_b1, b_phik_w2, b_phiv_w1, b_phiv_b1, b_phiv_w2, b_w_out, c_w_in, c_w_out, f_w_up, f_conv_w, f_conv_b, f_w_down):
    depth = norm_mix.shape[0]
    h = x
    for i in range(depth):
        kind, j = i % 3, i // 3
        mix = None
        if kind == 0:
            h = _mlstm_layer(h, norm_mix[i], a_w_in[j], a_b_gate[j], a_conv_w[j], a_conv_b[j],
                             a_wq[j], a_wk[j], a_wv[j], a_gn[j], a_skip[j], a_w_out[j])
        elif kind == 1:
            h = _nsa_layer(h, norm_mix[i], b_w_in[j], b_b_gate[j], b_pe_k[j], b_pe_v[j],
                           b_phik_w1[j], b_phik_b1[j], b_phik_w2[j],
                           b_phiv_w1[j], b_phiv_b1[j], b_phiv_w2[j], b_w_out[j])
        else:
            mix = _dilated_layer(h, norm_mix[i], c_w_in[j], c_w_out[j])
        h = _conv_ffn(h, norm_ffn[i], f_w_up[i].astype(BF16), f_conv_w[i], f_conv_b[i],
                      f_w_down[i].astype(BF16), norm_final, tm=512,
                      final_norm=(i == depth - 1), name="conv_ffn", mix=mix)
    return h
```

```python
import functools

import jax
import jax.numpy as jnp
from jax import lax
from jax.experimental import pallas as pl
from jax.experimental.pallas import tpu as pltpu

F32 = jnp.float32
BF16 = jnp.bfloat16

NORM_EPS = 1e-6
NEG = -1e30
LOG2E = 1.4426950408889634
LANE = 128
SUBLANE = 8
VMEM_LIMIT = 56 * 1024 * 1024
NMM_COLS = 512
NMM_MAX_STRIDE = 4

A_HEADS = 4
A_CHUNK = 256
A_CONV = 4
B_HEADS = 16
B_DH = 64
B_GROUPS = 4
B_QPG = 4
B_CMP_LEN = 32
B_CMP_STRIDE = 16
B_SEL_LEN = 64
B_TOPN = 16
B_N_LOCAL = 2
B_WINDOW = 512
FORCE_SCORE = 1e9
B_TQ = 256
B_TK = 256
C_PATTERNS = ((128, 1), (512, 4), (2048, 16))
C_HEADS = 16
C_DH = 64
C_QBLK = 128
C_BLOCKS_PER_ITER = 16
F_CHUNK = 256
F_SPLIT = 2


def _cparams(sem):
    return pltpu.CompilerParams(dimension_semantics=sem, vmem_limit_bytes=VMEM_LIMIT)


def _const_spec(shape):
    nd = len(shape)
    return pl.BlockSpec(shape, lambda *_: (0,) * nd, pipeline_mode=pl.Buffered(1))


def _rmsnorm_rows(x, g):
    ms = jnp.mean(x * x, axis=-1, keepdims=True)
    return x * lax.rsqrt(ms + NORM_EPS) * g


def _sigmoid(x):
    return 1.0 / (1.0 + jnp.exp(-x))


def _nmm_body(x_ref, g_ref, w_ref, o_ref, xn_ref, *rest, dil):
    @pl.when(pl.program_id(2) == 0)
    def _():
        xn_ref[...] = _rmsnorm_rows(x_ref[0], g_ref[...]).astype(BF16)

    tm, tn = xn_ref.shape[0], w_ref.shape[1]
    step = min(tn, NMM_COLS)
    for c0 in range(0, tn, step):
        r = jnp.dot(xn_ref[...], w_ref[:, c0:c0 + step], preferred_element_type=F32)
        for cl in range(step // LANE):
            c = c0 // LANE + cl
            piece = r[:, cl * LANE:(cl + 1) * LANE]
            if dil == 1:
                o_ref[0, c, 0] = piece.astype(o_ref.dtype)
            else:
                r_ref, t_ref = rest
                r_ref[c] = piece
                if dil <= NMM_MAX_STRIDE:
                    for rc in range(dil):
                        o_ref[0, c, rc] = r_ref[c, pl.ds(rc, tm // dil, stride=dil),
                                                :].astype(o_ref.dtype)
                else:
                    d1 = NMM_MAX_STRIDE
                    d2 = dil // d1
                    run = tm // d1
                    for a in range(d1):
                        t_ref[a * run:(a + 1) * run, :] = r_ref[c, pl.ds(a, run, stride=d1), :]
                    for a in range(d1):
                        for j in range(d2):
                            o_ref[0, c, d1 * j + a] = t_ref[
                                pl.ds(a * run + j, run // d2, stride=d2), :].astype(o_ref.dtype)


def _norm_matmul(h, g, w, *, out_dtype, tm, tn, dil=1, name):
    bsz, t_len, d = h.shape
    n = w.shape[1]
    assert t_len % tm == 0 and n % tn == 0 and tn % LANE == 0 and tm % (2 * SUBLANE * dil) == 0
    scratch = [pltpu.VMEM((tm, d), BF16)]
    out_shape = jax.ShapeDtypeStruct((bsz, n // LANE, dil, t_len // dil, LANE), out_dtype)
    out_spec = pl.BlockSpec((1, tn // LANE, dil, tm // dil, LANE), lambda b, i, j: (b, j, 0, i, 0))
    if dil > 1:
        assert dil <= NMM_MAX_STRIDE or dil % NMM_MAX_STRIDE == 0
        scratch += [pltpu.VMEM((tn // LANE, tm, LANE), F32), pltpu.VMEM((tm, LANE), F32)]
    return pl.pallas_call(
        functools.partial(_nmm_body, dil=dil),
        out_shape=out_shape,
        grid=(bsz, t_len // tm, n // tn),
        in_specs=[
            pl.BlockSpec((1, tm, d), lambda b, i, j: (b, i, 0)),
            pl.BlockSpec((1, d), lambda b, i, j: (0, 0)),
            _const_spec((d, n)) if tn == n else pl.BlockSpec((d, tn), lambda b, i, j: (0, j)),
        ],
        out_specs=out_spec,
        scratch_shapes=scratch,
        compiler_params=_cparams(("parallel", "parallel", "arbitrary")),
        name=name,
    )(h, g.reshape(1, d), w)


def _ffn_body(h_ref, hp_ref, g_ref, wup_ref, cw_ref, cb_ref, wdn_ref, gf_ref, *rest,
              tm, f_hidden, final_norm, mixed):
    halo = SUBLANE
    g = g_ref[...]
    if mixed:
        ox_ref, oxp_ref, wmix_ref, o_ref, xn_ref, acc_ref, hres_ref = rest
        slabs = range(ox_ref.shape[1])
        o_cur = jnp.concatenate([ox_ref[0, c] for c in slabs], axis=-1)
        o_prev = jnp.concatenate([oxp_ref[0, c] for c in slabs], axis=-1)
        hres_ref[...] = h_ref[0] + jnp.dot(o_cur, wmix_ref[...], preferred_element_type=F32)
        h_cur = hres_ref[...]
        h_prev = hp_ref[0] + jnp.dot(o_prev, wmix_ref[...],
                                     preferred_element_type=F32)[halo:2 * halo]
    else:
        o_ref, xn_ref, acc_ref = rest
        h_cur, h_prev = h_ref[0], hp_ref[0]
    keep = (pl.program_id(1) > 0).astype(F32)
    xn_ref[0:halo, :] = (_rmsnorm_rows(h_prev, g) * keep).astype(BF16)
    xn_ref[halo:, :] = _rmsnorm_rows(h_cur, g).astype(BF16)
    rows = tm // F_SPLIT
    xns = [xn_ref[s * rows:s * rows + rows + halo, :] for s in range(F_SPLIT)]

    def conv(u, col0):
        w = cw_ref[:, col0:col0 + F_CHUNK]
        b = cb_ref[:, col0:col0 + F_CHUNK]
        return (w[2:3] * u[halo:halo + rows] + w[1:2] * u[halo - 1:halo - 1 + rows]
                + w[0:1] * u[halo - 2:halo - 2 + rows] + b)

    def up(c):
        c0 = c * F_CHUNK
        return [(jnp.dot(xn, wup_ref[:, c0:c0 + F_CHUNK], preferred_element_type=F32),
                 jnp.dot(xn, wup_ref[:, f_hidden + c0:f_hidden + c0 + F_CHUNK],
                         preferred_element_type=F32)) for xn in xns]

    n_chunks = f_hidden // F_CHUNK
    nxt = up(0)
    accs = [None] * F_SPLIT
    for c in range(n_chunks):
        c0 = c * F_CHUNK
        cur = nxt
        if c + 1 < n_chunks:
            nxt = up(c + 1)
        for s, (ug, uv) in enumerate(cur):
            gate = conv(ug, c0)
            val = conv(uv, f_hidden + c0)
            act = (gate * _sigmoid(gate) * val).astype(BF16)
            part = jnp.dot(act, wdn_ref[c0:c0 + F_CHUNK, :], preferred_element_type=F32)
            accs[s] = part if c == 0 else accs[s] + part
    for s in range(F_SPLIT):
        acc_ref[s * rows:(s + 1) * rows, :] = accs[s]

    out = (hres_ref[...] if mixed else h_ref[0]) + acc_ref[...]
    if final_norm:
        out = _rmsnorm_rows(out, gf_ref[...])
    o_ref[0] = out


def _conv_ffn(h, g, w_up, conv_w, conv_b, w_down, g_final, *, tm, final_norm, name, mix=None):
    bsz, t_len, d = h.shape
    f2 = w_up.shape[1]
    f_hidden = f2 // 2
    assert f_hidden % F_CHUNK == 0 and t_len % tm == 0 and tm % (2 * SUBLANE) == 0
    rows_per_tile = tm // SUBLANE
    in_specs = [
        pl.BlockSpec((1, tm, d), lambda b, i: (b, i, 0)),
        pl.BlockSpec((1, SUBLANE, d), lambda b, i: (b, jnp.maximum(i * rows_per_tile - 1, 0), 0)),
        _const_spec((1, d)),
        _const_spec((d, f2)),
        _const_spec((3, f2)),
        _const_spec((1, f2)),
        _const_spec((f_hidden, d)),
        _const_spec((1, d)),
    ]
    args = [h, h, g.reshape(1, d), w_up, conv_w, conv_b.reshape(1, f2), w_down,
            g_final.reshape(1, d)]
    scratch = [pltpu.VMEM((tm + SUBLANE, d), BF16), pltpu.VMEM((tm, d), F32)]
    if mix is not None:
        o, w_mix = mix
        n_slab = o.shape[1]
        in_specs += [
            pl.BlockSpec((1, n_slab, tm, LANE), lambda b, i: (b, 0, i, 0)),
            pl.BlockSpec((1, n_slab, 2 * SUBLANE, LANE),
                         lambda b, i: (b, 0, jnp.maximum(i * (rows_per_tile // 2) - 1, 0), 0)),
            _const_spec(w_mix.shape),
        ]
        args += [o, o, w_mix]
        scratch.append(pltpu.VMEM((tm, d), F32))
    return pl.pallas_call(
        functools.partial(_ffn_body, tm=tm, f_hidden=f_hidden, final_norm=final_norm,
                          mixed=mix is not None),
        out_shape=jax.ShapeDtypeStruct((bsz, t_len, d), F32),
        grid=(bsz, t_len // tm),
        in_specs=in_specs,
        out_specs=pl.BlockSpec((1, tm, d), lambda b, i: (b, i, 0)),
        scratch_shapes=scratch,
        compiler_params=_cparams(("parallel", "parallel")),
        name=name,
    )(*args)


def _mlstm_body(h_ref, g_ref, win_ref, bg_ref, cw_ref, cb_ref, wq_ref, wk_ref, wv_ref,
                gn_ref, sk_ref, wout_ref, o_ref, c_ref, n_ref, m_ref, xp_ref, y_ref,
                *, chunk, dh):
    L = chunk
    halo = SUBLANE
    inner = A_HEADS * dh

    @pl.when(pl.program_id(1) == 0)
    def _():
        c_ref[...] = jnp.zeros_like(c_ref)
        n_ref[...] = jnp.zeros_like(n_ref)
        m_ref[...] = jnp.zeros_like(m_ref)
        xp_ref[...] = jnp.zeros_like(xp_ref)

    xn = _rmsnorm_rows(h_ref[0], g_ref[...]).astype(BF16)
    xm = jnp.dot(xn, win_ref[:, 0:inner], preferred_element_type=F32)
    gates = jnp.dot(xn, win_ref[:, 2 * inner:], preferred_element_type=F32) + bg_ref[...]
    heads = [dict(lo=hd * dh, hi=(hd + 1) * dh) for hd in range(A_HEADS)]
    for hd, st in enumerate(heads):
        st["v"] = jnp.dot(xm[:, st["lo"]:st["hi"]].astype(BF16), wv_ref[hd],
                          preferred_element_type=F32)
    o_pre = jnp.dot(xn, win_ref[:, inner:2 * inner], preferred_element_type=F32)

    xe = jnp.concatenate([xp_ref[...], xm], axis=0)
    cw = cw_ref[...]
    xc = (cw[3:4] * xe[halo:halo + L] + cw[2:3] * xe[halo - 1:halo - 1 + L]
          + cw[1:2] * xe[halo - 2:halo - 2 + L] + cw[0:1] * xe[halo - 3:halo - 3 + L]
          + cb_ref[...])
    xc = xc * _sigmoid(xc)
    xp_ref[...] = xm[L - halo:L]

    ii = lax.broadcasted_iota(jnp.int32, (L, L), 0)
    jj = lax.broadcasted_iota(jnp.int32, (L, L), 1)
    eye = ii == jj
    tril = jj <= ii

    def to_row(col):
        return jnp.sum(jnp.where(eye, col, 0.0), axis=0, keepdims=True)

    for hd, st in enumerate(heads):
        st["xc"] = xc[:, st["lo"]:st["hi"]]
        xcb = st["xc"].astype(BF16)
        st["q"] = jnp.dot(xcb, wq_ref[hd], preferred_element_type=F32)
        st["k"] = jnp.dot(xcb, wk_ref[hd], preferred_element_type=F32)

    for hd, st in enumerate(heads):
        li = gates[:, hd:hd + 1]
        fpre = gates[:, A_HEADS + hd:A_HEADS + hd + 1]
        lf = jnp.minimum(fpre, 0.0) - jnp.log1p(jnp.exp(-jnp.abs(fpre)))
        li_row = to_row(li)
        lf_row = to_row(lf)
        b_col = jnp.sum(jnp.where(tril, lf_row, 0.0), axis=1, keepdims=True)
        b_row = to_row(b_col)
        m_st = m_ref[hd][:, 0:1]
        log_d = jnp.where(tril, b_col - b_row + li_row, NEG)
        log_inter = b_col + m_st
        m = jnp.maximum(log_inter, jnp.max(log_d, axis=1, keepdims=True))
        st["m"] = m
        st["w_intra"] = jnp.exp(log_d - m)
        st["w_inter"] = jnp.exp(log_inter - m)
        b_last = b_col[L - 1:L, :]
        log_w = b_last - b_col + li
        m_new = jnp.maximum(b_last + m_st, jnp.max(log_w, axis=0, keepdims=True))
        st["decay"] = jnp.exp(b_last + m_st - m_new)
        st["w_k"] = jnp.exp(log_w - m_new)
        m_ref[hd] = jnp.broadcast_to(m_new, m_ref.shape[1:])

    for hd, st in enumerate(heads):
        st["k"] = st["k"] * (dh ** -0.5)
        st["qb"] = st["q"].astype(BF16)
        st["vb"] = st["v"].astype(BF16)
        st["s"] = lax.dot_general(st["qb"], st["k"].astype(BF16), (((1,), (1,)), ((), ())),
                                  preferred_element_type=F32)
        st["qc"] = jnp.dot(st["qb"], c_ref[hd].astype(BF16), preferred_element_type=F32)

    for hd, st in enumerate(heads):
        s = st["s"] * st["w_intra"]
        st["den"] = (st["w_inter"] * jnp.sum(st["q"] * n_ref[hd], axis=1, keepdims=True)
                     + jnp.sum(s, axis=1, keepdims=True))
        wk = st["w_k"] * st["k"]
        n_ref[hd] = st["decay"] * n_ref[hd] + jnp.sum(wk, axis=0, keepdims=True)
        st["sv"] = jnp.dot(s.astype(BF16), st["vb"], preferred_element_type=F32)
        st["kv"] = lax.dot_general(wk.astype(BF16), st["vb"], (((0,), (0,)), ((), ())),
                                   preferred_element_type=F32)

    for hd, st in enumerate(heads):
        lo, hi = st["lo"], st["hi"]
        c_ref[hd] = st["decay"] * c_ref[hd] + st["kv"]
        num = st["w_inter"] * st["qc"] + st["sv"]
        hh = num / jnp.maximum(jnp.abs(st["den"]), jnp.exp(-st["m"]))
        mu = jnp.mean(hh, axis=-1, keepdims=True)
        ctr = hh - mu
        var = jnp.mean(ctr * ctr, axis=-1, keepdims=True)
        hn = ctr * lax.rsqrt(var + NORM_EPS) * gn_ref[:, lo:hi]
        y = _sigmoid(o_pre[:, lo:hi]) * hn + sk_ref[:, lo:hi] * st["xc"]
        y_ref[:, lo:hi] = y.astype(y_ref.dtype)

    o_ref[0] = h_ref[0] + jnp.dot(y_ref[...], wout_ref[...], preferred_element_type=F32)


def _mlstm_layer(h, g, w_in, b_gate, conv_w, conv_b, wq, wk, wv, gn, skip, w_out):
    bsz, t_len, d = h.shape
    inner = w_out.shape[0]
    dh = inner // A_HEADS
    L = A_CHUNK
    w_a = w_in.astype(BF16)
    bg = b_gate.reshape(1, 2 * A_HEADS)
    return pl.pallas_call(
        functools.partial(_mlstm_body, chunk=L, dh=dh),
        out_shape=jax.ShapeDtypeStruct((bsz, t_len, d), F32),
        grid=(bsz, t_len // L),
        in_specs=[
            pl.BlockSpec((1, L, d), lambda b, c: (b, c, 0)),
            _const_spec((1, d)),
            _const_spec((d, 2 * inner + 2 * A_HEADS)),
            _const_spec((1, 2 * A_HEADS)),
            _const_spec((A_CONV, inner)),
            _const_spec((1, inner)),
            _const_spec((A_HEADS, dh, dh)),
            _const_spec((A_HEADS, dh, dh)),
            _const_spec((A_HEADS, dh, dh)),
            _const_spec((1, inner)),
            _const_spec((1, inner)),
            _const_spec((inner, d)),
        ],
        out_specs=pl.BlockSpec((1, L, d), lambda b, c: (b, c, 0)),
        scratch_shapes=[
            pltpu.VMEM((A_HEADS, dh, dh), F32),
            pltpu.VMEM((A_HEADS, 1, dh), F32),
            pltpu.VMEM((A_HEADS, 1, LANE), F32),
            pltpu.VMEM((SUBLANE, inner), F32),
            pltpu.VMEM((L, inner), BF16),
        ],
        compiler_params=_cparams(("parallel", "arbitrary")),
        name="mlstm_layer",
    )(h, g.reshape(1, d), w_a, bg, conv_w, conv_b.reshape(1, inner), wq.astype(BF16),
      wk.astype(BF16), wv.astype(BF16), gn.reshape(1, inner), skip.reshape(1, inner),
      w_out.astype(BF16))


def _cmp_body(x_ref, pe_ref, w1_ref, b1_ref, w2_ref, o_ref):
    half = x_ref.shape[-1]
    x = x_ref[0, 0, 0]
    pe = pe_ref[0]
    a = (x + pe[:, :half]).astype(BF16)
    b = (x + pe[:, half:]).astype(BF16)
    w1 = w1_ref[0]
    pa = jnp.dot(a, w1[:half], preferred_element_type=F32)
    pb = jnp.dot(b, w1[half:], preferred_element_type=F32)
    pb_next = jnp.concatenate([pb[1:], jnp.zeros((1, pb.shape[1]), F32)], axis=0)
    pre = pa + pb_next + b1_ref[0]
    hid = (pre * _sigmoid(pre)).astype(BF16)
    out = jnp.dot(hid, w2_ref[0], preferred_element_type=F32)
    row = lax.broadcasted_iota(jnp.int32, out.shape, 0)
    o_ref[0, 0, 0] = jnp.where(row < out.shape[0] - 1, out, 0.0).astype(o_ref.dtype)


def _nsa_compress(kv_cmp, pe, w1, b1, w2):
    two, bsz, g, t_len, dh = kv_cmp.shape
    n_half = t_len // B_CMP_STRIDE
    half = B_CMP_STRIDE * dh
    x = kv_cmp.reshape(two, bsz, g, n_half, half)
    hidden = w1.shape[-1]
    return pl.pallas_call(
        _cmp_body,
        out_shape=jax.ShapeDtypeStruct((two, bsz, g, n_half, dh), BF16),
        grid=(two, bsz, g),
        in_specs=[
            pl.BlockSpec((1, 1, 1, n_half, half), lambda s, b, gi: (s, b, gi, 0, 0)),
            pl.BlockSpec((1, 1, 2 * half), lambda s, b, gi: (s, 0, 0)),
            pl.BlockSpec((1, 2 * half, hidden), lambda s, b, gi: (s, 0, 0)),
            pl.BlockSpec((1, 1, hidden), lambda s, b, gi: (s, 0, 0)),
            pl.BlockSpec((1, hidden, dh), lambda s, b, gi: (s, 0, 0)),
        ],
        out_specs=pl.BlockSpec((1, 1, 1, n_half, dh), lambda s, b, gi: (s, b, gi, 0, 0)),
        compiler_params=_cparams(("parallel", "parallel", "parallel")),
        name="nsa_compress",
    )(x, pe.reshape(two, 1, 2 * half), w1, b1.reshape(two, 1, hidden), w2)


def _flash_t(q_ts, k_ref, vt_ref, acc_ref, m_ref, kt_lo, kt_hi, bias_fn, last_bias_fn, tk):
    acc_ref[...] = jnp.zeros_like(acc_ref)
    m_ref[...] = jnp.full_like(m_ref, NEG)
    groups = range(len(q_ts))

    def tile(kt, bias_of):
        ks = pl.multiple_of(kt * tk, tk)
        ss = [jnp.dot(k_ref[0, g, pl.ds(ks, tk), :], q_ts[g], preferred_element_type=F32)
              for g in groups]
        bias = None if bias_of is None else bias_of(kt, ks)
        ps, alphas = [], []
        for g in groups:
            s = ss[g] if bias is None else ss[g] + bias
            m = m_ref[g]
            m_new = jnp.maximum(m, jnp.max(s, axis=0, keepdims=True))
            alphas.append(jnp.exp2(m - m_new))
            ps.append(jnp.exp2(s - m_new).astype(BF16))
            m_ref[g] = m_new
        for g in groups:
            v_t = vt_ref[0, g, :, pl.ds(ks, tk)]
            acc_ref[g] = alphas[g] * acc_ref[g] + jnp.dot(v_t, ps[g],
                                                          preferred_element_type=F32)

    def body(kt, carry):
        tile(kt, bias_fn)
        return carry

    lax.fori_loop(kt_lo, kt_hi - 1, body, 0)
    tile(kt_hi - 1, last_bias_fn)
    dh = acc_ref.shape[1] - 16
    return [acc_ref[g, 0:dh, :] / acc_ref[g, dh:dh + 1, :] for g in groups]


def _nsa_body(q_ref, kc_ref, vc_ref, ks_ref, vs_ref, kw_ref, vw_ref, gp_ref, bg_ref,
              ovl_ref, wout_ref, h_ref, o_ref, acc_ref, m_ref, og_ref, *, tq, tk, t_len):
    qi = pl.program_id(1)
    t0 = qi * tq
    gw = B_QPG * B_DH
    ncp = kc_ref.shape[2]
    nsel = t_len // B_SEL_LEN
    n_top = min(B_TOPN, nsel)

    def lanes4(x):
        return jnp.concatenate([x] * B_QPG, axis=1)

    n_idx = lax.broadcasted_iota(jnp.int32, (ncp, tq), 0)
    t_col = t0 + lax.broadcasted_iota(jnp.int32, (ncp, tq), 1)
    cmask = (n_idx * B_CMP_STRIDE + (B_CMP_LEN - 1) <= t_col) & (n_idx < ncp - 1)
    cbias = lanes4(jnp.where(cmask, 0.0, NEG))
    ckeep = lanes4(cmask.astype(F32))
    blk = lax.broadcasted_iota(jnp.int32, (nsel, tq), 0)
    cur = (t0 + lax.broadcasted_iota(jnp.int32, (nsel, tq), 1)) // B_SEL_LEN
    valid = blk <= cur
    forced = ((blk == 0) | (cur - blk < B_N_LOCAL)) & valid

    q_ts, q_sels, o_cmps = [], [], []
    for g in range(B_GROUPS):
        q_full = q_ref[0, :, g * gw:(g + 1) * gw].astype(F32).T.astype(BF16)
        q_t = jnp.concatenate([q_full[r * B_DH:(r + 1) * B_DH, :] for r in range(B_QPG)], axis=1)
        q_ts.append(q_t)

        s = jnp.dot(kc_ref[0, g], q_t, preferred_element_type=F32) + cbias
        m = jnp.max(s, axis=0, keepdims=True)
        e = jnp.exp2(s - m) * ckeep
        den = jnp.sum(e, axis=0, keepdims=True)
        p = e / jnp.where(den > 0, den, 1.0)
        o_cmps.append(lax.dot_general(vc_ref[0, g], p.astype(BF16), (((0,), (0,)), ((), ())),
                                      preferred_element_type=F32))

        psum = p[:, 0:tq]
        for r in range(1, B_QPG):
            psum = psum + p[:, r * tq:(r + 1) * tq]
        p_hi = psum.astype(BF16)
        r1 = psum - p_hi.astype(F32)
        p_mid = r1.astype(BF16)
        p_lo = (r1 - p_mid.astype(F32)).astype(BF16)
        ovl = ovl_ref[...]
        imp = (jnp.dot(ovl, p_hi, preferred_element_type=F32)
               + jnp.dot(ovl, p_mid, preferred_element_type=F32)
               + jnp.dot(ovl, p_lo, preferred_element_type=F32))
        imp = imp[0:nsel]
        score = jnp.where(forced, FORCE_SCORE, jnp.where(valid, imp, -FORCE_SCORE))
        slabs = [score[r0:r0 + SUBLANE] for r0 in range(0, nsel, SUBLANE)]
        ranks = [jnp.zeros((SUBLANE, tq), F32) for _ in slabs]
        row = lax.broadcasted_iota(jnp.int32, (SUBLANE, tq), 0)
        for j in range(nsel):
            sj = score[j:j + 1, :]
            for si, sc in enumerate(slabs):
                r0 = si * SUBLANE
                ge = jnp.where(sj >= sc, 1.0, 0.0)
                gt = jnp.where(sj > sc, 1.0, 0.0)
                if r0 > j:
                    ahead = ge
                elif r0 + SUBLANE - 1 < j:
                    ahead = gt
                else:
                    ahead = jnp.where(row + r0 > j, ge, gt)
                ranks[si] = ranks[si] + ahead
        rank = jnp.concatenate(ranks, axis=0)
        sel_bias = jnp.where((rank < n_top) & valid, 0.0, NEG)
        q_sels.append(jnp.concatenate(
            [q_t, lanes4(sel_bias).astype(BF16),
             jnp.zeros((LANE - B_DH - nsel, B_QPG * tq), BF16)], axis=0))

    k_off = lax.broadcasted_iota(jnp.int32, (tk, tq), 0)
    t_pos = t0 + lax.broadcasted_iota(jnp.int32, (tk, tq), 1)

    def causal_bias(kt, ks):
        return lanes4(jnp.where(ks + k_off <= t_pos, 0.0, NEG))

    kt_hi = (t0 + tq + tk - 1) // tk
    o_sels = _flash_t(q_sels, ks_ref, vs_ref, acc_ref, m_ref, 0, kt_hi, None, causal_bias, tk)

    def win_bias(kt, ks):
        diff = t_pos - (ks + k_off)
        return lanes4(jnp.where((diff >= 0) & (diff <= B_WINDOW - 1), 0.0, NEG))

    w_lo = jnp.maximum(t0 - (B_WINDOW - 1), 0) // tk
    o_wins = _flash_t(q_ts, kw_ref, vw_ref, acc_ref, m_ref, w_lo, kt_hi, win_bias, win_bias, tk)

    for g in range(B_GROUPS):
        gate = _sigmoid(gp_ref[0, g] + bg_ref[g])
        outs = []
        for r in range(B_QPG):
            sl = slice(r * tq, (r + 1) * tq)
            outs.append(gate[r:r + 1, :] * o_cmps[g][:, sl]
                        + gate[B_QPG + r:B_QPG + r + 1, :] * o_sels[g][:, sl]
                        + gate[2 * B_QPG + r:2 * B_QPG + r + 1, :] * o_wins[g][:, sl])
        og_ref[:, g * gw:(g + 1) * gw] = jnp.concatenate(outs, axis=0).T.astype(og_ref.dtype)

    o_ref[0] = h_ref[0] + jnp.dot(og_ref[...], wout_ref[...], preferred_element_type=F32)


def _nsa_attention(q, kc, vc, k_sel, vt_sel, k_win, vt_win, gpre_t, bg, w_out, h):
    bsz, t_len, _ = q.shape
    d = h.shape[-1]
    tq, tk = B_TQ, B_TK
    ncp = kc.shape[2]
    nsel = t_len // B_SEL_LEN
    gw = B_QPG * B_DH
    c_start = jnp.arange(ncp)[None, :] * B_CMP_STRIDE
    s_start = jnp.arange(LANE)[:, None] * B_SEL_LEN
    ovl = ((c_start < s_start + B_SEL_LEN) & (c_start + B_CMP_LEN > s_start)
           & (jnp.arange(LANE)[:, None] < nsel) & (jnp.arange(ncp)[None, :] < ncp - 1)).astype(BF16)

    def k_spec(n_rows, width=B_DH):
        return pl.BlockSpec((1, B_GROUPS, n_rows, width), lambda b, i: (b, 0, 0, 0))

    vt_spec = pl.BlockSpec((1, B_GROUPS, B_DH + 16, t_len), lambda b, i: (b, 0, 0, 0))
    n_q = B_QPG * tq

    return pl.pallas_call(
        functools.partial(_nsa_body, tq=tq, tk=tk, t_len=t_len),
        out_shape=jax.ShapeDtypeStruct((bsz, t_len, d), F32),
        grid=(bsz, t_len // tq),
        in_specs=[
            pl.BlockSpec((1, tq, B_GROUPS * gw), lambda b, i: (b, i, 0)),
            k_spec(ncp), k_spec(ncp), k_spec(t_len, LANE), vt_spec, k_spec(t_len), vt_spec,
            pl.BlockSpec((1, B_GROUPS, 16, tq), lambda b, i: (b, 0, 0, i)),
            _const_spec((B_GROUPS, 16, tq)),
            _const_spec((LANE, ncp)),
            _const_spec((B_GROUPS * gw, d)),
            pl.BlockSpec((1, tq, d), lambda b, i: (b, i, 0)),
        ],
        out_specs=pl.BlockSpec((1, tq, d), lambda b, i: (b, i, 0)),
        scratch_shapes=[pltpu.VMEM((B_GROUPS, B_DH + 16, n_q), F32),
                        pltpu.VMEM((B_GROUPS, 1, n_q), F32),
                        pltpu.VMEM((tq, B_GROUPS * gw), BF16)],
        compiler_params=_cparams(("parallel", "parallel")),
        name="nsa_attention",
    )(q, kc, vc, k_sel, vt_sel, k_win, vt_win, gpre_t, bg, ovl, w_out, h)


def _nsa_in_body(x_ref, g_ref, w_ref, q_ref, kvc_ref, ks_ref, vs_ref, kw_ref, vw_ref, gp_ref):
    nq = B_HEADS * B_DH
    gw = B_GROUPS * B_DH
    xn = _rmsnorm_rows(x_ref[0], g_ref[...]).astype(BF16)
    q_ref[0] = jnp.dot(xn, w_ref[:, 0:nq], preferred_element_type=F32).astype(q_ref.dtype)

    def seg(i):
        return jnp.dot(xn, w_ref[:, nq + i * gw:nq + (i + 1) * gw], preferred_element_type=F32)

    def put_rows(ref, lead, r):
        for gi in range(B_GROUPS):
            ref[lead + (gi,)] = r[:, gi * B_DH:(gi + 1) * B_DH].astype(ref.dtype)

    def put_cols(ref, r, width, ones_rows=0):
        r_t = r.T
        tm = r.shape[0]
        for gi in range(B_GROUPS):
            ref[0, gi, 0:width, :] = r_t[gi * width:(gi + 1) * width, :].astype(ref.dtype)
            if ones_rows:
                ref[0, gi, width:width + ones_rows, :] = jnp.ones((ones_rows, tm), ref.dtype)

    put_rows(kvc_ref, (0, 0), seg(0))
    put_rows(kvc_ref, (1, 0), seg(1))
    r = seg(2)
    tm = r.shape[0]
    blk = (pl.program_id(1) * tm + lax.broadcasted_iota(jnp.int32, (tm, LANE - B_DH), 0)) // B_SEL_LEN
    onehot = (blk == lax.broadcasted_iota(jnp.int32, (tm, LANE - B_DH), 1)).astype(F32)
    for gi in range(B_GROUPS):
        ks_ref[0, gi] = jnp.concatenate([r[:, gi * B_DH:(gi + 1) * B_DH], onehot],
                                        axis=1).astype(ks_ref.dtype)
    put_cols(vs_ref, seg(3), B_DH, ones_rows=16)
    put_rows(kw_ref, (0,), seg(4))
    put_cols(vw_ref, seg(5), B_DH, ones_rows=16)
    gates = jnp.dot(xn, w_ref[:, nq + 6 * gw:], preferred_element_type=F32)
    put_cols(gp_ref, gates, 16)


def _nsa_in(h, g, w):
    bsz, t_len, d = h.shape
    tm = 512
    nq = B_HEADS * B_DH
    G = B_GROUPS
    sds = jax.ShapeDtypeStruct
    k_shape = sds((bsz, G, t_len, B_DH), BF16)
    vt_shape = sds((bsz, G, B_DH + 16, t_len), BF16)
    k_spec = pl.BlockSpec((1, G, tm, B_DH), lambda b, i: (b, 0, i, 0))
    vt_spec = pl.BlockSpec((1, G, B_DH + 16, tm), lambda b, i: (b, 0, 0, i))
    return pl.pallas_call(
        _nsa_in_body,
        out_shape=(sds((bsz, t_len, nq), BF16), sds((2, bsz, G, t_len, B_DH), F32),
                   sds((bsz, G, t_len, LANE), BF16), vt_shape, k_shape, vt_shape,
                   sds((bsz, G, 16, t_len), F32)),
        grid=(bsz, t_len // tm),
        in_specs=[pl.BlockSpec((1, tm, d), lambda b, i: (b, i, 0)),
                  _const_spec((1, d)), _const_spec(w.shape)],
        out_specs=(pl.BlockSpec((1, tm, nq), lambda b, i: (b, i, 0)),
                   pl.BlockSpec((2, 1, G, tm, B_DH), lambda b, i: (0, b, 0, i, 0)),
                   pl.BlockSpec((1, G, tm, LANE), lambda b, i: (b, 0, i, 0)), vt_spec,
                   k_spec, vt_spec,
                   pl.BlockSpec((1, G, 16, tm), lambda b, i: (b, 0, 0, i))),
        compiler_params=_cparams(("parallel", "parallel")),
        name="nsa_in",
    )(h, g.reshape(1, d), w)


def _nsa_layer(h, g, w_in, b_gate, pe_k, pe_v, phik_w1, phik_b1, phik_w2,
               phiv_w1, phiv_b1, phiv_w2, w_out):
    bsz, t_len, d = h.shape
    nq = B_HEADS * B_DH
    nkv = 6 * B_GROUPS * B_DH
    wg = w_in[:, nq + nkv:].reshape(d, B_GROUPS, B_QPG, 3).transpose(0, 1, 3, 2)
    wg = jnp.pad(wg.reshape(d, B_GROUPS, 3 * B_QPG), ((0, 0), (0, 0), (0, 16 - 3 * B_QPG)))
    wg = jnp.pad(wg.reshape(d, B_GROUPS * 16), ((0, 0), (0, LANE - B_GROUPS * 16)))
    bg = b_gate.reshape(B_GROUPS, B_QPG, 3).transpose(0, 2, 1).reshape(B_GROUPS, 3 * B_QPG)
    bg = jnp.pad(bg, ((0, 0), (0, 16 - 3 * B_QPG)))
    bg = jnp.broadcast_to(bg[:, :, None], (B_GROUPS, 16, B_TQ))
    w_b = jnp.concatenate([w_in[:, :nq] * (B_DH ** -0.5 * LOG2E), w_in[:, nq:nq + nkv], wg],
                          axis=1).astype(BF16)
    q, kv_cmp, k_sel, vt_sel, k_win, vt_win, gpre_t = _nsa_in(h, g, w_b)

    cmp = _nsa_compress(
        kv_cmp, jnp.stack([pe_k, pe_v]), jnp.stack([phik_w1, phiv_w1]).astype(BF16),
        jnp.stack([phik_b1, phiv_b1]), jnp.stack([phik_w2, phiv_w2]).astype(BF16))
    return _nsa_attention(q, cmp[0], cmp[1], k_sel, vt_sel, k_win, vt_win, gpre_t, bg,
                          w_out.astype(BF16), h)


def _dil_body(*refs, t_len):
    qkv = refs[:9]
    o_ref = refs[9]
    og_ref, lg_ref = refs[10], refs[11]
    qb = C_QBLK
    assert qb == LANE and 2 * C_DH == LANE
    top = lax.broadcasted_iota(jnp.int32, (LANE, qb), 0) < C_DH

    n_blocks = t_len // qb
    per_iter = C_BLOCKS_PER_ITER
    assert n_blocks % per_iter == 0

    def body(it, carry):
        blocks = []
        for gi, (window, dil) in enumerate(C_PATTERNS):
            sub = t_len // dil
            nb = sub // qb
            nk = qb if nb == 1 else 2 * qb
            for j in range(per_iter):
                i = it * per_iter + j
                rc = i // nb
                bq = i % nb
                bk = jnp.maximum(bq - 1, 0)
                q0 = pl.multiple_of(rc * sub + bq * qb, qb)
                k0 = pl.multiple_of(rc * sub + bk * qb, qb)
                if dil == 1:
                    rows = pl.ds(q0, qb)
                else:
                    rows = pl.ds(rc + bq * (qb * dil), qb, stride=dil)
                blocks.append(dict(gi=gi, nk=nk, k0=k0, rows=rows, shift=(bq - bk) * qb,
                                   max_dist=window // dil,
                                   q2=qkv[3 * gi][0, 0, pl.ds(q0, qb), :]))
        for b in blocks:
            q_t = b["q2"].astype(F32).T
            b["q_bd"] = jnp.concatenate([jnp.where(top, q_t, 0.0), jnp.where(top, 0.0, q_t)],
                                        axis=1).astype(BF16)
        for b in blocks:
            k2 = qkv[3 * b["gi"] + 1][0, 0, pl.ds(b["k0"], b["nk"]), :]
            b["s"] = jnp.dot(k2, b["q_bd"], preferred_element_type=F32)
        for b in blocks:
            nk = b["nk"]
            diff = (b["shift"] + lax.broadcasted_iota(jnp.int32, (nk, qb), 1)
                    - lax.broadcasted_iota(jnp.int32, (nk, qb), 0))
            bias = jnp.where((diff >= 0) & (diff <= b["max_dist"]), 0.0, NEG)
            s = b["s"] + jnp.concatenate([bias, bias], axis=1)
            m = jnp.max(s, axis=0, keepdims=True)
            e = jnp.exp2(s - m)
            den = jnp.sum(e, axis=0, keepdims=True)
            b["e"] = e.astype(BF16)
            b["inv"] = 1.0 / den
            b["lse"] = m + jnp.log2(den)
        for b in blocks:
            v2 = qkv[3 * b["gi"] + 2][0, 0, pl.ds(b["k0"], b["nk"]), :]
            b["pv"] = lax.dot_general(v2, b["e"], (((0,), (0,)), ((), ())),
                                      preferred_element_type=F32)
        for b in blocks:
            pv, inv, lse = b["pv"], b["inv"], b["lse"]
            o_t = jnp.concatenate([pv[0:C_DH, 0:qb] * inv[:, 0:qb],
                                   pv[C_DH:, qb:] * inv[:, qb:]], axis=0)
            l_t = jnp.concatenate([jnp.broadcast_to(lse[:, 0:qb], (C_DH, qb)),
                                   jnp.broadcast_to(lse[:, qb:], (C_DH, qb))], axis=0)
            og_ref[b["gi"], b["rows"], :] = o_t.T
            lg_ref[b["gi"], b["rows"], :] = l_t.T
        return carry

    lax.fori_loop(0, n_blocks // per_iter, body, 0)

    lse = lg_ref[...]
    mx = jnp.max(lse, axis=0, keepdims=True)
    w = jnp.exp2(lse - mx)
    alpha = w / jnp.sum(w, axis=0, keepdims=True)
    o_ref[0, 0] = jnp.sum(alpha * og_ref[...], axis=0).astype(o_ref.dtype)


def _dilated_attention(projs):
    bsz, _, t_len, _ = projs[0].shape
    npairs = C_HEADS * C_DH // LANE
    ng = len(C_PATTERNS)

    def spec(col):
        return pl.BlockSpec((1, 1, t_len, LANE), lambda b, hp, col=col: (b, col * npairs + hp, 0, 0))

    return pl.pallas_call(
        functools.partial(_dil_body, t_len=t_len),
        out_shape=jax.ShapeDtypeStruct((bsz, npairs, t_len, LANE), BF16),
        grid=(bsz, npairs),
        in_specs=[spec(c) for _ in range(ng) for c in range(3)],
        out_specs=pl.BlockSpec((1, 1, t_len, LANE), lambda b, hp: (b, hp, 0, 0)),
        scratch_shapes=[pltpu.VMEM((ng, t_len, LANE), F32), pltpu.VMEM((ng, t_len, LANE), F32)],
        compiler_params=_cparams(("parallel", "parallel")),
        name="dilated_attention",
    )(*[projs[gi] for gi in range(ng) for _ in range(3)])


def _dilated_layer(h, g, w_in, w_out):
    bsz, t_len, d = h.shape
    gw = C_HEADS * C_DH
    scale = jnp.ones((3, 1), F32).at[0, :].set(C_DH ** -0.5 * LOG2E)
    w_c = w_in.reshape(d, len(C_PATTERNS), 3, gw)
    projs = []
    for gi, (_, dil) in enumerate(C_PATTERNS):
        w_g = (w_c[:, gi] * scale).reshape(d, 3 * gw).astype(BF16)
        p = _norm_matmul(h, g, w_g, out_dtype=BF16, tm=512, tn=3 * gw, dil=dil,
                         name=f"dilated_in{gi}")
        projs.append(p.reshape(bsz, 3 * gw // LANE, t_len, LANE))
    return _dilated_attention(projs), w_out.astype(BF16)


def kernel(x, norm_mix, norm_ffn, norm_final, a_w_in, a_b_gate, a_conv_w, a_conv_b, a_wq, a_wk, a_wv, a_gn, a_skip, a_w_out, b_w_in, b_b_gate, b_pe_k, b_pe_v, b_phik_w1, b_phik_b1, b_phik_w2, b_phiv_w1, b_phiv_b1, b_phiv_w2, b_w_out, c_w_in, c_w_out, f_w_up, f_conv_w, f_conv_b, f_w_down):
    depth = norm_mix.shape[0]
    h = x
    for i in range(depth):
        kind, j = i % 3, i // 3
        mix = None
        if kind == 0:
            h = _mlstm_layer(h, norm_mix[i], a_w_in[j], a_b_gate[j], a_conv_w[j], a_conv_b[j],
                             a_wq[j], a_wk[j], a_wv[j], a_gn[j], a_skip[j], a_w_out[j])
        elif kind == 1:
            h = _nsa_layer(h, norm_mix[i], b_w_in[j], b_b_gate[j], b_pe_k[j], b_pe_v[j],
                           b_phik_w1[j], b_phik_b1[j], b_phik_w2[j],
                           b_phiv_w1[j], b_phiv_b1[j], b_phiv_w2[j], b_w_out[j])
        else:
            mix = _dilated_layer(h, norm_mix[i], c_w_in[j], c_w_out[j])
        h = _conv_ffn(h, norm_ffn[i], f_w_up[i].astype(BF16), f_conv_w[i], f_conv_b[i],
                      f_w_down[i].astype(BF16), norm_final, tm=512,
                      final_norm=(i == depth - 1), name="conv_ffn", mix=mix)
    return h
```

```python
import functools

import jax
import jax.numpy as jnp
from jax import lax
from jax.experimental import pallas as pl
from jax.experimental.pallas import tpu as pltpu

F32 = jnp.float32
BF16 = jnp.bfloat16

NORM_EPS = 1e-6
NEG = -1e30
LOG2E = 1.4426950408889634
LANE = 128
SUBLANE = 8
VMEM_LIMIT = 56 * 1024 * 1024
NMM_COLS = 512
NMM_MAX_STRIDE = 4

A_HEADS = 4
A_CHUNK = 256
A_CONV = 4
B_HEADS = 16
B_DH = 64
B_GROUPS = 4
B_QPG = 4
B_CMP_LEN = 32
B_CMP_STRIDE = 16
B_SEL_LEN = 64
B_TOPN = 16
B_N_LOCAL = 2
B_WINDOW = 512
FORCE_SCORE = 1e9
B_TQ = 256
B_TK = 256
C_PATTERNS = ((128, 1), (512, 4), (2048, 16))
C_HEADS = 16
C_DH = 64
C_QBLK = 128
C_BLOCKS_PER_ITER = 16
F_CHUNK = 256
F_SPLIT = 2


def _cparams(sem):
    return pltpu.CompilerParams(dimension_semantics=sem, vmem_limit_bytes=VMEM_LIMIT)


def _const_spec(shape):
    nd = len(shape)
    return pl.BlockSpec(shape, lambda *_: (0,) * nd, pipeline_mode=pl.Buffered(1))


def _rmsnorm_rows(x, g):
    ms = jnp.mean(x * x, axis=-1, keepdims=True)
    return x * lax.rsqrt(ms + NORM_EPS) * g


def _sigmoid(x):
    return 1.0 / (1.0 + jnp.exp(-x))


def _nmm_body(x_ref, g_ref, w_ref, o_ref, xn_ref, *rest, dil):
    @pl.when(pl.program_id(2) == 0)
    def _():
        xn_ref[...] = _rmsnorm_rows(x_ref[0], g_ref[...]).astype(BF16)

    tm, tn = xn_ref.shape[0], w_ref.shape[1]
    step = min(tn, NMM_COLS)
    for c0 in range(0, tn, step):
        r = jnp.dot(xn_ref[...], w_ref[:, c0:c0 + step], preferred_element_type=F32)
        for cl in range(step // LANE):
            c = c0 // LANE + cl
            piece = r[:, cl * LANE:(cl + 1) * LANE]
            if dil == 1:
                o_ref[0, c, 0] = piece.astype(o_ref.dtype)
            else:
                r_ref, t_ref = rest
                r_ref[c] = piece
                if dil <= NMM_MAX_STRIDE:
                    for rc in range(dil):
                        o_ref[0, c, rc] = r_ref[c, pl.ds(rc, tm // dil, stride=dil),
                                                :].astype(o_ref.dtype)
                else:
                    d1 = NMM_MAX_STRIDE
                    d2 = dil // d1
                    run = tm // d1
                    for a in range(d1):
                        t_ref[a * run:(a + 1) * run, :] = r_ref[c, pl.ds(a, run, stride=d1), :]
                    for a in range(d1):
                        for j in range(d2):
                            o_ref[0, c, d1 * j + a] = t_ref[
                                pl.ds(a * run + j, run // d2, stride=d2), :].astype(o_ref.dtype)


def _norm_matmul(h, g, w, *, out_dtype, tm, tn, dil=1, name):
    bsz, t_len, d = h.shape
    n = w.shape[1]
    assert t_len % tm == 0 and n % tn == 0 and tn % LANE == 0 and tm % (2 * SUBLANE * dil) == 0
    scratch = [pltpu.VMEM((tm, d), BF16)]
    out_shape = jax.ShapeDtypeStruct((bsz, n // LANE, dil, t_len // dil, LANE), out_dtype)
    out_spec = pl.BlockSpec((1, tn // LANE, dil, tm // dil, LANE), lambda b, i, j: (b, j, 0, i, 0))
    if dil > 1:
        assert dil <= NMM_MAX_STRIDE or dil % NMM_MAX_STRIDE == 0
        scratch += [pltpu.VMEM((tn // LANE, tm, LANE), F32), pltpu.VMEM((tm, LANE), F32)]
    return pl.pallas_call(
        functools.partial(_nmm_body, dil=dil),
        out_shape=out_shape,
        grid=(bsz, t_len // tm, n // tn),
        in_specs=[
            pl.BlockSpec((1, tm, d), lambda b, i, j: (b, i, 0)),
            pl.BlockSpec((1, d), lambda b, i, j: (0, 0)),
            _const_spec((d, n)) if tn == n else pl.BlockSpec((d, tn), lambda b, i, j: (0, j)),
        ],
        out_specs=out_spec,
        scratch_shapes=scratch,
        compiler_params=_cparams(("parallel", "parallel", "arbitrary")),
        name=name,
    )(h, g.reshape(1, d), w)


def _ffn_body(h_ref, hp_ref, g_ref, wup_ref, cw_ref, cb_ref, wdn_ref, gf_ref, *rest,
              tm, f_hidden, final_norm, mixed):
    halo = SUBLANE
    g = g_ref[...]
    if mixed:
        ox_ref, oxp_ref, wmix_ref, o_ref, xn_ref, acc_ref, hres_ref = rest
        slabs = range(ox_ref.shape[1])
        o_cur = jnp.concatenate([ox_ref[0, c] for c in slabs], axis=-1)
        o_prev = jnp.concatenate([oxp_ref[0, c] for c in slabs], axis=-1)
        hres_ref[...] = h_ref[0] + jnp.dot(o_cur, wmix_ref[...], preferred_element_type=F32)
        h_cur = hres_ref[...]
        h_prev = hp_ref[0] + jnp.dot(o_prev, wmix_ref[...],
                                     preferred_element_type=F32)[halo:2 * halo]
    else:
        o_ref, xn_ref, acc_ref = rest
        h_cur, h_prev = h_ref[0], hp_ref[0]
    keep = (pl.program_id(1) > 0).astype(F32)
    xn_ref[0:halo, :] = (_rmsnorm_rows(h_prev, g) * keep).astype(BF16)
    xn_ref[halo:, :] = _rmsnorm_rows(h_cur, g).astype(BF16)
    rows = tm // F_SPLIT
    xns = [xn_ref[s * rows:s * rows + rows + halo, :] for s in range(F_SPLIT)]

    def conv(u, col0):
        w = cw_ref[:, col0:col0 + F_CHUNK]
        b = cb_ref[:, col0:col0 + F_CHUNK]
        return (w[2:3] * u[halo:halo + rows] + w[1:2] * u[halo - 1:halo - 1 + rows]
                + w[0:1] * u[halo - 2:halo - 2 + rows] + b)

    def up(c):
        c0 = c * F_CHUNK
        return [(jnp.dot(xn, wup_ref[:, c0:c0 + F_CHUNK], preferred_element_type=F32),
                 jnp.dot(xn, wup_ref[:, f_hidden + c0:f_hidden + c0 + F_CHUNK],
                         preferred_element_type=F32)) for xn in xns]

    n_chunks = f_hidden // F_CHUNK
    nxt = up(0)
    accs = [None] * F_SPLIT
    for c in range(n_chunks):
        c0 = c * F_CHUNK
        cur = nxt
        if c + 1 < n_chunks:
            nxt = up(c + 1)
        for s, (ug, uv) in enumerate(cur):
            gate = conv(ug, c0)
            val = conv(uv, f_hidden + c0)
            act = (gate * _sigmoid(gate) * val).astype(BF16)
            part = jnp.dot(act, wdn_ref[c0:c0 + F_CHUNK, :], preferred_element_type=F32)
            accs[s] = part if c == 0 else accs[s] + part
    for s in range(F_SPLIT):
        acc_ref[s * rows:(s + 1) * rows, :] = accs[s]

    out = (hres_ref[...] if mixed else h_ref[0]) + acc_ref[...]
    if final_norm:
        out = _rmsnorm_rows(out, gf_ref[...])
    o_ref[0] = out


def _conv_ffn(h, g, w_up, conv_w, conv_b, w_down, g_final, *, tm, final_norm, name, mix=None):
    bsz, t_len, d = h.shape
    f2 = w_up.shape[1]
    f_hidden = f2 // 2
    assert f_hidden % F_CHUNK == 0 and t_len % tm == 0 and tm % (2 * SUBLANE) == 0
    rows_per_tile = tm // SUBLANE
    in_specs = [
        pl.BlockSpec((1, tm, d), lambda b, i: (b, i, 0)),
        pl.BlockSpec((1, SUBLANE, d), lambda b, i: (b, jnp.maximum(i * rows_per_tile - 1, 0), 0)),
        _const_spec((1, d)),
        _const_spec((d, f2)),
        _const_spec((3, f2)),
        _const_spec((1, f2)),
        _const_spec((f_hidden, d)),
        _const_spec((1, d)),
    ]
    args = [h, h, g.reshape(1, d), w_up, conv_w, conv_b.reshape(1, f2), w_down,
            g_final.reshape(1, d)]
    scratch = [pltpu.VMEM((tm + SUBLANE, d), BF16), pltpu.VMEM((tm, d), F32)]
    if mix is not None:
        o, w_mix = mix
        n_slab = o.shape[1]
        in_specs += [
            pl.BlockSpec((1, n_slab, tm, LANE), lambda b, i: (b, 0, i, 0)),
            pl.BlockSpec((1, n_slab, 2 * SUBLANE, LANE),
                         lambda b, i: (b, 0, jnp.maximum(i * (rows_per_tile // 2) - 1, 0), 0)),
            _const_spec(w_mix.shape),
        ]
        args += [o, o, w_mix]
        scratch.append(pltpu.VMEM((tm, d), F32))
    return pl.pallas_call(
        functools.partial(_ffn_body, tm=tm, f_hidden=f_hidden, final_norm=final_norm,
                          mixed=mix is not None),
        out_shape=jax.ShapeDtypeStruct((bsz, t_len, d), F32),
        grid=(bsz, t_len // tm),
        in_specs=in_specs,
        out_specs=pl.BlockSpec((1, tm, d), lambda b, i: (b, i, 0)),
        scratch_shapes=scratch,
        compiler_params=_cparams(("parallel", "parallel")),
        name=name,
    )(*args)


def _mlstm_body(h_ref, g_ref, win_ref, bg_ref, cw_ref, cb_ref, wq_ref, wk_ref, wv_ref,
                gn_ref, sk_ref, wout_ref, o_ref, c_ref, n_ref, m_ref, xp_ref, y_ref,
                *, chunk, dh):
    L = chunk
    halo = SUBLANE
    inner = A_HEADS * dh

    @pl.when(pl.program_id(1) == 0)
    def _():
        c_ref[...] = jnp.zeros_like(c_ref)
        n_ref[...] = jnp.zeros_like(n_ref)
        m_ref[...] = jnp.zeros_like(m_ref)
        xp_ref[...] = jnp.zeros_like(xp_ref)

    xn = _rmsnorm_rows(h_ref[0], g_ref[...]).astype(BF16)
    xm = jnp.dot(xn, win_ref[:, 0:inner], preferred_element_type=F32)
    gates = jnp.dot(xn, win_ref[:, 2 * inner:], preferred_element_type=F32) + bg_ref[...]
    heads = [dict(lo=hd * dh, hi=(hd + 1) * dh) for hd in range(A_HEADS)]
    for hd, st in enumerate(heads):
        st["v"] = jnp.dot(xm[:, st["lo"]:st["hi"]].astype(BF16), wv_ref[hd],
                          preferred_element_type=F32)
    o_pre = jnp.dot(xn, win_ref[:, inner:2 * inner], preferred_element_type=F32)

    xe = jnp.concatenate([xp_ref[...], xm], axis=0)
    cw = cw_ref[...]
    xc = (cw[3:4] * xe[halo:halo + L] + cw[2:3] * xe[halo - 1:halo - 1 + L]
          + cw[1:2] * xe[halo - 2:halo - 2 + L] + cw[0:1] * xe[halo - 3:halo - 3 + L]
          + cb_ref[...])
    xc = xc * _sigmoid(xc)
    xp_ref[...] = xm[L - halo:L]

    ii = lax.broadcasted_iota(jnp.int32, (L, L), 0)
    jj = lax.broadcasted_iota(jnp.int32, (L, L), 1)
    eye = ii == jj
    tril = jj <= ii

    def to_row(col):
        return jnp.sum(jnp.where(eye, col, 0.0), axis=0, keepdims=True)

    for hd, st in enumerate(heads):
        st["xc"] = xc[:, st["lo"]:st["hi"]]
        xcb = st["xc"].astype(BF16)
        st["q"] = jnp.dot(xcb, wq_ref[hd], preferred_element_type=F32)
        st["k"] = jnp.dot(xcb, wk_ref[hd], preferred_element_type=F32)

    for hd, st in enumerate(heads):
        li = gates[:, hd:hd + 1]
        fpre = gates[:, A_HEADS + hd:A_HEADS + hd + 1]
        lf = jnp.minimum(fpre, 0.0) - jnp.log1p(jnp.exp(-jnp.abs(fpre)))
        li_row = to_row(li)
        lf_row = to_row(lf)
        b_col = jnp.sum(jnp.where(tril, lf_row, 0.0), axis=1, keepdims=True)
        b_row = to_row(b_col)
        m_st = m_ref[hd][:, 0:1]
        log_d = jnp.where(tril, b_col - b_row + li_row, NEG)
        log_inter = b_col + m_st
        m = jnp.maximum(log_inter, jnp.max(log_d, axis=1, keepdims=True))
        st["m"] = m
        st["w_intra"] = jnp.exp(log_d - m)
        st["w_inter"] = jnp.exp(log_inter - m)
        b_last = b_col[L - 1:L, :]
        log_w = b_last - b_col + li
        m_new = jnp.maximum(b_last + m_st, jnp.max(log_w, axis=0, keepdims=True))
        st["decay"] = jnp.exp(b_last + m_st - m_new)
        st["w_k"] = jnp.exp(log_w - m_new)
        m_ref[hd] = jnp.broadcast_to(m_new, m_ref.shape[1:])

    for hd, st in enumerate(heads):
        st["k"] = st["k"] * (dh ** -0.5)
        st["qb"] = st["q"].astype(BF16)
        st["vb"] = st["v"].astype(BF16)
        st["s"] = lax.dot_general(st["qb"], st["k"].astype(BF16), (((1,), (1,)), ((), ())),
                                  preferred_element_type=F32)
        st["qc"] = jnp.dot(st["qb"], c_ref[hd].astype(BF16), preferred_element_type=F32)

    for hd, st in enumerate(heads):
        s = st["s"] * st["w_intra"]
        st["den"] = (st["w_inter"] * jnp.sum(st["q"] * n_ref[hd], axis=1, keepdims=True)
                     + jnp.sum(s, axis=1, keepdims=True))
        wk = st["w_k"] * st["k"]
        n_ref[hd] = st["decay"] * n_ref[hd] + jnp.sum(wk, axis=0, keepdims=True)
        st["sv"] = jnp.dot(s.astype(BF16), st["vb"], preferred_element_type=F32)
        st["kv"] = lax.dot_general(wk.astype(BF16), st["vb"], (((0,), (0,)), ((), ())),
                                   preferred_element_type=F32)

    for hd, st in enumerate(heads):
        lo, hi = st["lo"], st["hi"]
        c_ref[hd] = st["decay"] * c_ref[hd] + st["kv"]
        num = st["w_inter"] * st["qc"] + st["sv"]
        hh = num / jnp.maximum(jnp.abs(st["den"]), jnp.exp(-st["m"]))
        mu = jnp.mean(hh, axis=-1, keepdims=True)
        ctr = hh - mu
        var = jnp.mean(ctr * ctr, axis=-1, keepdims=True)
        hn = ctr * lax.rsqrt(var + NORM_EPS) * gn_ref[:, lo:hi]
        y = _sigmoid(o_pre[:, lo:hi]) * hn + sk_ref[:, lo:hi] * st["xc"]
        y_ref[:, lo:hi] = y.astype(y_ref.dtype)

    o_ref[0] = h_ref[0] + jnp.dot(y_ref[...], wout_ref[...], preferred_element_type=F32)


def _mlstm_layer(h, g, w_in, b_gate, conv_w, conv_b, wq, wk, wv, gn, skip, w_out):
    bsz, t_len, d = h.shape
    inner = w_out.shape[0]
    dh = inner // A_HEADS
    L = A_CHUNK
    w_a = w_in.astype(BF16)
    bg = b_gate.reshape(1, 2 * A_HEADS)
    return pl.pallas_call(
        functools.partial(_mlstm_body, chunk=L, dh=dh),
        out_shape=jax.ShapeDtypeStruct((bsz, t_len, d), F32),
        grid=(bsz, t_len // L),
        in_specs=[
            pl.BlockSpec((1, L, d), lambda b, c: (b, c, 0)),
            _const_spec((1, d)),
            _const_spec((d, 2 * inner + 2 * A_HEADS)),
            _const_spec((1, 2 * A_HEADS)),
            _const_spec((A_CONV, inner)),
            _const_spec((1, inner)),
            _const_spec((A_HEADS, dh, dh)),
            _const_spec((A_HEADS, dh, dh)),
            _const_spec((A_HEADS, dh, dh)),
            _const_spec((1, inner)),
            _const_spec((1, inner)),
            _const_spec((inner, d)),
        ],
        out_specs=pl.BlockSpec((1, L, d), lambda b, c: (b, c, 0)),
        scratch_shapes=[
            pltpu.VMEM((A_HEADS, dh, dh), F32),
            pltpu.VMEM((A_HEADS, 1, dh), F32),
            pltpu.VMEM((A_HEADS, 1, LANE), F32),
            pltpu.VMEM((SUBLANE, inner), F32),
            pltpu.VMEM((L, inner), BF16),
        ],
        compiler_params=_cparams(("parallel", "arbitrary")),
        name="mlstm_layer",
    )(h, g.reshape(1, d), w_a, bg, conv_w, conv_b.reshape(1, inner), wq.astype(BF16),
      wk.astype(BF16), wv.astype(BF16), gn.reshape(1, inner), skip.reshape(1, inner),
      w_out.astype(BF16))


def _cmp_body(x_ref, pe_ref, w1_ref, b1_ref, w2_ref, o_ref):
    half = x_ref.shape[-1]
    x = x_ref[0, 0, 0]
    pe = pe_ref[0]
    a = (x + pe[:, :half]).astype(BF16)
    b = (x + pe[:, half:]).astype(BF16)
    w1 = w1_ref[0]
    pa = jnp.dot(a, w1[:half], preferred_element_type=F32)
    pb = jnp.dot(b, w1[half:], preferred_element_type=F32)
    pb_next = jnp.concatenate([pb[1:], jnp.zeros((1, pb.shape[1]), F32)], axis=0)
    pre = pa + pb_next + b1_ref[0]
    hid = (pre * _sigmoid(pre)).astype(BF16)
    out = jnp.dot(hid, w2_ref[0], preferred_element_type=F32)
    row = lax.broadcasted_iota(jnp.int32, out.shape, 0)
    o_ref[0, 0, 0] = jnp.where(row < out.shape[0] - 1, out, 0.0).astype(o_ref.dtype)


def _nsa_compress(kv_cmp, pe, w1, b1, w2):
    two, bsz, g, t_len, dh = kv_cmp.shape
    n_half = t_len // B_CMP_STRIDE
    half = B_CMP_STRIDE * dh
    x = kv_cmp.reshape(two, bsz, g, n_half, half)
    hidden = w1.shape[-1]
    return pl.pallas_call(
        _cmp_body,
        out_shape=jax.ShapeDtypeStruct((two, bsz, g, n_half, dh), BF16),
        grid=(two, bsz, g),
        in_specs=[
            pl.BlockSpec((1, 1, 1, n_half, half), lambda s, b, gi: (s, b, gi, 0, 0)),
            pl.BlockSpec((1, 1, 2 * half), lambda s, b, gi: (s, 0, 0)),
            pl.BlockSpec((1, 2 * half, hidden), lambda s, b, gi: (s, 0, 0)),
            pl.BlockSpec((1, 1, hidden), lambda s, b, gi: (s, 0, 0)),
            pl.BlockSpec((1, hidden, dh), lambda s, b, gi: (s, 0, 0)),
        ],
        out_specs=pl.BlockSpec((1, 1, 1, n_half, dh), lambda s, b, gi: (s, b, gi, 0, 0)),
        compiler_params=_cparams(("parallel", "parallel", "parallel")),
        name="nsa_compress",
    )(x, pe.reshape(two, 1, 2 * half), w1, b1.reshape(two, 1, hidden), w2)


def _flash_t(q_ts, k_ref, vt_ref, acc_ref, m_ref, kt_lo, kt_hi, bias_fn, last_bias_fn, tk):
    acc_ref[...] = jnp.zeros_like(acc_ref)
    m_ref[...] = jnp.full_like(m_ref, NEG)
    groups = range(len(q_ts))

    def tile(kt, bias_of):
        ks = pl.multiple_of(kt * tk, tk)
        ss = [jnp.dot(k_ref[0, g, pl.ds(ks, tk), :], q_ts[g], preferred_element_type=F32)
              for g in groups]
        bias = None if bias_of is None else bias_of(kt, ks)
        ps, alphas = [], []
        for g in groups:
            s = ss[g] if bias is None else ss[g] + bias
            m = m_ref[g]
            m_new = jnp.maximum(m, jnp.max(s, axis=0, keepdims=True))
            alphas.append(jnp.exp2(m - m_new))
            ps.append(jnp.exp2(s - m_new).astype(BF16))
            m_ref[g] = m_new
        for g in groups:
            v_t = vt_ref[0, g, :, pl.ds(ks, tk)]
            acc_ref[g] = alphas[g] * acc_ref[g] + jnp.dot(v_t, ps[g],
                                                          preferred_element_type=F32)

    def body(kt, carry):
        tile(kt, bias_fn)
        return carry

    lax.fori_loop(kt_lo, kt_hi - 1, body, 0)
    tile(kt_hi - 1, last_bias_fn)
    dh = acc_ref.shape[1] - 16
    return [acc_ref[g, 0:dh, :] / acc_ref[g, dh:dh + 1, :] for g in groups]


def _flash_pair(q_sels, q_wins, ks_ref, vs_ref, kw_ref, vw_ref, acc_ref, m_ref, acc2_ref, m2_ref,
                w_lo, kt_hi, causal_bias, win_bias, tk):
    for ref in (acc_ref, acc2_ref):
        ref[...] = jnp.zeros_like(ref)
    for ref in (m_ref, m2_ref):
        ref[...] = jnp.full_like(ref, NEG)
    groups = range(len(q_sels))

    def tile(kt, with_win, sel_bias_of):
        ks = pl.multiple_of(kt * tk, tk)
        chains = [(ks_ref, vs_ref, q_sels, acc_ref, m_ref,
                   None if sel_bias_of is None else sel_bias_of(kt, ks))]
        if with_win:
            chains.append((kw_ref, vw_ref, q_wins, acc2_ref, m2_ref, win_bias(kt, ks)))
        scores = [[jnp.dot(k_ref[0, g, pl.ds(ks, tk), :], qs[g], preferred_element_type=F32)
                   for g in groups] for (k_ref, _, qs, _, _, _) in chains]
        probs = []
        for ci, (_, _, _, _, mr, bias) in enumerate(chains):
            row = []
            for g in groups:
                s = scores[ci][g] if bias is None else scores[ci][g] + bias
                m = mr[g]
                m_new = jnp.maximum(m, jnp.max(s, axis=0, keepdims=True))
                row.append((jnp.exp2(m - m_new), jnp.exp2(s - m_new).astype(BF16)))
                mr[g] = m_new
            probs.append(row)
        for ci, (_, vt_ref, _, ar, _, _) in enumerate(chains):
            for g in groups:
                alpha, p = probs[ci][g]
                ar[g] = alpha * ar[g] + jnp.dot(vt_ref[0, g, :, pl.ds(ks, tk)], p,
                                                preferred_element_type=F32)

    def sel_only(kt, carry):
        tile(kt, False, None)
        return carry

    def both(kt, carry):
        tile(kt, True, None)
        return carry

    lax.fori_loop(0, w_lo, sel_only, 0)
    lax.fori_loop(w_lo, kt_hi - 1, both, 0)
    tile(kt_hi - 1, True, causal_bias)
    dh = acc_ref.shape[1] - 16
    return ([acc_ref[g, 0:dh, :] / acc_ref[g, dh:dh + 1, :] for g in groups],
            [acc2_ref[g, 0:dh, :] / acc2_ref[g, dh:dh + 1, :] for g in groups])


def _nsa_body(q_ref, kc_ref, vc_ref, ks_ref, vs_ref, kw_ref, vw_ref, gp_ref, bg_ref,
              ovl_ref, wout_ref, h_ref, o_ref, acc_ref, m_ref, acc2_ref, m2_ref, og_ref,
              *, tq, tk, t_len):
    qi = pl.program_id(1)
    t0 = qi * tq
    gw = B_QPG * B_DH
    ncp = kc_ref.shape[2]
    nsel = t_len // B_SEL_LEN
    n_top = min(B_TOPN, nsel)

    def lanes4(x):
        return jnp.concatenate([x] * B_QPG, axis=1)

    n_idx = lax.broadcasted_iota(jnp.int32, (ncp, tq), 0)
    t_col = t0 + lax.broadcasted_iota(jnp.int32, (ncp, tq), 1)
    cmask = (n_idx * B_CMP_STRIDE + (B_CMP_LEN - 1) <= t_col) & (n_idx < ncp - 1)
    cbias = lanes4(jnp.where(cmask, 0.0, NEG))
    ckeep = lanes4(cmask.astype(F32))
    blk = lax.broadcasted_iota(jnp.int32, (nsel, tq), 0)
    cur = (t0 + lax.broadcasted_iota(jnp.int32, (nsel, tq), 1)) // B_SEL_LEN
    valid = blk <= cur
    forced = ((blk == 0) | (cur - blk < B_N_LOCAL)) & valid

    q_ts, q_sels, o_cmps = [], [], []
    for g in range(B_GROUPS):
        q_full = q_ref[0, :, g * gw:(g + 1) * gw].astype(F32).T.astype(BF16)
        q_t = jnp.concatenate([q_full[r * B_DH:(r + 1) * B_DH, :] for r in range(B_QPG)], axis=1)
        q_ts.append(q_t)

        s = jnp.dot(kc_ref[0, g], q_t, preferred_element_type=F32) + cbias
        m = jnp.max(s, axis=0, keepdims=True)
        e = jnp.exp2(s - m) * ckeep
        den = jnp.sum(e, axis=0, keepdims=True)
        p = e / jnp.where(den > 0, den, 1.0)
        o_cmps.append(lax.dot_general(vc_ref[0, g], p.astype(BF16), (((0,), (0,)), ((), ())),
                                      preferred_element_type=F32))

        psum = p[:, 0:tq]
        for r in range(1, B_QPG):
            psum = psum + p[:, r * tq:(r + 1) * tq]
        p_hi = psum.astype(BF16)
        r1 = psum - p_hi.astype(F32)
        p_mid = r1.astype(BF16)
        p_lo = (r1 - p_mid.astype(F32)).astype(BF16)
        ovl = ovl_ref[...]
        imp = (jnp.dot(ovl, p_hi, preferred_element_type=F32)
               + jnp.dot(ovl, p_mid, preferred_element_type=F32)
               + jnp.dot(ovl, p_lo, preferred_element_type=F32))
        imp = imp[0:nsel]
        score = jnp.where(forced, FORCE_SCORE, jnp.where(valid, imp, -FORCE_SCORE))
        slabs = [score[r0:r0 + SUBLANE] for r0 in range(0, nsel, SUBLANE)]
        ranks = [jnp.zeros((SUBLANE, tq), F32) for _ in slabs]
        row = lax.broadcasted_iota(jnp.int32, (SUBLANE, tq), 0)
        for j in range(nsel):
            sj = score[j:j + 1, :]
            for si, sc in enumerate(slabs):
                r0 = si * SUBLANE
                ge = jnp.where(sj >= sc, 1.0, 0.0)
                gt = jnp.where(sj > sc, 1.0, 0.0)
                if r0 > j:
                    ahead = ge
                elif r0 + SUBLANE - 1 < j:
                    ahead = gt
                else:
                    ahead = jnp.where(row + r0 > j, ge, gt)
                ranks[si] = ranks[si] + ahead
        rank = jnp.concatenate(ranks, axis=0)
        sel_bias = jnp.where((rank < n_top) & valid, 0.0, NEG)
        q_sels.append(jnp.concatenate(
            [q_t, lanes4(sel_bias).astype(BF16),
             jnp.zeros((LANE - B_DH - nsel, B_QPG * tq), BF16)], axis=0))

    k_off = lax.broadcasted_iota(jnp.int32, (tk, tq), 0)
    t_pos = t0 + lax.broadcasted_iota(jnp.int32, (tk, tq), 1)

    def causal_bias(kt, ks):
        return lanes4(jnp.where(ks + k_off <= t_pos, 0.0, NEG))

    kt_hi = (t0 + tq + tk - 1) // tk

    def win_bias(kt, ks):
        diff = t_pos - (ks + k_off)
        return lanes4(jnp.where((diff >= 0) & (diff <= B_WINDOW - 1), 0.0, NEG))

    w_lo = jnp.maximum(t0 - (B_WINDOW - 1), 0) // tk
    o_sels, o_wins = _flash_pair(q_sels, q_ts, ks_ref, vs_ref, kw_ref, vw_ref, acc_ref, m_ref,
                                 acc2_ref, m2_ref, w_lo, kt_hi, causal_bias, win_bias, tk)

    for g in range(B_GROUPS):
        gate = _sigmoid(gp_ref[0, g] + bg_ref[g])
        outs = []
        for r in range(B_QPG):
            sl = slice(r * tq, (r + 1) * tq)
            outs.append(gate[r:r + 1, :] * o_cmps[g][:, sl]
                        + gate[B_QPG + r:B_QPG + r + 1, :] * o_sels[g][:, sl]
                        + gate[2 * B_QPG + r:2 * B_QPG + r + 1, :] * o_wins[g][:, sl])
        og_ref[:, g * gw:(g + 1) * gw] = jnp.concatenate(outs, axis=0).T.astype(og_ref.dtype)

    o_ref[0] = h_ref[0] + jnp.dot(og_ref[...], wout_ref[...], preferred_element_type=F32)


def _nsa_attention(q, kc, vc, k_sel, vt_sel, k_win, vt_win, gpre_t, bg, w_out, h):
    bsz, t_len, _ = q.shape
    d = h.shape[-1]
    tq, tk = B_TQ, B_TK
    ncp = kc.shape[2]
    nsel = t_len // B_SEL_LEN
    gw = B_QPG * B_DH
    c_start = jnp.arange(ncp)[None, :] * B_CMP_STRIDE
    s_start = jnp.arange(LANE)[:, None] * B_SEL_LEN
    ovl = ((c_start < s_start + B_SEL_LEN) & (c_start + B_CMP_LEN > s_start)
           & (jnp.arange(LANE)[:, None] < nsel) & (jnp.arange(ncp)[None, :] < ncp - 1)).astype(BF16)

    def k_spec(n_rows, width=B_DH):
        return pl.BlockSpec((1, B_GROUPS, n_rows, width), lambda b, i: (b, 0, 0, 0))

    vt_spec = pl.BlockSpec((1, B_GROUPS, B_DH + 16, t_len), lambda b, i: (b, 0, 0, 0))
    n_q = B_QPG * tq

    return pl.pallas_call(
        functools.partial(_nsa_body, tq=tq, tk=tk, t_len=t_len),
        out_shape=jax.ShapeDtypeStruct((bsz, t_len, d), F32),
        grid=(bsz, t_len // tq),
        in_specs=[
            pl.BlockSpec((1, tq, B_GROUPS * gw), lambda b, i: (b, i, 0)),
            k_spec(ncp), k_spec(ncp), k_spec(t_len, LANE), vt_spec, k_spec(t_len), vt_spec,
            pl.BlockSpec((1, B_GROUPS, 16, tq), lambda b, i: (b, 0, 0, i)),
            _const_spec((B_GROUPS, 16, tq)),
            _const_spec((LANE, ncp)),
            _const_spec((B_GROUPS * gw, d)),
            pl.BlockSpec((1, tq, d), lambda b, i: (b, i, 0)),
        ],
        out_specs=pl.BlockSpec((1, tq, d), lambda b, i: (b, i, 0)),
        scratch_shapes=[pltpu.VMEM((B_GROUPS, B_DH + 16, n_q), F32),
                        pltpu.VMEM((B_GROUPS, 1, n_q), F32),
                        pltpu.VMEM((B_GROUPS, B_DH + 16, n_q), F32),
                        pltpu.VMEM((B_GROUPS, 1, n_q), F32),
                        pltpu.VMEM((tq, B_GROUPS * gw), BF16)],
        compiler_params=_cparams(("parallel", "parallel")),
        name="nsa_attention",
    )(q, kc, vc, k_sel, vt_sel, k_win, vt_win, gpre_t, bg, ovl, w_out, h)


def _nsa_in_body(x_ref, g_ref, w_ref, q_ref, kvc_ref, ks_ref, vs_ref, kw_ref, vw_ref, gp_ref):
    nq = B_HEADS * B_DH
    gw = B_GROUPS * B_DH
    xn = _rmsnorm_rows(x_ref[0], g_ref[...]).astype(BF16)
    q_ref[0] = jnp.dot(xn, w_ref[:, 0:nq], preferred_element_type=F32).astype(q_ref.dtype)

    def seg(i):
        return jnp.dot(xn, w_ref[:, nq + i * gw:nq + (i + 1) * gw], preferred_element_type=F32)

    def put_rows(ref, lead, r):
        for gi in range(B_GROUPS):
            ref[lead + (gi,)] = r[:, gi * B_DH:(gi + 1) * B_DH].astype(ref.dtype)

    def put_cols(ref, r, width, ones_rows=0):
        r_t = r.T
        tm = r.shape[0]
        for gi in range(B_GROUPS):
            ref[0, gi, 0:width, :] = r_t[gi * width:(gi + 1) * width, :].astype(ref.dtype)
            if ones_rows:
                ref[0, gi, width:width + ones_rows, :] = jnp.ones((ones_rows, tm), ref.dtype)

    put_rows(kvc_ref, (0, 0), seg(0))
    put_rows(kvc_ref, (1, 0), seg(1))
    r = seg(2)
    tm = r.shape[0]
    blk = (pl.program_id(1) * tm + lax.broadcasted_iota(jnp.int32, (tm, LANE - B_DH), 0)) // B_SEL_LEN
    onehot = (blk == lax.broadcasted_iota(jnp.int32, (tm, LANE - B_DH), 1)).astype(F32)
    for gi in range(B_GROUPS):
        ks_ref[0, gi] = jnp.concatenate([r[:, gi * B_DH:(gi + 1) * B_DH], onehot],
                                        axis=1).astype(ks_ref.dtype)
    put_cols(vs_ref, seg(3), B_DH, ones_rows=16)
    put_rows(kw_ref, (0,), seg(4))
    put_cols(vw_ref, seg(5), B_DH, ones_rows=16)
    gates = jnp.dot(xn, w_ref[:, nq + 6 * gw:], preferred_element_type=F32)
    put_cols(gp_ref, gates, 16)


def _nsa_in(h, g, w):
    bsz, t_len, d = h.shape
    tm = 512
    nq = B_HEADS * B_DH
    G = B_GROUPS
    sds = jax.ShapeDtypeStruct
    k_shape = sds((bsz, G, t_len, B_DH), BF16)
    vt_shape = sds((bsz, G, B_DH + 16, t_len), BF16)
    k_spec = pl.BlockSpec((1, G, tm, B_DH), lambda b, i: (b, 0, i, 0))
    vt_spec = pl.BlockSpec((1, G, B_DH + 16, tm), lambda b, i: (b, 0, 0, i))
    return pl.pallas_call(
        _nsa_in_body,
        out_shape=(sds((bsz, t_len, nq), BF16), sds((2, bsz, G, t_len, B_DH), F32),
                   sds((bsz, G, t_len, LANE), BF16), vt_shape, k_shape, vt_shape,
                   sds((bsz, G, 16, t_len), F32)),
        grid=(bsz, t_len // tm),
        in_specs=[pl.BlockSpec((1, tm, d), lambda b, i: (b, i, 0)),
                  _const_spec((1, d)), _const_spec(w.shape)],
        out_specs=(pl.BlockSpec((1, tm, nq), lambda b, i: (b, i, 0)),
                   pl.BlockSpec((2, 1, G, tm, B_DH), lambda b, i: (0, b, 0, i, 0)),
                   pl.BlockSpec((1, G, tm, LANE), lambda b, i: (b, 0, i, 0)), vt_spec,
                   k_spec, vt_spec,
                   pl.BlockSpec((1, G, 16, tm), lambda b, i: (b, 0, 0, i))),
        compiler_params=_cparams(("parallel", "parallel")),
        name="nsa_in",
    )(h, g.reshape(1, d), w)


def _nsa_layer(h, g, w_in, b_gate, pe_k, pe_v, phik_w1, phik_b1, phik_w2,
               phiv_w1, phiv_b1, phiv_w2, w_out):
    bsz, t_len, d = h.shape
    nq = B_HEADS * B_DH
    nkv = 6 * B_GROUPS * B_DH
    wg = w_in[:, nq + nkv:].reshape(d, B_GROUPS, B_QPG, 3).transpose(0, 1, 3, 2)
    wg = jnp.pad(wg.reshape(d, B_GROUPS, 3 * B_QPG), ((0, 0), (0, 0), (0, 16 - 3 * B_QPG)))
    wg = jnp.pad(wg.reshape(d, B_GROUPS * 16), ((0, 0), (0, LANE - B_GROUPS * 16)))
    bg = b_gate.reshape(B_GROUPS, B_QPG, 3).transpose(0, 2, 1).reshape(B_GROUPS, 3 * B_QPG)
    bg = jnp.pad(bg, ((0, 0), (0, 16 - 3 * B_QPG)))
    bg = jnp.broadcast_to(bg[:, :, None], (B_GROUPS, 16, B_TQ))
    w_b = jnp.concatenate([w_in[:, :nq] * (B_DH ** -0.5 * LOG2E), w_in[:, nq:nq + nkv], wg],
                          axis=1).astype(BF16)
    q, kv_cmp, k_sel, vt_sel, k_win, vt_win, gpre_t = _nsa_in(h, g, w_b)

    cmp = _nsa_compress(
        kv_cmp, jnp.stack([pe_k, pe_v]), jnp.stack([phik_w1, phiv_w1]).astype(BF16),
        jnp.stack([phik_b1, phiv_b1]), jnp.stack([phik_w2, phiv_w2]).astype(BF16))
    return _nsa_attention(q, cmp[0], cmp[1], k_sel, vt_sel, k_win, vt_win, gpre_t, bg,
                          w_out.astype(BF16), h)


def _dil_body(*refs, t_len):
    qkv = refs[:9]
    o_ref = refs[9]
    og_ref, lg_ref = refs[10], refs[11]
    qb = C_QBLK
    assert qb == LANE and 2 * C_DH == LANE
    top = lax.broadcasted_iota(jnp.int32, (LANE, qb), 0) < C_DH

    n_blocks = t_len // qb
    per_iter = C_BLOCKS_PER_ITER
    assert n_blocks % per_iter == 0

    def body(it, carry):
        blocks = []
        for gi, (window, dil) in enumerate(C_PATTERNS):
            sub = t_len // dil
            nb = sub // qb
            nk = qb if nb == 1 else 2 * qb
            for j in range(per_iter):
                i = it * per_iter + j
                rc = i // nb
                bq = i % nb
                bk = jnp.maximum(bq - 1, 0)
                q0 = pl.multiple_of(rc * sub + bq * qb, qb)
                k0 = pl.multiple_of(rc * sub + bk * qb, qb)
                if dil == 1:
                    rows = pl.ds(q0, qb)
                else:
                    rows = pl.ds(rc + bq * (qb * dil), qb, stride=dil)
                blocks.append(dict(gi=gi, nk=nk, k0=k0, rows=rows, shift=(bq - bk) * qb,
                                   max_dist=window // dil,
                                   q2=qkv[3 * gi][0, 0, pl.ds(q0, qb), :]))
        for b in blocks:
            q_t = b["q2"].astype(F32).T
            b["q_bd"] = jnp.concatenate([jnp.where(top, q_t, 0.0), jnp.where(top, 0.0, q_t)],
                                        axis=1).astype(BF16)
        for b in blocks:
            k2 = qkv[3 * b["gi"] + 1][0, 0, pl.ds(b["k0"], b["nk"]), :]
            b["s"] = jnp.dot(k2, b["q_bd"], preferred_element_type=F32)
        for b in blocks:
            nk = b["nk"]
            diff = (b["shift"] + lax.broadcasted_iota(jnp.int32, (nk, qb), 1)
                    - lax.broadcasted_iota(jnp.int32, (nk, qb), 0))
            bias = jnp.where((diff >= 0) & (diff <= b["max_dist"]), 0.0, NEG)
            s = b["s"] + jnp.concatenate([bias, bias], axis=1)
            m = jnp.max(s, axis=0, keepdims=True)
            e = jnp.exp2(s - m)
            den = jnp.sum(e, axis=0, keepdims=True)
            b["e"] = e.astype(BF16)
            b["inv"] = 1.0 / den
            b["lse"] = m + jnp.log2(den)
        for b in blocks:
            v2 = qkv[3 * b["gi"] + 2][0, 0, pl.ds(b["k0"], b["nk"]), :]
            b["pv"] = lax.dot_general(v2, b["e"], (((0,), (0,)), ((), ())),
                                      preferred_element_type=F32)
        for b in blocks:
            pv, inv, lse = b["pv"], b["inv"], b["lse"]
            o_t = jnp.concatenate([pv[0:C_DH, 0:qb] * inv[:, 0:qb],
                                   pv[C_DH:, qb:] * inv[:, qb:]], axis=0)
            l_t = jnp.concatenate([jnp.broadcast_to(lse[:, 0:qb], (C_DH, qb)),
                                   jnp.broadcast_to(lse[:, qb:], (C_DH, qb))], axis=0)
            og_ref[b["gi"], b["rows"], :] = o_t.T
            lg_ref[b["gi"], b["rows"], :] = l_t.T
        return carry

    lax.fori_loop(0, n_blocks // per_iter, body, 0)

    lse = lg_ref[...]
    mx = jnp.max(lse, axis=0, keepdims=True)
    w = jnp.exp2(lse - mx)
    alpha = w / jnp.sum(w, axis=0, keepdims=True)
    o_ref[0, 0] = jnp.sum(alpha * og_ref[...], axis=0).astype(o_ref.dtype)


def _dilated_attention(projs):
    bsz, _, t_len, _ = projs[0].shape
    npairs = C_HEADS * C_DH // LANE
    ng = len(C_PATTERNS)

    def spec(col):
        return pl.BlockSpec((1, 1, t_len, LANE), lambda b, hp, col=col: (b, col * npairs + hp, 0, 0))

    return pl.pallas_call(
        functools.partial(_dil_body, t_len=t_len),
        out_shape=jax.ShapeDtypeStruct((bsz, npairs, t_len, LANE), BF16),
        grid=(bsz, npairs),
        in_specs=[spec(c) for _ in range(ng) for c in range(3)],
        out_specs=pl.BlockSpec((1, 1, t_len, LANE), lambda b, hp: (b, hp, 0, 0)),
        scratch_shapes=[pltpu.VMEM((ng, t_len, LANE), F32), pltpu.VMEM((ng, t_len, LANE), F32)],
        compiler_params=_cparams(("parallel", "parallel")),
        name="dilated_attention",
    )(*[projs[gi] for gi in range(ng) for _ in range(3)])


def _dilated_layer(h, g, w_in, w_out):
    bsz, t_len, d = h.shape
    gw = C_HEADS * C_DH
    scale = jnp.ones((3, 1), F32).at[0, :].set(C_DH ** -0.5 * LOG2E)
    w_c = w_in.reshape(d, len(C_PATTERNS), 3, gw)
    projs = []
    for gi, (_, dil) in enumerate(C_PATTERNS):
        w_g = (w_c[:, gi] * scale).reshape(d, 3 * gw).astype(BF16)
        p = _norm_matmul(h, g, w_g, out_dtype=BF16, tm=512, tn=3 * gw, dil=dil,
                         name=f"dilated_in{gi}")
        projs.append(p.reshape(bsz, 3 * gw // LANE, t_len, LANE))
    return _dilated_attention(projs), w_out.astype(BF16)


def kernel(x, norm_mix, norm_ffn, norm_final, a_w_in, a_b_gate, a_conv_w, a_conv_b, a_wq, a_wk, a_wv, a_gn, a_skip, a_w_out, b_w_in, b_b_gate, b_pe_k, b_pe_v, b_phik_w1, b_phik_b1, b_phik_w2, b_phiv_w1, b_phiv_b1, b_phiv_w2, b_w_out, c_w_in, c_w_out, f_w_up, f_conv_w, f_conv_b, f_w_down):
    depth = norm_mix.shape[0]
    h = x
    for i in range(depth):
        kind, j = i % 3, i // 3
        mix = None
        if kind == 0:
            h = _mlstm_layer(h, norm_mix[i], a_w_in[j], a_b_gate[j], a_conv_w[j], a_conv_b[j],
                             a_wq[j], a_wk[j], a_wv[j], a_gn[j], a_skip[j], a_w_out[j])
        elif kind == 1:
            h = _nsa_layer(h, norm_mix[i], b_w_in[j], b_b_gate[j], b_pe_k[j], b_pe_v[j],
                           b_phik_w1[j], b_phik_b1[j], b_phik_w2[j],
                           b_phiv_w1[j], b_phiv_b1[j], b_phiv_w2[j], b_w_out[j])
        else:
            mix = _dilated_layer(h, norm_mix[i], c_w_in[j], c_w_out[j])
        h = _conv_ffn(h, norm_ffn[i], f_w_up[i].astype(BF16), f_conv_w[i], f_conv_b[i],
                      f_w_down[i].astype(BF16), norm_final, tm=512,
                      final_norm=(i == depth - 1), name="conv_ffn", mix=mix)
    return h
```
